```python
import math, functools
import jax, jax.numpy as jnp
from jax import lax
import numpy as np

D_MODEL = 1024
BATCH = 2
SEQ = 8192
DEPTH = 2
DEC_BATCH = 32
DEC_SEQ = 1
PAST_LEN = 16384
PAGE_SIZE = 128

N_EVEN = (DEPTH + 1) // 2
N_ODD = DEPTH // 2

D_FF = 2816
SSD_HEADS = 16
SSD_HEAD_DIM = 64
SSD_INNER = SSD_HEADS * SSD_HEAD_DIM
SSD_GROUPS = 2
SSD_STATE = 64
SSD_CONV = 4
SSD_CONV_DIM = SSD_INNER + 2 * SSD_GROUPS * SSD_STATE
SSD_CHUNK = 128
CF_CH = 512
CF_WIDTH = 31
MOBA_HEADS = 8
MOBA_HEAD_DIM = 64
MOBA_WIDTH = MOBA_HEADS * MOBA_HEAD_DIM
MOBA_BLOCK = 256
MOBA_TOPK = 3
MOBA_QBLOCK = 64
ROPE_DIM = MOBA_HEAD_DIM // 4
ROPE_THETA = 500000.0
HG_HEADS = 4
HG_K = 128
HG_V = 128
HG_WIDTH = HG_HEADS * HG_K
HG_CHUNK = 64

EVEN_IN = SSD_INNER + SSD_CONV_DIM + SSD_HEADS + 2 * CF_CH
EVEN_OUT = SSD_INNER + CF_CH
ODD_IN = 3 * MOBA_WIDTH + 4 * HG_WIDTH
ODD_OUT = MOBA_WIDTH + HG_HEADS * HG_V
NORM_EPS = 1e-6
NEG = -1e30

kernel_name = "hybrid_ssd_conformer_moba_hgrn2_step"


def rmsnorm(x, g):
    x32 = x.astype(jnp.float32)
    y = x32 * lax.rsqrt(jnp.mean(x32 * x32, axis=-1, keepdims=True) + NORM_EPS)
    return (y * g.astype(jnp.float32)).astype(x.dtype)


def swiglu(h, w_gu, w_down):
    g, u = jnp.split(h @ w_gu, 2, axis=-1)
    return (jax.nn.silu(g) * u) @ w_down


def causal_dwconv(u, buf, w):
    full = jnp.concatenate([buf.astype(u.dtype), u], axis=1)
    y = lax.conv_general_dilated(full, w[:, None, :].astype(u.dtype), (1,), 'VALID',
                                 dimension_numbers=('NWC', 'WIO', 'NWC'),
                                 feature_group_count=u.shape[-1])
    return y, full[:, full.shape[1] - (w.shape[0] - 1):]


def partial_rope(x, pos):
    half = ROPE_DIM // 2
    inv = ROPE_THETA ** (-jnp.arange(half, dtype=jnp.float32) / half)
    ang = pos.astype(jnp.float32)[:, None] * inv[None, :]
    cos, sin = jnp.cos(ang)[:, None, :], jnp.sin(ang)[:, None, :]
    x32 = x.astype(jnp.float32)
    x1, x2 = x32[..., :half], x32[..., half:ROPE_DIM]
    out = jnp.concatenate([x1 * cos - x2 * sin, x2 * cos + x1 * sin, x32[..., ROPE_DIM:]], axis=-1)
    return out.astype(x.dtype)


def _chunk_len(L, C):
    c = min(C, L)
    return c, -(-L // c)


def _to_chunks(a, c, n):
    pad = n * c - a.shape[1]
    a = jnp.pad(a, [(0, 0), (0, pad)] + [(0, 0)] * (a.ndim - 2))
    a = a.reshape(a.shape[0], n, c, *a.shape[2:])
    return jnp.moveaxis(a, 1, 0)


def _from_chunks(a, L):
    a = jnp.moveaxis(a, 0, 1)
    a = a.reshape(a.shape[0], a.shape[1] * a.shape[2], *a.shape[3:])
    return a[:, :L]


def ssd_scan(xh, dt, a, bm, cm, s0):
    L = xh.shape[1]
    c, n = _chunk_len(L, SSD_CHUNK)
    mask = jnp.tril(jnp.ones((c, c), bool))[None, :, :, None]

    def step(s, inp):
        xc, dtc, bc, cc = inp
        cum = jnp.cumsum(dtc * a, axis=1)
        seg = cum[:, :, None, :] - cum[:, None, :, :]
        decay = jnp.where(mask, jnp.exp(jnp.where(mask, seg, 0.0)), 0.0)
        w = jnp.einsum('bihn,bjhn->bijh', cc, bc) * decay * dtc[:, None, :, :]
        y = jnp.einsum('bijh,bjhp->bihp', w, xc)
        y = y + jnp.einsum('bihn,bhpn->bihp', cc, s) * jnp.exp(cum)[..., None]
        tail = jnp.exp(cum[:, -1:, :] - cum) * dtc
        s_new = jnp.exp(cum[:, -1, :])[:, :, None, None] * s + jnp.einsum('bjhn,bjh,bjhp->bhpn', bc, tail, xc)
        return s_new, y

    s_fin, ys = lax.scan(step, s0, tuple(_to_chunks(t, c, n) for t in (xh, dt, bm, cm)))
    return _from_chunks(ys, L), s_fin


def hgrn2_scan(q, k, v, lf, s0):
    L = q.shape[1]
    c, n = _chunk_len(L, HG_CHUNK)
    mask = jnp.tril(jnp.ones((c, c), bool))[None, :, :, None, None]

    def step(s, inp):
        qc, kc, vc, lfc = inp
        cum = jnp.cumsum(lfc, axis=1)
        seg = cum[:, :, None] - cum[:, None]
        decay = jnp.where(mask, jnp.exp(jnp.where(mask, seg, 0.0)), 0.0)
        att = jnp.einsum('bthk,bshk,btshk->btsh', qc, kc, decay)
        o = jnp.einsum('btsh,bshv->bthv', att, vc) + jnp.einsum('bthk,bhkv->bthv', qc * jnp.exp(cum), s)
        s_new = jnp.exp(cum[:, -1])[..., None] * s + jnp.einsum('bshk,bshv->bhkv', kc * jnp.exp(cum[:, -1:] - cum), vc)
        return s_new, o

    s_fin, os_ = lax.scan(step, s0, tuple(_to_chunks(t, c, n) for t in (q, k, v, lf)))
    return _from_chunks(os_, L), s_fin


def moba_prompt(q, k, v):
    Bsz, L, H, Dh = q.shape
    dtype = q.dtype
    q, k, v = (jnp.swapaxes(t, 1, 2).astype(jnp.float32) for t in (q, k, v))
    nb = -(-L // MOBA_BLOCK)
    pad = nb * MOBA_BLOCK - L
    kp = jnp.pad(k, ((0, 0), (0, 0), (0, pad), (0, 0)))
    vp = jnp.pad(v, ((0, 0), (0, 0), (0, pad), (0, 0)))
    kb = kp.reshape(Bsz, H, nb, MOBA_BLOCK, Dh)
    vb = vp.reshape(Bsz, H, nb, MOBA_BLOCK, Dh)
    kmean = kb.mean(axis=3)
    topk = min(MOBA_TOPK, nb)
    qb = math.gcd(L, MOBA_QBLOCK)
    bi = jnp.arange(Bsz)[:, None, None, None]
    hi = jnp.arange(H)[None, :, None, None]
    scale = Dh ** -0.5

    def one(start):
        qs = lax.dynamic_slice_in_dim(q, start, qb, axis=2)
        qpos = start + jnp.arange(qb)
        bt = start // MOBA_BLOCK
        gate = jnp.einsum('bhqd,bhnd->bhqn', qs, kmean)
        gate = jnp.where(jnp.arange(nb) < bt, gate, -jnp.inf)
        _, idx = lax.top_k(gate, topk)
        valid = idx < bt
        kg = kb[bi, hi, idx]
        vg = vb[bi, hi, idx]
        s_sel = jnp.einsum('bhqd,bhqnkd->bhqnk', qs, kg) * scale
        s_sel = jnp.where(valid[..., None], s_sel, NEG)
        ko = lax.dynamic_slice_in_dim(kp, bt * MOBA_BLOCK, MOBA_BLOCK, axis=2)
        vo = lax.dynamic_slice_in_dim(vp, bt * MOBA_BLOCK, MOBA_BLOCK, axis=2)
        kpos = bt * MOBA_BLOCK + jnp.arange(MOBA_BLOCK)
        s_own = jnp.einsum('bhqd,bhkd->bhqk', qs, ko) * scale
        s_own = jnp.where(kpos[None, :] <= qpos[:, None], s_own, NEG)
        s = jnp.concatenate([s_sel.reshape(Bsz, H, qb, topk * MOBA_BLOCK), s_own], axis=-1)
        p = jax.nn.softmax(s, axis=-1)
        p_sel = p[..., :topk * MOBA_BLOCK].reshape(Bsz, H, qb, topk, MOBA_BLOCK)
        p_own = p[..., topk * MOBA_BLOCK:]
        return jnp.einsum('bhqnk,bhqnkd->bhqd', p_sel, vg) + jnp.einsum('bhqk,bhkd->bhqd', p_own, vo)

    out = lax.map(one, jnp.arange(L // qb, dtype=jnp.int32) * qb)
    out = jnp.moveaxis(out, 0, 2).reshape(Bsz, H, L, Dh)
    return jnp.swapaxes(out, 1, 2).astype(dtype)


def moba_sample(q, k, v, pos, cache_k, cache_v, page_table):
    Bsz, S, H, Dh = q.shape
    dtype = q.dtype
    n_pages = page_table.shape[1]
    ppb = MOBA_BLOCK // PAGE_SIZE
    npn = -(-S // PAGE_SIZE)
    q, k, v = (jnp.swapaxes(t, 1, 2).astype(jnp.float32) for t in (q, k, v))
    pad = npn * PAGE_SIZE - S
    kn = jnp.pad(k, ((0, 0), (0, 0), (0, pad), (0, 0))).reshape(Bsz, H, npn, PAGE_SIZE, Dh)
    vn = jnp.pad(v, ((0, 0), (0, 0), (0, pad), (0, 0))).reshape(Bsz, H, npn, PAGE_SIZE, Dh)
    past_sum = cache_k[page_table].astype(jnp.float32).sum(axis=3)
    page_sum = jnp.concatenate([jnp.swapaxes(past_sum, 1, 2), kn.sum(axis=3)], axis=2)
    npt = n_pages + npn
    nbt = -(-npt // ppb)
    page_sum = jnp.pad(page_sum, ((0, 0), (0, 0), (0, nbt * ppb - npt), (0, 0)))
    kmean = page_sum.reshape(Bsz, H, nbt, ppb, Dh).sum(axis=3) / MOBA_BLOCK
    bt = pos // MOBA_BLOCK
    gate = jnp.einsum('bhsd,bhnd->bhsn', q, kmean)
    gate = jnp.where(jnp.arange(nbt)[None, :] < bt[:, None], gate, -jnp.inf)
    topk = min(MOBA_TOPK, nbt)
    _, idx = lax.top_k(gate, topk)
    valid = idx < bt[:, None]
    blocks = jnp.concatenate([idx, jnp.broadcast_to(bt[None, None, :, None], (Bsz, H, S, 1))], axis=-1)
    bvalid = jnp.concatenate([valid, jnp.ones((Bsz, H, S, 1), bool)], axis=-1)
    lp = (blocks[..., None] * ppb + jnp.arange(ppb)).reshape(Bsz, H, S, (topk + 1) * ppb)
    pvalid = jnp.repeat(bvalid, ppb, axis=-1)
    in_past = (lp < n_pages)[..., None, None]
    bi = jnp.arange(Bsz)[:, None, None, None]
    hi = jnp.arange(H)[None, :, None, None]
    phys = page_table[bi, jnp.clip(lp, 0, n_pages - 1)]
    newp = jnp.clip(lp - n_pages, 0, npn - 1)
    kg = jnp.where(in_past, cache_k[phys, hi].astype(jnp.float32), kn[bi, hi, newp])
    vg = jnp.where(in_past, cache_v[phys, hi].astype(jnp.float32), vn[bi, hi, newp])
    kpos = lp[..., None] * PAGE_SIZE + jnp.arange(PAGE_SIZE)
    mask = pvalid[..., None] & (kpos <= pos[None, None, :, None, None])
    s = jnp.einsum('bhsd,bhsnpd->bhsnp', q, kg) * (Dh ** -0.5)
    s = jnp.where(mask, s, NEG)
    nps = lp.shape[-1]
    p = jax.nn.softmax(s.reshape(Bsz, H, S, nps * PAGE_SIZE), axis=-1).reshape(Bsz, H, S, nps, PAGE_SIZE)
    out = jnp.einsum('bhsnp,bhsnpd->bhsd', p, vg)
    return jnp.swapaxes(out, 1, 2).astype(dtype)


def even_mixer(h, w_in, conv_w, conv_b, dt_bias, a_log, d_skip, ssd_norm, cf_w, cf_b, cf_g, cf_beta, w_out,
               s_ssd, buf_ssd, buf_cf):
    f32 = jnp.float32
    Bsz, L, _ = h.shape
    proj = h @ w_in
    z, xbc, dt_raw, glu = jnp.split(proj, [SSD_INNER, SSD_INNER + SSD_CONV_DIM,
                                           SSD_INNER + SSD_CONV_DIM + SSD_HEADS], axis=-1)
    xbc_c, buf_ssd_new = causal_dwconv(xbc, buf_ssd, conv_w)
    xbc_c = jax.nn.silu(xbc_c.astype(f32) + conv_b.astype(f32))
    xs, bm, cm = jnp.split(xbc_c, [SSD_INNER, SSD_INNER + SSD_GROUPS * SSD_STATE], axis=-1)
    xs = xs.reshape(Bsz, L, SSD_HEADS, SSD_HEAD_DIM)
    rep = SSD_HEADS // SSD_GROUPS
    bm = jnp.repeat(bm.reshape(Bsz, L, SSD_GROUPS, SSD_STATE), rep, axis=2)
    cm = jnp.repeat(cm.reshape(Bsz, L, SSD_GROUPS, SSD_STATE), rep, axis=2)
    dt = jax.nn.softplus(dt_raw.astype(f32) + dt_bias.astype(f32))
    a = -jnp.exp(a_log.astype(f32))
    y, s_new = ssd_scan(xs, dt, a, bm, cm, s_ssd.astype(f32))
    y = y + d_skip.astype(f32)[:, None] * xs
    y = y.reshape(Bsz, L, SSD_INNER) * jax.nn.silu(z.astype(f32))
    y = rmsnorm(y.reshape(Bsz, L, SSD_GROUPS, SSD_INNER // SSD_GROUPS),
                ssd_norm.reshape(SSD_GROUPS, SSD_INNER // SSD_GROUPS)).reshape(Bsz, L, SSD_INNER)
    ga, gb = jnp.split(glu, 2, axis=-1)
    u = ga * jax.nn.sigmoid(gb)
    c, buf_cf_new = causal_dwconv(u, buf_cf, cf_w)
    c32 = c.astype(f32) + cf_b.astype(f32)
    mu = jnp.mean(c32, axis=-1, keepdims=True)
    var = jnp.mean(jnp.square(c32 - mu), axis=-1, keepdims=True)
    c32 = (c32 - mu) * lax.rsqrt(var + NORM_EPS) * cf_g.astype(f32) + cf_beta.astype(f32)
    c32 = jax.nn.silu(c32)
    out = jnp.concatenate([y.astype(h.dtype), c32.astype(h.dtype)], axis=-1) @ w_out
    return out, s_new.astype(s_ssd.dtype), buf_ssd_new, buf_cf_new


def odd_mixer(h, pos, w_in, lb, g_norm, w_out, s_hg, attend):
    f32 = jnp.float32
    Bsz, L, _ = h.shape
    proj = h @ w_in
    splits = [MOBA_WIDTH * i for i in (1, 2, 3)] + [3 * MOBA_WIDTH + HG_WIDTH * j for j in (1, 2, 3)]
    q, k, v, hq, hf, hi_, hg = jnp.split(proj, splits, axis=-1)
    q = partial_rope(q.reshape(Bsz, L, MOBA_HEADS, MOBA_HEAD_DIM), pos)
    k = partial_rope(k.reshape(Bsz, L, MOBA_HEADS, MOBA_HEAD_DIM), pos)
    v = v.reshape(Bsz, L, MOBA_HEADS, MOBA_HEAD_DIM)
    att = attend(q, k, v).reshape(Bsz, L, MOBA_WIDTH)
    f = lb + (1.0 - lb) * jax.nn.sigmoid(hf.astype(f32))
    shp = (Bsz, L, HG_HEADS, HG_K)
    o, s_new = hgrn2_scan(jax.nn.silu(hq.astype(f32)).reshape(shp), (1.0 - f).reshape(shp),
                          hi_.astype(f32).reshape(Bsz, L, HG_HEADS, HG_V), jnp.log(f).reshape(shp),
                          s_hg.astype(f32))
    o = rmsnorm(o, g_norm.reshape(HG_HEADS, HG_V)) * jax.nn.silu(hg.astype(f32).reshape(Bsz, L, HG_HEADS, HG_V))
    out = jnp.concatenate([att.astype(h.dtype), o.reshape(Bsz, L, HG_HEADS * HG_V).astype(h.dtype)], axis=-1) @ w_out
    return out, s_new.astype(s_hg.dtype), jnp.swapaxes(k, 1, 2), jnp.swapaxes(v, 1, 2)


def setup_inputs(seed: int = 0) -> dict:
    key = jax.random.key(seed)
    keys = iter(jax.random.split(key, 48))
    f32 = jnp.float32

    def nrm(shape, scale):
        return jax.random.normal(next(keys), shape, f32) * scale

    def gain(shape):
        return 1.0 + nrm(shape, 0.02)

    n_pages = PAST_LEN // PAGE_SIZE
    n_used = DEC_BATCH * n_pages
    n_pool = n_used + (n_used + 3) // 4
    page_table = jax.random.permutation(next(keys), n_pool)[:n_used].reshape(DEC_BATCH, n_pages).astype(jnp.int32)
    dt0 = jnp.exp(jax.random.uniform(next(keys), (N_EVEN, SSD_HEADS), f32, math.log(1e-3), math.log(1e-1)))
    a0 = jax.random.uniform(next(keys), (N_EVEN, SSD_HEADS), f32, 1.0, 16.0)
    return {
        'x_prompt': nrm((BATCH, SEQ, D_MODEL), 1.0),
        'x_sample': nrm((DEC_BATCH, DEC_SEQ, D_MODEL), 1.0),
        'state_ssd': nrm((N_EVEN, DEC_BATCH, SSD_HEADS, SSD_HEAD_DIM, SSD_STATE), 0.5),
        'state_ssd_conv': nrm((N_EVEN, DEC_BATCH, SSD_CONV - 1, SSD_CONV_DIM), 1.0),
        'state_cf_conv': nrm((N_EVEN, DEC_BATCH, CF_WIDTH - 1, CF_CH), 0.5),
        'cache_k': nrm((N_ODD, n_pool, MOBA_HEADS, PAGE_SIZE, MOBA_HEAD_DIM), 1.0),
        'cache_v': nrm((N_ODD, n_pool, MOBA_HEADS, PAGE_SIZE, MOBA_HEAD_DIM), 1.0),
        'page_table': page_table,
        'state_hg': nrm((N_ODD, DEC_BATCH, HG_HEADS, HG_K, HG_V), 0.5),
        'ffn1_norm': gain((DEPTH, D_MODEL)),
        'ffn1_w_gu': nrm((DEPTH, D_MODEL, 2 * D_FF), D_MODEL ** -0.5),
        'ffn1_w_down': nrm((DEPTH, D_FF, D_MODEL), D_FF ** -0.5),
        'mix_norm': gain((DEPTH, D_MODEL)),
        'ffn2_norm': gain((DEPTH, D_MODEL)),
        'ffn2_w_gu': nrm((DEPTH, D_MODEL, 2 * D_FF), D_MODEL ** -0.5),
        'ffn2_w_down': nrm((DEPTH, D_FF, D_MODEL), D_FF ** -0.5),
        'final_norm': gain((D_MODEL,)),
        'even_w_in': nrm((N_EVEN, D_MODEL, EVEN_IN), D_MODEL ** -0.5),
        'ssd_conv_w': nrm((N_EVEN, SSD_CONV, SSD_CONV_DIM), SSD_CONV ** -0.5),
        'ssd_conv_b': nrm((N_EVEN, SSD_CONV_DIM), 0.02),
        'ssd_dt_bias': jnp.log(jnp.expm1(dt0)),
        'ssd_a_log': jnp.log(a0),
        'ssd_d': gain((N_EVEN, SSD_HEADS)),
        'ssd_norm': gain((N_EVEN, SSD_INNER)),
        'cf_dw_w': nrm((N_EVEN, CF_WIDTH, CF_CH), CF_WIDTH ** -0.5),
        'cf_dw_b': nrm((N_EVEN, CF_CH), 0.02),
        'cf_ln_g': gain((N_EVEN, CF_CH)),
        'cf_ln_b': nrm((N_EVEN, CF_CH), 0.02),
        'even_w_out': nrm((N_EVEN, EVEN_OUT, D_MODEL), EVEN_OUT ** -0.5),
        'odd_w_in': nrm((N_ODD, D_MODEL, ODD_IN), D_MODEL ** -0.5),
        'hg_lower_bound': nrm((DEPTH, HG_WIDTH), 0.5),
        'hg_norm': gain((N_ODD, HG_WIDTH)),
        'odd_w_out': nrm((N_ODD, ODD_OUT, D_MODEL), ODD_OUT ** -0.5),
    }


def reference(x_prompt, x_sample, state_ssd, state_ssd_conv, state_cf_conv, cache_k, cache_v, page_table, state_hg,
              ffn1_norm, ffn1_w_gu, ffn1_w_down, mix_norm, ffn2_norm, ffn2_w_gu, ffn2_w_down, final_norm,
              even_w_in, ssd_conv_w, ssd_conv_b, ssd_dt_bias, ssd_a_log, ssd_d, ssd_norm,
              cf_dw_w, cf_dw_b, cf_ln_g, cf_ln_b, even_w_out,
              odd_w_in, hg_lower_bound, hg_norm, odd_w_out):
    lb_cum = jnp.cumsum(jax.nn.softmax(hg_lower_bound.astype(jnp.float32), axis=0), axis=0)

    def forward(x, pos, ssd0, ssdbuf0, cfbuf0, hg0, attend):
        ssd_l, ssdb_l, cfb_l, k_l, v_l, hg_l = [], [], [], [], [], []
        for l in range(DEPTH):
            x = x + 0.5 * swiglu(rmsnorm(x, ffn1_norm[l]), ffn1_w_gu[l], ffn1_w_down[l])
            h = rmsnorm(x, mix_norm[l])
            if l % 2 == 0:
                e = l // 2
                out, s, b1, b2 = even_mixer(h, even_w_in[e], ssd_conv_w[e], ssd_conv_b[e], ssd_dt_bias[e],
                                            ssd_a_log[e], ssd_d[e], ssd_norm[e], cf_dw_w[e], cf_dw_b[e],
                                            cf_ln_g[e], cf_ln_b[e], even_w_out[e], ssd0[e], ssdbuf0[e], cfbuf0[e])
                ssd_l.append(s)
                ssdb_l.append(b1)
                cfb_l.append(b2)
            else:
                o = l // 2
                out, s, kr, vr = odd_mixer(h, pos, odd_w_in[o], lb_cum[l] - lb_cum[0], hg_norm[o], odd_w_out[o],
                                           hg0[o], functools.partial(attend, o))
                hg_l.append(s)
                k_l.append(kr)
                v_l.append(vr)
            x = x + out
            x = x + 0.5 * swiglu(rmsnorm(x, ffn2_norm[l]), ffn2_w_gu[l], ffn2_w_down[l])
        return (rmsnorm(x, final_norm), jnp.stack(ssd_l), jnp.stack(ssdb_l), jnp.stack(cfb_l),
                jnp.stack(k_l), jnp.stack(v_l), jnp.stack(hg_l))

    bp, lp_ = x_prompt.shape[0], x_prompt.shape[1]
    dt_ = x_prompt.dtype
    pos_p = jnp.arange(lp_, dtype=jnp.int32)

    def attend_prompt(o, q, k, v):
        return moba_prompt(q, k, v)

    (y_prompt, ssd_p, ssdc_p, cfc_p, k_p, v_p, hg_p) = forward(
        x_prompt, pos_p,
        jnp.zeros((N_EVEN, bp, SSD_HEADS, SSD_HEAD_DIM, SSD_STATE), dt_),
        jnp.zeros((N_EVEN, bp, SSD_CONV - 1, SSD_CONV_DIM), dt_),
        jnp.zeros((N_EVEN, bp, CF_WIDTH - 1, CF_CH), dt_),
        jnp.zeros((N_ODD, bp, HG_HEADS, HG_K, HG_V), dt_),
        attend_prompt)

    past = page_table.shape[1] * PAGE_SIZE
    pos_s = past + jnp.arange(x_sample.shape[1], dtype=jnp.int32)

    def attend_sample(o, q, k, v):
        return moba_sample(q, k, v, pos_s, cache_k[o], cache_v[o], page_table)

    (y_sample, ssd_s, ssdc_s, cfc_s, k_s, v_s, hg_s) = forward(
        x_sample, pos_s, state_ssd, state_ssd_conv, state_cf_conv, state_hg, attend_sample)

    return (y_prompt, y_sample, ssd_p, ssd_s, ssdc_p, ssdc_s, cfc_p, cfc_s, k_p, k_s, v_p, v_s, hg_p, hg_s)
```

```python
import functools
import math

import numpy as np
import jax
import jax.numpy as jnp
from jax import lax
from jax.experimental import pallas as pl
from jax.experimental.pallas import tpu as pltpu

F32 = jnp.float32
BF16 = jnp.bfloat16

SSD_HEADS = 16
SSD_HEAD_DIM = 64
SSD_INNER = SSD_HEADS * SSD_HEAD_DIM
SSD_GROUPS = 2
SSD_STATE = 64
SSD_CONV = 4
SSD_CONV_DIM = SSD_INNER + 2 * SSD_GROUPS * SSD_STATE
CF_CH = 512
CF_WIDTH = 31
MOBA_HEADS = 8
MOBA_HEAD_DIM = 64
MOBA_WIDTH = MOBA_HEADS * MOBA_HEAD_DIM
MOBA_BLOCK = 256
MOBA_TOPK = 3
ROPE_DIM = MOBA_HEAD_DIM // 4
ROPE_THETA = 500000.0
HG_HEADS = 4
HG_K = 128
HG_V = 128
HG_WIDTH = HG_HEADS * HG_K
PAGE_SIZE = 128
NORM_EPS = 1e-6
NEG = -1e30

LANES = 128
SSD_CHUNK = 128
TILE = 256
VMEM_LIMIT = 56 * 1024 * 1024


def _sigmoid(x):
    return 1.0 / (1.0 + jnp.exp(-x))


def _silu(x):
    return x * _sigmoid(x)


def _softplus(x):
    return jnp.maximum(x, 0.0) + jnp.log1p(jnp.exp(-jnp.abs(x)))


def _rms(x, g):
    return x * lax.rsqrt(jnp.mean(x * x, axis=-1, keepdims=True) + NORM_EPS) * g


def _dot(a, b):
    return jnp.dot(a.astype(BF16), b.astype(BF16), preferred_element_type=F32)


def _dot_nt(a, b):
    return lax.dot_general(a.astype(BF16), b.astype(BF16), (((1,), (1,)), ((), ())),
                           preferred_element_type=F32)


def _split3(a):
    hi = a.astype(BF16)
    r = a - hi.astype(F32)
    mid = r.astype(BF16)
    lo = (r - mid.astype(F32)).astype(BF16)
    return hi, mid, lo


def _dot3_l(a, m):
    hi, mid, lo = _split3(a)
    return (jnp.dot(hi, m, preferred_element_type=F32) + jnp.dot(mid, m, preferred_element_type=F32)
            + jnp.dot(lo, m, preferred_element_type=F32))


def _dot3_r(m, parts):
    hi, mid, lo = parts
    return (jnp.dot(m, hi, preferred_element_type=F32) + jnp.dot(m, mid, preferred_element_type=F32)
            + jnp.dot(m, lo, preferred_element_type=F32))


def _const_spec(shape):
    n = len(shape)
    return pl.BlockSpec(shape, lambda *_: (0,) * n, pipeline_mode=pl.Buffered(1))


def _params(sem):
    return pltpu.CompilerParams(dimension_semantics=sem, vmem_limit_bytes=VMEM_LIMIT)


def _ffn_body(*refs, pre, final):
    it = iter(refs)
    x_ref = next(it)
    if pre:
        a_ref, b_ref, wa_ref, wb_ref = next(it), next(it), next(it), next(it)
    g_ref, wg_ref, wu_ref, wd_ref = next(it), next(it), next(it), next(it)
    if final:
        fg_ref = next(it)
    o_ref = next(it)
    x = x_ref[...]
    if pre:
        x = x + _dot(a_ref[...], wa_ref[...]) + _dot(b_ref[...], wb_ref[...])
    hb = _rms(x, g_ref[...]).astype(BF16)
    g = jnp.dot(hb, wg_ref[...], preferred_element_type=F32)
    u = jnp.dot(hb, wu_ref[...], preferred_element_type=F32)
    act = (_silu(g) * u).astype(BF16)
    y = x + 0.5 * jnp.dot(act, wd_ref[...], preferred_element_type=F32)
    if final:
        y = _rms(y, fg_ref[...])
    o_ref[...] = y


def _ffn(x, w, *, tm, pre=None, final_g=None):
    m, d = x.shape
    g, wg, wu, wd = w
    row = lambda c: pl.BlockSpec((tm, c), lambda i: (i, 0))
    args, specs = [x], [row(d)]
    if pre is not None:
        a, b, wa, wb = pre
        args += [a, b, wa, wb]
        specs += [row(a.shape[1]), row(b.shape[1]), _const_spec(wa.shape), _const_spec(wb.shape)]
    args += [g, wg, wu, wd]
    specs += [_const_spec(g.shape), _const_spec(wg.shape), _const_spec(wu.shape), _const_spec(wd.shape)]
    if final_g is not None:
        args.append(final_g)
        specs.append(_const_spec(final_g.shape))
    return pl.pallas_call(
        functools.partial(_ffn_body, pre=pre is not None, final=final_g is not None),
        grid=(m // tm,), in_specs=specs, out_specs=row(d),
        out_shape=jax.ShapeDtypeStruct((m, d), F32),
        compiler_params=_params(("parallel",)), name="ffn")(*args)


def _even_prompt_body(x_ref, g_ref, wz_ref, wxbc_ref, wdt_ref, wga_ref, wgb_ref,
                      cw_ref, cb_ref, dtb_ref, alog_ref, dfull_ref, sn_ref, sel_ref,
                      cfw_ref, cfb_ref, cfg_ref, cfbeta_ref, woy_ref, woc_ref,
                      xo_ref, st_ref, xtail_ref, utail_ref,
                      xbuf, ubuf, s_ref, ybuf, *, nt):
    t = pl.program_id(1)
    T = TILE
    C = SSD_CHUNK

    @pl.when(t == 0)
    def _():
        xbuf[0:8, :] = jnp.zeros((8, SSD_CONV_DIM), F32)
        ubuf[0:32, :] = jnp.zeros((32, CF_CH), F32)
        s_ref[...] = jnp.zeros(s_ref.shape, F32)

    x = x_ref[0]
    hb = _rms(x, g_ref[...]).astype(BF16)
    z = jnp.dot(hb, wz_ref[...], preferred_element_type=F32)
    xbc = jnp.dot(hb, wxbc_ref[...], preferred_element_type=F32)
    dtr = jnp.dot(hb, wdt_ref[...], preferred_element_type=F32)
    ga = jnp.dot(hb, wga_ref[...], preferred_element_type=F32)
    gb = jnp.dot(hb, wgb_ref[...], preferred_element_type=F32)

    xbuf[8:8 + T, :] = xbc
    acc = cw_ref[0:1, :] * xbuf[5:5 + T, :]
    for k in range(1, SSD_CONV):
        acc = acc + cw_ref[k:k + 1, :] * xbuf[5 + k:5 + k + T, :]
    tail8 = xbuf[T:T + 8, :]
    xtail_ref[0] = tail8
    xbuf[0:8, :] = tail8
    xc = _silu(acc + cb_ref[...])
    xs = xc[:, 0:SSD_INNER]
    bm = xc[:, SSD_INNER:SSD_INNER + LANES]
    cm = xc[:, SSD_INNER + LANES:SSD_INNER + 2 * LANES]
    dt = _softplus(dtr + dtb_ref[...])
    dta = dt * (-jnp.exp(alog_ref[...]))
    sel = sel_ref[...]

    ri = lax.broadcasted_iota(jnp.int32, (C, C), 0)
    ci = lax.broadcasted_iota(jnp.int32, (C, C), 1)
    tri = ri >= ci
    trib = tri.astype(BF16)
    low = ci < SSD_HEAD_DIM

    for c in range(T // C):
        r0 = c * C
        xs_c, bm_c, cm_c = xs[r0:r0 + C], bm[r0:r0 + C], cm[r0:r0 + C]
        dt_c = dt[r0:r0 + C]
        cum = _dot3_r(trib, _split3(dta[r0:r0 + C]))
        cum_t = cum.T
        dt_t = dt_c.T
        cum_last = cum[C - 1:C, :]
        ecum_full = _dot3_l(jnp.exp(cum), sel)
        tail_full = _dot3_l(jnp.exp(cum_last - cum) * dt_c, sel)
        dlast_full = _dot3_l(jnp.broadcast_to(jnp.exp(cum_last), (8, LANES)), sel)[0:1]
        bm_t = bm_c.T
        cmb = cm_c.astype(BF16)
        for grp in range(SSD_GROUPS):
            bm_tg = jnp.where((ri // SSD_STATE) == grp, bm_t, 0.0).astype(BF16)
            gmat = jnp.dot(cmb, bm_tg, preferred_element_type=F32)
            pairs = SSD_HEADS // SSD_GROUPS // 2
            for pp in range(pairs):
                p = grp * pairs + pp
                lanes = slice(p * LANES, (p + 1) * LANES)
                xp = xs_c[:, lanes]
                xpb = xp.astype(BF16)
                ys = []
                for e in range(2):
                    h = 2 * p + e
                    seg = cum[:, h:h + 1] - cum_t[h:h + 1, :]
                    wm = gmat * jnp.exp(jnp.where(tri, seg, NEG)) * dt_t[h:h + 1, :]
                    ys.append(jnp.dot(wm.astype(BF16), xpb, preferred_element_type=F32))
                yp = jnp.where(low, ys[0], ys[1])
                sp = s_ref[p]
                yp = yp + jnp.dot(cmb, sp.astype(BF16), preferred_element_type=F32) * ecum_full[:, lanes]
                s_ref[p] = dlast_full[:, lanes] * sp + jnp.dot(
                    bm_tg, (xp * tail_full[:, lanes]).astype(BF16), preferred_element_type=F32)
                ybuf[r0:r0 + C, lanes] = yp

    @pl.when(t == nt - 1)
    def _():
        for p in range(SSD_HEADS // 2):
            st_ref[0, p] = s_ref[p].T

    y = (ybuf[...] + dfull_ref[...] * xs) * _silu(z)
    gw = SSD_INNER // SSD_GROUPS
    yn = []
    for grp in range(SSD_GROUPS):
        yg = y[:, grp * gw:(grp + 1) * gw]
        yn.append(_rms(yg, sn_ref[:, grp * gw:(grp + 1) * gw]).astype(BF16))

    u = ga * _sigmoid(gb)
    ubuf[32:32 + T, :] = u
    cacc = cfw_ref[0:1, :] * ubuf[2:2 + T, :]
    for k in range(1, CF_WIDTH):
        cacc = cacc + cfw_ref[k:k + 1, :] * ubuf[2 + k:2 + k + T, :]
    tail32 = ubuf[T:T + 32, :]
    utail_ref[0] = tail32
    ubuf[0:32, :] = tail32
    c32 = cacc + cfb_ref[...]
    mu = jnp.mean(c32, axis=-1, keepdims=True)
    var = jnp.mean(jnp.square(c32 - mu), axis=-1, keepdims=True)
    c32 = _silu((c32 - mu) * lax.rsqrt(var + NORM_EPS) * cfg_ref[...] + cfbeta_ref[...])

    out = jnp.dot(c32.astype(BF16), woc_ref[...], preferred_element_type=F32)
    for grp in range(SSD_GROUPS):
        out = out + jnp.dot(yn[grp], woy_ref[grp * gw:(grp + 1) * gw, :], preferred_element_type=F32)
    xo_ref[0] = x + out


def _even_prompt(x, w):
    b, l, d = x.shape
    nt = l // TILE
    consts = [w[k] for k in ("g", "wz", "wxbc", "wdt", "wga", "wgb", "cw", "cb", "dtb", "alog", "dfull", "sn",
                             "sel", "cfw", "cfb", "cfg", "cfbeta", "woy", "woc")]
    out_shape = (jax.ShapeDtypeStruct((b, l, d), F32),
                 jax.ShapeDtypeStruct((b, SSD_HEADS // 2, LANES, LANES), F32),
                 jax.ShapeDtypeStruct((b, 8, SSD_CONV_DIM), F32),
                 jax.ShapeDtypeStruct((b, 32, CF_CH), F32))
    out_specs = (pl.BlockSpec((1, TILE, d), lambda i, t: (i, t, 0)),
                 pl.BlockSpec((1, SSD_HEADS // 2, LANES, LANES), lambda i, t: (i, 0, 0, 0)),
                 pl.BlockSpec((1, 8, SSD_CONV_DIM), lambda i, t: (i, 0, 0)),
                 pl.BlockSpec((1, 32, CF_CH), lambda i, t: (i, 0, 0)))
    xo, st, xtail, utail = pl.pallas_call(
        functools.partial(_even_prompt_body, nt=nt),
        grid=(b, nt),
        in_specs=[pl.BlockSpec((1, TILE, d), lambda i, t: (i, t, 0))] + [_const_spec(c.shape) for c in consts],
        out_specs=out_specs, out_shape=out_shape,
        scratch_shapes=[pltpu.VMEM((TILE + 8, SSD_CONV_DIM), F32), pltpu.VMEM((TILE + 32, CF_CH), F32),
                        pltpu.VMEM((SSD_HEADS // 2, LANES, LANES), F32), pltpu.VMEM((TILE, SSD_INNER), F32)],
        compiler_params=_params(("arbitrary", "arbitrary")), name="even_prompt")(x, *consts)
    half = SSD_HEADS // 2 // SSD_GROUPS
    parts = []
    for p in range(SSD_HEADS // 2):
        grp = p // half
        for e in range(2):
            parts.append(st[:, p, e * 64:(e + 1) * 64, grp * 64:(grp + 1) * 64])
    state = jnp.stack(parts, axis=1)
    return xo, state, xtail[:, 8 - (SSD_CONV - 1):], utail[:, 32 - (CF_WIDTH - 1):]


def _hg_levels(T):
    t = np.arange(T)
    le = (t[None, :] <= t[:, None]).astype(np.float32)
    ds, ms = [], []
    m = 1
    while m < T:
        rb = (t // (2 * m)) * 2 * m + m - 1
        ds.append(le - (t[None, :] <= rb[:, None]).astype(np.float32))
        same = (t[:, None] // (2 * m)) == (t[None, :] // (2 * m))
        ms.append((same & ((t[:, None] % (2 * m)) >= m) & ((t[None, :] % (2 * m)) < m)).astype(np.float32))
        m *= 2
    return (jnp.asarray(le, BF16), jnp.asarray(np.stack(ds), BF16), jnp.asarray(np.stack(ms), BF16))


def _rope_tables(pos):
    half = ROPE_DIM // 2
    inv = ROPE_THETA ** (-jnp.arange(half, dtype=F32) / half)
    ang = pos.astype(F32)[:, None] * inv[None, :]
    cos, sin = jnp.cos(ang), jnp.sin(ang)
    n = pos.shape[0]
    one = jnp.ones((n, MOBA_HEAD_DIM - ROPE_DIM), F32)
    zero = jnp.zeros((n, MOBA_HEAD_DIM - ROPE_DIM), F32)
    zh = jnp.zeros((n, half), F32)
    c = jnp.concatenate([cos, cos, one], axis=1)
    s1 = jnp.concatenate([-sin, zh, zero], axis=1)
    s2 = jnp.concatenate([zh, sin, zero], axis=1)
    return tuple(jnp.concatenate([a, a], axis=1) for a in (c, s1, s2))


def _rope(x, c, s1, s2):
    outs = []
    for j in range(x.shape[1] // LANES):
        blk = x[:, j * LANES:(j + 1) * LANES]
        outs.append(blk * c + pltpu.roll(blk, LANES - ROPE_DIM // 2, 1) * s1 + pltpu.roll(blk, ROPE_DIM // 2, 1) * s2)
    return outs


def _hg_lower_bound(lb_ref, layer):
    a = lb_ref[...]
    mx = jnp.max(a, axis=0, keepdims=True)
    e = jnp.exp(a - mx)
    return jnp.sum(e[1:layer + 1], axis=0, keepdims=True) / jnp.sum(e, axis=0, keepdims=True)


def _odd_prompt_body(x_ref, g_ref, win_ref, c_ref, s1_ref, s2_ref, lb_ref, gn_ref, tril_ref, dm_ref, mm_ref,
                     q_ref, kb_ref, vt_ref, km_ref, ko_ref, vo_ref, o_ref, hs_ref,
                     st_ref, *, nt, layer):
    t = pl.program_id(1)
    T = TILE
    W = MOBA_WIDTH

    @pl.when(t == 0)
    def _():
        st_ref[...] = jnp.zeros(st_ref.shape, F32)

    x = x_ref[0]
    hb = _rms(x, g_ref[...]).astype(BF16)
    proj = jnp.dot(hb, win_ref[...], preferred_element_type=F32)

    c, s1, s2 = c_ref[...], s1_ref[...], s2_ref[...]
    qb = _rope(proj[:, 0:W], c, s1, s2)
    kb = _rope(proj[:, W:2 * W], c, s1, s2)
    v = proj[:, 2 * W:3 * W]
    for j in range(W // LANES):
        lanes = slice(j * LANES, (j + 1) * LANES)
        q_ref[0, :, lanes] = qb[j]
        kb_ref[0, :, lanes] = kb[j].astype(BF16)
        km_ref[0, 0, :, lanes] = jnp.mean(kb[j], axis=0, keepdims=True)
        vj = v[:, lanes]
        vt_ref[0, lanes, :] = vj.T.astype(BF16)
        for e in range(2):
            ko_ref[0, 2 * j + e] = kb[j][:, e * 64:(e + 1) * 64]
            vo_ref[0, 2 * j + e] = vj[:, e * 64:(e + 1) * 64]

    lb = _hg_lower_bound(lb_ref, layer)
    hq = _silu(proj[:, 3 * W:3 * W + HG_WIDTH])
    f = lb + (1.0 - lb) * _sigmoid(proj[:, 3 * W + HG_WIDTH:3 * W + 2 * HG_WIDTH])
    hv = proj[:, 3 * W + 2 * HG_WIDTH:3 * W + 3 * HG_WIDTH]
    hgate = proj[:, 3 * W + 3 * HG_WIDTH:3 * W + 4 * HG_WIDTH]
    kk = 1.0 - f
    lf3 = _split3(jnp.log(f))
    cum = _dot3_r(tril_ref[...], lf3)
    nlev = dm_ref.shape[0]
    ri = lax.broadcasted_iota(jnp.int32, (T, T), 0)
    ci = lax.broadcasted_iota(jnp.int32, (T, T), 1)
    eye = ri == ci
    amat = []
    for h in range(HG_HEADS):
        lanes = slice(h * HG_K, (h + 1) * HG_K)
        diag = jnp.sum(hq[:, lanes] * kk[:, lanes], axis=1, keepdims=True)
        amat.append(jnp.where(eye, diag, 0.0))
    for lev in range(nlev):
        e_all = jnp.exp(-jnp.abs(_dot3_r(dm_ref[lev], lf3)))
        mk = mm_ref[lev].astype(F32)
        for h in range(HG_HEADS):
            lanes = slice(h * HG_K, (h + 1) * HG_K)
            a_m = _dot_nt(hq[:, lanes] * e_all[:, lanes], kk[:, lanes] * e_all[:, lanes])
            amat[h] = amat[h] + a_m * mk
    cum_last = cum[T - 1:T, :]
    ecum = jnp.exp(cum)
    ktail = kk * jnp.exp(cum_last - cum)
    elast = jnp.exp(cum_last)
    for h in range(HG_HEADS):
        lanes = slice(h * HG_K, (h + 1) * HG_K)
        vh = hv[:, lanes]
        vhb = vh.astype(BF16)
        st = st_ref[h]
        o = jnp.dot(amat[h].astype(BF16), vhb, preferred_element_type=F32)
        o = o + _dot_nt(hq[:, lanes] * ecum[:, lanes], st)
        st_new = elast[:, lanes] * st + jnp.dot(vh.T.astype(BF16), ktail[:, lanes].astype(BF16),
                                                preferred_element_type=F32)
        st_ref[h] = st_new
        o_ref[0, :, lanes] = _rms(o, gn_ref[:, lanes]) * _silu(hgate[:, lanes])

    @pl.when(t == nt - 1)
    def _():
        for h in range(HG_HEADS):
            hs_ref[0, h] = st_ref[h].T


def _odd_prompt_proj(x, w, layer):
    b, l, d = x.shape
    nt = l // TILE
    pos = jnp.arange(l, dtype=jnp.int32)
    c, s1, s2 = _rope_tables(pos)
    tril, dm, mm = _hg_levels(TILE)
    consts_a = [w["g"], w["win"]]
    consts_b = [w["lb"], w["gn"], tril, dm, mm]
    tab = pl.BlockSpec((TILE, LANES), lambda i, t: (t, 0))
    W = MOBA_WIDTH
    out_shape = (jax.ShapeDtypeStruct((b, l, W), F32),
                 jax.ShapeDtypeStruct((b, l, W), BF16),
                 jax.ShapeDtypeStruct((b, W, l), BF16),
                 jax.ShapeDtypeStruct((b, nt, 1, W), F32),
                 jax.ShapeDtypeStruct((b, MOBA_HEADS, l, MOBA_HEAD_DIM), F32),
                 jax.ShapeDtypeStruct((b, MOBA_HEADS, l, MOBA_HEAD_DIM), F32),
                 jax.ShapeDtypeStruct((b, l, HG_WIDTH), F32),
                 jax.ShapeDtypeStruct((b, HG_HEADS, HG_K, HG_V), F32))
    out_specs = (pl.BlockSpec((1, TILE, W), lambda i, t: (i, t, 0)),
                 pl.BlockSpec((1, TILE, W), lambda i, t: (i, t, 0)),
                 pl.BlockSpec((1, W, TILE), lambda i, t: (i, 0, t)),
                 pl.BlockSpec((1, 1, 1, W), lambda i, t: (i, t, 0, 0)),
                 pl.BlockSpec((1, MOBA_HEADS, TILE, MOBA_HEAD_DIM), lambda i, t: (i, 0, t, 0)),
                 pl.BlockSpec((1, MOBA_HEADS, TILE, MOBA_HEAD_DIM), lambda i, t: (i, 0, t, 0)),
                 pl.BlockSpec((1, TILE, HG_WIDTH), lambda i, t: (i, t, 0)),
                 pl.BlockSpec((1, HG_HEADS, HG_K, HG_V), lambda i, t: (i, 0, 0, 0)))
    return pl.pallas_call(
        functools.partial(_odd_prompt_body, nt=nt, layer=layer),
        grid=(b, nt),
        in_specs=([pl.BlockSpec((1, TILE, d), lambda i, t: (i, t, 0))] + [_const_spec(a.shape) for a in consts_a]
                  + [tab, tab, tab] + [_const_spec(a.shape) for a in consts_b]),
        out_specs=out_specs, out_shape=out_shape,
        scratch_shapes=[pltpu.VMEM((HG_HEADS, HG_V, HG_K), F32)],
        compiler_params=_params(("arbitrary", "arbitrary")), name="odd_prompt_proj")(
            x, *consts_a, c, s1, s2, *consts_b)


def _moba_prompt_body(q_ref, k_ref, vt_ref, km_ref, o_ref, sel_ref):
    i = pl.program_id(2)
    T = TILE
    hd = MOBA_HEAD_DIM
    q = q_ref[0]
    km = km_ref[0, 0]
    lane = lax.broadcasted_iota(jnp.int32, (T, LANES), 1)
    accs = []
    for e in range(2):
        mine = (lane // hd) == e
        qe = jnp.where(mine, q, 0.0)
        gate = lax.dot_general(qe, km, (((1,), (1,)), ((), ())), precision=lax.Precision.HIGHEST,
                               preferred_element_type=F32)
        elig = lane < i
        gcur = jnp.where(elig, gate, -jnp.inf)
        selm = jnp.zeros((T, LANES), F32)
        for _ in range(MOBA_TOPK):
            mx = jnp.max(gcur, axis=1, keepdims=True)
            first = jnp.min(jnp.where(gcur == mx, lane, LANES), axis=1, keepdims=True)
            pick = (lane == first) & elig
            selm = jnp.where(pick, 1.0, selm)
            gcur = jnp.where(lane == first, -jnp.inf, gcur)
        sel_ref[e] = selm.T

        qs = (qe * (hd ** -0.5)).astype(BF16)
        ri = lax.broadcasted_iota(jnp.int32, (T, T), 0)
        ci = lax.broadcasted_iota(jnp.int32, (T, T), 1)
        own = pl.multiple_of(i * T, T)
        s = _dot_nt(k_ref[0, pl.ds(own, T), :], qs)
        s = jnp.where(ri <= ci, s, NEG)
        m0 = jnp.max(s, axis=0, keepdims=True)
        p0 = jnp.exp(s - m0)
        l0 = jnp.sum(p0, axis=0, keepdims=True)
        a0 = jnp.dot(vt_ref[0, e * hd:(e + 1) * hd, pl.ds(own, T)], p0.astype(BF16), preferred_element_type=F32)

        def body(n, carry, qs=qs, e=e):
            m, l, acc = carry
            off = pl.multiple_of(n * T, T)
            s = _dot_nt(k_ref[0, pl.ds(off, T), :], qs)
            s = jnp.where(sel_ref[e, pl.ds(n, 1), :] > 0.5, s, NEG)
            m_new = jnp.maximum(m, jnp.max(s, axis=0, keepdims=True))
            alpha = jnp.exp(m - m_new)
            p = jnp.exp(s - m_new)
            l = alpha * l + jnp.sum(p, axis=0, keepdims=True)
            acc = alpha * acc + jnp.dot(vt_ref[0, e * hd:(e + 1) * hd, pl.ds(off, T)], p.astype(BF16),
                                        preferred_element_type=F32)
            return m_new, l, acc

        m, l, acc = lax.fori_loop(0, i, body, (m0, l0, a0))
        accs.append(acc / l)
    o_ref[0] = jnp.concatenate(accs, axis=0).T


def _moba_prompt(q, kb, vt, kmean):
    b, l, w = q.shape
    nq = l // TILE
    npair = w // LANES
    nb = kmean.shape[1]
    km = kmean.reshape(b, nb, npair, LANES).transpose(0, 2, 1, 3)
    km = jnp.pad(km, ((0, 0), (0, 0), (0, LANES - nb), (0, 0)))
    return pl.pallas_call(
        _moba_prompt_body,
        grid=(b, npair, nq),
        in_specs=[pl.BlockSpec((1, TILE, LANES), lambda i, p, t: (i, t, p)),
                  pl.BlockSpec((1, l, LANES), lambda i, p, t: (i, 0, p)),
                  pl.BlockSpec((1, LANES, l), lambda i, p, t: (i, p, 0)),
                  pl.BlockSpec((1, 1, LANES, LANES), lambda i, p, t: (i, p, 0, 0))],
        out_specs=pl.BlockSpec((1, TILE, LANES), lambda i, p, t: (i, t, p)),
        out_shape=jax.ShapeDtypeStruct((b, l, w), F32),
        scratch_shapes=[pltpu.VMEM((2, LANES, TILE), F32)],
        compiler_params=_params(("arbitrary", "arbitrary", "arbitrary")), name="moba_prompt")(q, kb, vt, km)


def _even_sample_proj_body(x_ref, g_ref, wz_ref, wxbc_ref, wdt_ref, wga_ref, wgb_ref, cw_ref, cb_ref, dtb_ref,
                           xbuf_ref, cfw_ref, cfb_ref, cfg_ref, cfbeta_ref, ubuf_ref,
                           z_ref, xc_ref, dt_ref, xnew_ref, c_ref, unew_ref):
    x = x_ref[...]
    hb = _rms(x, g_ref[...]).astype(BF16)
    z_ref[...] = jnp.dot(hb, wz_ref[...], preferred_element_type=F32)
    xbc = jnp.dot(hb, wxbc_ref[...], preferred_element_type=F32)
    dtr = jnp.dot(hb, wdt_ref[...], preferred_element_type=F32)
    ga = jnp.dot(hb, wga_ref[...], preferred_element_type=F32)
    gb = jnp.dot(hb, wgb_ref[...], preferred_element_type=F32)
    k1 = SSD_CONV - 1
    acc = cw_ref[k1:k1 + 1, :] * xbc
    for k in range(k1):
        acc = acc + cw_ref[k:k + 1, :] * xbuf_ref[k]
        if k > 0:
            xnew_ref[k - 1] = xbuf_ref[k]
    xnew_ref[k1 - 1] = xbc
    xc_ref[...] = _silu(acc + cb_ref[...])
    dt_ref[...] = _softplus(dtr + dtb_ref[...])
    u = ga * _sigmoid(gb)
    k2 = CF_WIDTH - 1
    cacc = cfw_ref[k2:k2 + 1, :] * u
    for k in range(k2):
        cacc = cacc + cfw_ref[k:k + 1, :] * ubuf_ref[k]
        if k > 0:
            unew_ref[k - 1] = ubuf_ref[k]
    unew_ref[k2 - 1] = u
    c32 = cacc + cfb_ref[...]
    mu = jnp.mean(c32, axis=-1, keepdims=True)
    var = jnp.mean(jnp.square(c32 - mu), axis=-1, keepdims=True)
    c_ref[...] = _silu((c32 - mu) * lax.rsqrt(var + NORM_EPS) * cfg_ref[...] + cfbeta_ref[...])


def _expand_mats(n_outer, n_inner):
    j = np.arange(n_outer * n_inner)
    rep = (j[None, :] // n_inner == np.arange(n_outer)[:, None]).astype(np.float32)
    til = (j[None, :] % n_inner == np.arange(n_inner)[:, None]).astype(np.float32)
    return jnp.asarray(rep, BF16), jnp.asarray(til, BF16)


def _ssd_step_body(s_ref, x_ref, dt_ref, alog_ref, b_ref, c_ref, rep_ref, til_ref, so_ref, y_ref):
    dt = dt_ref[...]
    decay = jnp.exp(dt * (-jnp.exp(alog_ref[...])))
    xrep = _dot3_l(x_ref[...] * dt, rep_ref[...])
    btil = _dot3_l(b_ref[...], til_ref[...])
    ctil = _dot3_l(c_ref[...], til_ref[...])
    s_new = decay * s_ref[...] + xrep * btil
    so_ref[...] = s_new
    hi, mid, lo = _split3(s_new * ctil)
    rep = rep_ref[...]
    nt = (((1,), (1,)), ((), ()))
    y_ref[...] = (lax.dot_general(hi, rep, nt, preferred_element_type=F32)
                  + lax.dot_general(mid, rep, nt, preferred_element_type=F32)
                  + lax.dot_general(lo, rep, nt, preferred_element_type=F32))


def _even_sample_out_body(x_ref, y_ref, xs_ref, z_ref, dfull_ref, sn_ref, c_ref, woy_ref, woc_ref, o_ref):
    y = (y_ref[...] + dfull_ref[...] * xs_ref[...]) * _silu(z_ref[...])
    gw = SSD_INNER // SSD_GROUPS
    out = _dot(c_ref[...], woc_ref[...])
    for grp in range(SSD_GROUPS):
        lanes = slice(grp * gw, (grp + 1) * gw)
        out = out + _dot(_rms(y[:, lanes], sn_ref[:, lanes]), woy_ref[lanes, :])
    o_ref[...] = x_ref[...] + out


def _call(body, out_shape, *args, name):
    return pl.pallas_call(body, out_shape=out_shape, compiler_params=_params(None), name=name)(*args)


def _even_sample(x, w, s_ssd, buf_ssd, buf_cf):
    nb = x.shape[0]
    f = lambda *s: jax.ShapeDtypeStruct(s, F32)
    z, xc, dt, xnew, c, unew = _call(
        _even_sample_proj_body,
        (f(nb, SSD_INNER), f(nb, SSD_CONV_DIM), f(nb, LANES), f(SSD_CONV - 1, nb, SSD_CONV_DIM), f(nb, CF_CH),
         f(CF_WIDTH - 1, nb, CF_CH)),
        x, w["g"], w["wz"], w["wxbc"], w["wdt"], w["wga"], w["wgb"], w["cw"], w["cb"], w["dtb"],
        jnp.swapaxes(buf_ssd, 0, 1), w["cfw"], w["cfb"], w["cfg"], w["cfbeta"], jnp.swapaxes(buf_cf, 0, 1),
        name="even_sample_proj")
    xs = xc[:, :SSD_INNER]
    rows = nb * SSD_HEADS
    rep_heads = SSD_HEADS // SSD_GROUPS
    grp = lambda a: jnp.repeat(a.reshape(nb, SSD_GROUPS, SSD_STATE), rep_heads, axis=1).reshape(rows, SSD_STATE)
    bm = grp(xc[:, SSD_INNER:SSD_INNER + SSD_GROUPS * SSD_STATE])
    cm = grp(xc[:, SSD_INNER + SSD_GROUPS * SSD_STATE:])
    rep, til = _expand_mats(SSD_HEAD_DIM, SSD_STATE)
    pn = SSD_HEAD_DIM * SSD_STATE
    rb = 128
    rowspec = lambda cdim: pl.BlockSpec((rb, cdim), lambda i: (i, 0))
    s_new, y = pl.pallas_call(
        _ssd_step_body, grid=(rows // rb,),
        in_specs=[rowspec(pn), rowspec(SSD_HEAD_DIM), rowspec(1), rowspec(1), rowspec(SSD_STATE), rowspec(SSD_STATE),
                  _const_spec(rep.shape), _const_spec(til.shape)],
        out_specs=(rowspec(pn), rowspec(SSD_HEAD_DIM)),
        out_shape=(f(rows, pn), f(rows, SSD_HEAD_DIM)),
        compiler_params=_params(("parallel",)), name="ssd_step")(
            s_ssd.reshape(rows, pn), xs.reshape(rows, SSD_HEAD_DIM), dt[:, :SSD_HEADS].reshape(rows, 1),
            jnp.tile(w["alog"][0, :SSD_HEADS], nb).reshape(rows, 1), bm, cm, rep, til)
    xo = _call(_even_sample_out_body, f(nb, x.shape[1]),
               x, y.reshape(nb, SSD_INNER), xs, z, w["dfull"], w["sn"], c, w["woy"], w["woc"],
               name="even_sample_out")
    return (xo, s_new.reshape(s_ssd.shape), jnp.swapaxes(xnew, 0, 1), jnp.swapaxes(unew, 0, 1))


def _odd_sample_proj_body(x_ref, g_ref, win_ref, c_ref, s1_ref, s2_ref, lb_ref,
                          q_ref, k_ref, v_ref, hq_ref, f_ref, hv_ref, hg_ref, *, layer):
    W = MOBA_WIDTH
    hb = _rms(x_ref[...], g_ref[...]).astype(BF16)
    proj = jnp.dot(hb, win_ref[...], preferred_element_type=F32)
    c, s1, s2 = c_ref[...], s1_ref[...], s2_ref[...]
    qb = _rope(proj[:, 0:W], c, s1, s2)
    kb = _rope(proj[:, W:2 * W], c, s1, s2)
    for j in range(W // LANES):
        q_ref[:, j * LANES:(j + 1) * LANES] = qb[j]
        k_ref[:, j * LANES:(j + 1) * LANES] = kb[j]
    v_ref[...] = proj[:, 2 * W:3 * W]
    lb = _hg_lower_bound(lb_ref, layer)
    hq_ref[...] = _silu(proj[:, 3 * W:3 * W + HG_WIDTH])
    f_ref[...] = lb + (1.0 - lb) * _sigmoid(proj[:, 3 * W + HG_WIDTH:3 * W + 2 * HG_WIDTH])
    hv_ref[...] = proj[:, 3 * W + 2 * HG_WIDTH:3 * W + 3 * HG_WIDTH]
    hg_ref[...] = proj[:, 3 * W + 3 * HG_WIDTH:3 * W + 4 * HG_WIDTH]


def _hg_step_body(s_ref, q_ref, f_ref, v_ref, rep_ref, til_ref, so_ref, o_ref):
    f = f_ref[...]
    frep = _dot3_l(f, rep_ref[...])
    krep = _dot3_l(1.0 - f, rep_ref[...])
    qrep = _dot3_l(q_ref[...], rep_ref[...])
    vtil = _dot3_l(v_ref[...], til_ref[...])
    s_new = frep * s_ref[...] + krep * vtil
    so_ref[...] = s_new
    hi, mid, lo = _split3(s_new * qrep)
    til = til_ref[...]
    nt = (((1,), (1,)), ((), ()))
    o_ref[...] = (lax.dot_general(hi, til, nt, preferred_element_type=F32)
                  + lax.dot_general(mid, til, nt, preferred_element_type=F32)
                  + lax.dot_general(lo, til, nt, preferred_element_type=F32))


def _page_sum_body(k_ref, o_ref):
    lane = lax.broadcasted_iota(jnp.int32, (k_ref.shape[0], LANES), 1)

    def head_sum(h):
        s = jnp.sum(k_ref[:, h], axis=1)
        return s + pltpu.roll(s, MOBA_HEAD_DIM, 1)

    for j in range(MOBA_HEADS // 2):
        o_ref[:, j, :] = jnp.where(lane < MOBA_HEAD_DIM, head_sum(2 * j), head_sum(2 * j + 1))


def _sample_gate_body(pt_ref, ps_ref, q_ref, kn_ref, seg_ref, idx_ref, km_ref, *, n_pages):
    b = pl.program_id(0)
    ppb = MOBA_BLOCK // PAGE_SIZE
    nblk = n_pages // ppb
    km_ref[...] = jnp.zeros(km_ref.shape, F32)

    def fill(j, carry):
        acc = ps_ref[pl.ds(pt_ref[b, j * ppb], 1), :]
        for r in range(1, ppb):
            acc = acc + ps_ref[pl.ds(pt_ref[b, j * ppb + r], 1), :]
        km_ref[pl.ds(j, 1), :] = acc * (1.0 / MOBA_BLOCK)
        return carry

    lax.fori_loop(0, nblk, fill, 0)
    km_ref[nblk:nblk + 1, :] = kn_ref[0] * (1.0 / MOBA_BLOCK)
    gate = _dot3_l(km_ref[...] * q_ref[0], seg_ref[...])
    row = lax.broadcasted_iota(jnp.int32, gate.shape, 0)
    gcur = jnp.where(row < nblk, gate, -jnp.inf)
    big = gate.shape[0]
    for r in range(MOBA_TOPK):
        mx = jnp.max(gcur, axis=0, keepdims=True)
        first = jnp.min(jnp.where(gcur == mx, row, big), axis=0, keepdims=True)
        idx_ref[0, r:r + 1, :] = first
        gcur = jnp.where(row == first, -jnp.inf, gcur)
    idx_ref[0, MOBA_TOPK:, :] = jnp.zeros((8 - MOBA_TOPK, LANES), jnp.int32)


def _sample_attn_body(phys_ref, q_ref, k_ref, v_ref, kn_ref, vn_ref, o_ref, m_ref, l_ref, acc_ref, *, nsel):
    j = pl.program_id(2)
    scale = MOBA_HEAD_DIM ** -0.5
    q = q_ref[0, 0]

    @pl.when(j == 0)
    def _():
        s0 = jnp.sum(q * kn_ref[0, 0], axis=1, keepdims=True) * scale
        m_ref[...] = s0
        l_ref[...] = jnp.ones(l_ref.shape, F32)
        acc_ref[...] = jnp.broadcast_to(vn_ref[0, 0], acc_ref.shape)

    s = _dot_nt(q, k_ref[0, 0]) * scale
    m = m_ref[...]
    m_new = jnp.maximum(m, jnp.max(s, axis=1, keepdims=True))
    alpha = jnp.exp(m - m_new)
    p = jnp.exp(s - m_new)
    l_ref[...] = alpha * l_ref[...] + jnp.sum(p, axis=1, keepdims=True)
    acc_ref[...] = alpha * acc_ref[...] + _dot(p, v_ref[0, 0])
    m_ref[...] = m_new

    @pl.when(j == nsel - 1)
    def _():
        o_ref[0, 0] = acc_ref[...] / l_ref[...]


def _odd_sample_out_body(x_ref, att_ref, o_ref, hg_ref, gn_ref, woa_ref, woh_ref, xo_ref):
    out = _dot(att_ref[...], woa_ref[...])
    o = o_ref[...]
    gated = []
    for h in range(HG_HEADS):
        lanes = slice(h * HG_V, (h + 1) * HG_V)
        gated.append(_rms(o[:, lanes], gn_ref[:, lanes]) * _silu(hg_ref[:, lanes]))
    out = out + _dot(jnp.concatenate(gated, axis=1), woh_ref[...])
    xo_ref[...] = x_ref[...] + out


def _odd_sample(x, w, layer, s_hg, cache_k, cache_v, page_table, past_len):
    nb = x.shape[0]
    f = lambda *s: jax.ShapeDtypeStruct(s, F32)
    n_pool = cache_k.shape[0]
    n_pages = page_table.shape[1]
    ppb = MOBA_BLOCK // PAGE_SIZE
    nblk = n_pages // ppb
    assert n_pages % ppb == 0 and nblk >= MOBA_TOPK and nblk < LANES
    c, s1, s2 = _rope_tables(jnp.full((1,), past_len, jnp.int32))
    q, k, v, hq, fg, hv, hg = _call(
        functools.partial(_odd_sample_proj_body, layer=layer),
        tuple(f(nb, MOBA_WIDTH) for _ in range(7)),
        x, w["g"], w["win"], c, s1, s2, w["lb"], name="odd_sample_proj")

    rows = nb * HG_HEADS
    rep, til = _expand_mats(HG_K, HG_V)
    kv = HG_K * HG_V
    rb = 32
    rowspec = lambda cdim: pl.BlockSpec((rb, cdim), lambda i: (i, 0))
    s_new, o = pl.pallas_call(
        _hg_step_body, grid=(rows // rb,),
        in_specs=[rowspec(kv), rowspec(HG_K), rowspec(HG_K), rowspec(HG_V), _const_spec(rep.shape),
                  _const_spec(til.shape)],
        out_specs=(rowspec(kv), rowspec(HG_V)), out_shape=(f(rows, kv), f(rows, HG_V)),
        compiler_params=_params(("parallel",)), name="hg_step")(
            s_hg.reshape(rows, kv), hq.reshape(rows, HG_K), fg.reshape(rows, HG_K), hv.reshape(rows, HG_V), rep, til)

    pb = 16
    assert n_pool % pb == 0
    ps = pl.pallas_call(
        _page_sum_body, grid=(n_pool // pb,),
        in_specs=[pl.BlockSpec((pb, MOBA_HEADS, PAGE_SIZE // 2, LANES), lambda i: (i, 0, 0, 0))],
        out_specs=pl.BlockSpec((pb, MOBA_HEADS // 2, LANES), lambda i: (i, 0, 0)),
        out_shape=f(n_pool, MOBA_HEADS // 2, LANES),
        compiler_params=_params(("parallel",)), name="page_sum")(
            cache_k.reshape(n_pool, MOBA_HEADS, PAGE_SIZE // 2, LANES))
    seg = jnp.asarray((np.arange(MOBA_WIDTH)[:, None] // MOBA_HEAD_DIM == np.arange(LANES)[None, :])
                      .astype(np.float32), BF16)
    idx = pl.pallas_call(
        functools.partial(_sample_gate_body, n_pages=n_pages),
        grid_spec=pltpu.PrefetchScalarGridSpec(
            num_scalar_prefetch=1, grid=(nb,),
            in_specs=[pl.BlockSpec((n_pool, MOBA_WIDTH), lambda i, pt: (0, 0), pipeline_mode=pl.Buffered(1)),
                      pl.BlockSpec((1, 1, MOBA_WIDTH), lambda i, pt: (i, 0, 0)),
                      pl.BlockSpec((1, 1, MOBA_WIDTH), lambda i, pt: (i, 0, 0)),
                      pl.BlockSpec((MOBA_WIDTH, LANES), lambda i, pt: (0, 0), pipeline_mode=pl.Buffered(1))],
            out_specs=pl.BlockSpec((1, 8, LANES), lambda i, pt: (i, 0, 0)),
            scratch_shapes=[pltpu.VMEM((LANES, MOBA_WIDTH), F32)]),
        out_shape=jax.ShapeDtypeStruct((nb, 8, LANES), jnp.int32),
        compiler_params=_params(("arbitrary",)), name="sample_gate")(
            page_table, ps.reshape(n_pool, MOBA_WIDTH), q.reshape(nb, 1, MOBA_WIDTH), k.reshape(nb, 1, MOBA_WIDTH), seg)
    blocks = jnp.swapaxes(idx[:, :MOBA_TOPK, :MOBA_HEADS], 1, 2)
    lp = (blocks[..., None] * ppb + jnp.arange(ppb, dtype=jnp.int32)).reshape(nb, MOBA_HEADS, MOBA_TOPK * ppb)
    phys = jnp.take_along_axis(page_table[:, None, :], lp, axis=2)
    nsel = MOBA_TOPK * ppb
    hd = MOBA_HEAD_DIM
    q8 = jnp.broadcast_to(q.reshape(nb, MOBA_HEADS, 1, hd), (nb, MOBA_HEADS, 8, hd))
    kn = k.reshape(nb, MOBA_HEADS, 1, hd)
    vn = v.reshape(nb, MOBA_HEADS, 1, hd)
    page = lambda i, h, j, ph: (ph[i, h, j], h, 0, 0)
    att = pl.pallas_call(
        functools.partial(_sample_attn_body, nsel=nsel),
        grid_spec=pltpu.PrefetchScalarGridSpec(
            num_scalar_prefetch=1, grid=(nb, MOBA_HEADS, nsel),
            in_specs=[pl.BlockSpec((1, 1, 8, hd), lambda i, h, j, ph: (i, h, 0, 0)),
                      pl.BlockSpec((1, 1, PAGE_SIZE, hd), page),
                      pl.BlockSpec((1, 1, PAGE_SIZE, hd), page),
                      pl.BlockSpec((1, 1, 1, hd), lambda i, h, j, ph: (i, h, 0, 0)),
                      pl.BlockSpec((1, 1, 1, hd), lambda i, h, j, ph: (i, h, 0, 0))],
            out_specs=pl.BlockSpec((1, 1, 8, hd), lambda i, h, j, ph: (i, h, 0, 0)),
            scratch_shapes=[pltpu.VMEM((8, 1), F32), pltpu.VMEM((8, 1), F32), pltpu.VMEM((8, hd), F32)]),
        out_shape=f(nb, MOBA_HEADS, 8, hd),
        compiler_params=_params(("arbitrary", "arbitrary", "arbitrary")), name="sample_attn")(
            phys, q8, cache_k, cache_v, kn, vn)
    att = att[:, :, 0, :].reshape(nb, MOBA_WIDTH)

    xo = _call(_odd_sample_out_body, f(nb, x.shape[1]),
               x, att, o.reshape(nb, HG_WIDTH), hg, w["gn"], w["woa"], w["woh"], name="odd_sample_out")
    return xo, s_new.reshape(s_hg.shape), kn, vn


def _row(a):
    return a.reshape(1, -1).astype(F32)


def _pad_lanes(a, n=LANES):
    return jnp.pad(a, [(0, 0)] * (a.ndim - 1) + [(0, n - a.shape[-1])])


def kernel(x_prompt, x_sample, state_ssd, state_ssd_conv, state_cf_conv, cache_k, cache_v, page_table, state_hg,
           ffn1_norm, ffn1_w_gu, ffn1_w_down, mix_norm, ffn2_norm, ffn2_w_gu, ffn2_w_down, final_norm,
           even_w_in, ssd_conv_w, ssd_conv_b, ssd_dt_bias, ssd_a_log, ssd_d, ssd_norm,
           cf_dw_w, cf_dw_b, cf_ln_g, cf_ln_b, even_w_out,
           odd_w_in, hg_lower_bound, hg_norm, odd_w_out):
    depth = ffn1_norm.shape[0]
    bp, lp, d = x_prompt.shape
    nb = x_sample.shape[0]
    assert x_sample.shape[1] == 1 and lp % TILE == 0 and lp // TILE <= LANES
    d_ff = ffn1_w_down.shape[1]
    past_len = page_table.shape[1] * PAGE_SIZE

    def ffn_w(norm, w_gu, w_down, l):
        return (_row(norm[l]), w_gu[l, :, :d_ff].astype(BF16), w_gu[l, :, d_ff:].astype(BF16), w_down[l].astype(BF16))

    def even_w(e, l):
        wi = even_w_in[e]
        o1 = SSD_INNER
        o2 = o1 + SSD_CONV_DIM
        o3 = o2 + SSD_HEADS
        sel = (np.arange(LANES)[:, None] == np.arange(SSD_INNER)[None, :] // SSD_HEAD_DIM).astype(np.float32)
        return dict(
            g=_row(mix_norm[l]), wz=wi[:, :o1].astype(BF16), wxbc=wi[:, o1:o2].astype(BF16),
            wdt=_pad_lanes(wi[:, o2:o3]).astype(BF16), wga=wi[:, o3:o3 + CF_CH].astype(BF16),
            wgb=wi[:, o3 + CF_CH:].astype(BF16), cw=ssd_conv_w[e], cb=_row(ssd_conv_b[e]),
            dtb=_pad_lanes(_row(ssd_dt_bias[e])), alog=_pad_lanes(_row(ssd_a_log[e])),
            dfull=_row(jnp.repeat(ssd_d[e], SSD_HEAD_DIM)), sn=_row(ssd_norm[e]), sel=jnp.asarray(sel, BF16),
            cfw=cf_dw_w[e], cfb=_row(cf_dw_b[e]), cfg=_row(cf_ln_g[e]), cfbeta=_row(cf_ln_b[e]),
            woy=even_w_out[e, :SSD_INNER].astype(BF16), woc=even_w_out[e, SSD_INNER:].astype(BF16))

    def odd_w(o, l):
        return dict(g=_row(mix_norm[l]), win=odd_w_in[o].astype(BF16), lb=hg_lower_bound.astype(F32),
                    gn=_row(hg_norm[o]), woa=odd_w_out[o, :MOBA_WIDTH].astype(BF16),
                    woh=odd_w_out[o, MOBA_WIDTH:].astype(BF16))

    xp = x_prompt.reshape(bp * lp, d)
    xs = x_sample.reshape(nb, d)
    ssd_p, ssd_s, sc_p, sc_s, cf_p, cf_s = [], [], [], [], [], []
    k_p, k_s, v_p, v_s, hg_p, hg_s = [], [], [], [], [], []
    pre_p = pre_s = None
    for l in range(depth):
        w1 = ffn_w(ffn1_norm, ffn1_w_gu, ffn1_w_down, l)
        w2 = ffn_w(ffn2_norm, ffn2_w_gu, ffn2_w_down, l)
        fin = _row(final_norm) if l == depth - 1 else None
        xp = _ffn(xp, w1, tm=TILE)
        xs = _ffn(xs, w1, tm=nb)
        if l % 2 == 0:
            e = l // 2
            w = even_w(e, l)
            xp3, st, xt, ut = _even_prompt(xp.reshape(bp, lp, d), w)
            xp = xp3.reshape(bp * lp, d)
            ssd_p.append(st)
            sc_p.append(xt)
            cf_p.append(ut)
            xs, st, xt, ut = _even_sample(xs, w, state_ssd[e], state_ssd_conv[e], state_cf_conv[e])
            ssd_s.append(st)
            sc_s.append(xt)
            cf_s.append(ut)
            xp = _ffn(xp, w2, tm=TILE, final_g=fin)
        else:
            o = l // 2
            w = odd_w(o, l)
            q, kb, vt, kmean, ko, vo, ohg, hst = _odd_prompt_proj(xp.reshape(bp, lp, d), w, l)
            att = _moba_prompt(q, kb, vt, kmean[:, :, 0, :])
            k_p.append(ko)
            v_p.append(vo)
            hg_p.append(hst)
            xp = _ffn(xp, w2, tm=TILE, final_g=fin,
                      pre=(att.reshape(bp * lp, MOBA_WIDTH), ohg.reshape(bp * lp, HG_WIDTH), w["woa"], w["woh"]))
            xs, hst, kn, vn = _odd_sample(xs, w, l, state_hg[o], cache_k[o], cache_v[o], page_table, past_len)
            k_s.append(kn)
            v_s.append(vn)
            hg_s.append(hst)
        xs = _ffn(xs, w2, tm=nb, final_g=fin)
    st = jnp.stack
    return (xp.reshape(bp, lp, d), xs.reshape(nb, 1, d), st(ssd_p), st(ssd_s), st(sc_p), st(sc_s), st(cf_p), st(cf_s),
            st(k_p), st(k_s), st(v_p), st(v_s), st(hg_p), st(hg_s))
```

```python
import functools
import math

import numpy as np
import jax
import jax.numpy as jnp
from jax import lax
from jax.experimental import pallas as pl
from jax.experimental.pallas import tpu as pltpu

F32 = jnp.float32
BF16 = jnp.bfloat16

SSD_HEADS = 16
SSD_HEAD_DIM = 64
SSD_INNER = SSD_HEADS * SSD_HEAD_DIM
SSD_GROUPS = 2
SSD_STATE = 64
SSD_CONV = 4
SSD_CONV_DIM = SSD_INNER + 2 * SSD_GROUPS * SSD_STATE
CF_CH = 512
CF_WIDTH = 31
MOBA_HEADS = 8
MOBA_HEAD_DIM = 64
MOBA_WIDTH = MOBA_HEADS * MOBA_HEAD_DIM
MOBA_BLOCK = 256
MOBA_TOPK = 3
ROPE_DIM = MOBA_HEAD_DIM // 4
ROPE_THETA = 500000.0
HG_HEADS = 4
HG_K = 128
HG_V = 128
HG_WIDTH = HG_HEADS * HG_K
PAGE_SIZE = 128
NORM_EPS = 1e-6
NEG = -1e30

LANES = 128
SSD_CHUNK = 128
TILE = 256
VMEM_LIMIT = 56 * 1024 * 1024


def _sigmoid(x):
    return 1.0 / (1.0 + jnp.exp(-x))


def _silu(x):
    return x * _sigmoid(x)


def _softplus(x):
    return jnp.maximum(x, 0.0) + jnp.log1p(jnp.exp(-jnp.abs(x)))


def _rms(x, g):
    return x * lax.rsqrt(jnp.mean(x * x, axis=-1, keepdims=True) + NORM_EPS) * g


def _dot(a, b):
    return jnp.dot(a.astype(BF16), b.astype(BF16), preferred_element_type=F32)


def _dot_nt(a, b):
    return lax.dot_general(a.astype(BF16), b.astype(BF16), (((1,), (1,)), ((), ())),
                           preferred_element_type=F32)


def _split3(a):
    hi = a.astype(BF16)
    r = a - hi.astype(F32)
    mid = r.astype(BF16)
    lo = (r - mid.astype(F32)).astype(BF16)
    return hi, mid, lo


def _dot3_l(a, m):
    hi, mid, lo = _split3(a)
    return (jnp.dot(hi, m, preferred_element_type=F32) + jnp.dot(mid, m, preferred_element_type=F32)
            + jnp.dot(lo, m, preferred_element_type=F32))


def _dot3_r(m, parts):
    hi, mid, lo = parts
    return (jnp.dot(m, hi, preferred_element_type=F32) + jnp.dot(m, mid, preferred_element_type=F32)
            + jnp.dot(m, lo, preferred_element_type=F32))


def _const_spec(shape):
    n = len(shape)
    return pl.BlockSpec(shape, lambda *_: (0,) * n, pipeline_mode=pl.Buffered(1))


def _params(sem):
    return pltpu.CompilerParams(dimension_semantics=sem, vmem_limit_bytes=VMEM_LIMIT)


def _ffn_body(*refs, pre, final):
    it = iter(refs)
    x_ref = next(it)
    if pre:
        a_ref, b_ref, wa_ref, wb_ref = next(it), next(it), next(it), next(it)
    g_ref, wg_ref, wu_ref, wd_ref = next(it), next(it), next(it), next(it)
    if final:
        fg_ref = next(it)
    o_ref = next(it)
    x = x_ref[...]
    if pre:
        x = x + _dot(a_ref[...], wa_ref[...]) + _dot(b_ref[...], wb_ref[...])
    hb = _rms(x, g_ref[...]).astype(BF16)
    g = jnp.dot(hb, wg_ref[...], preferred_element_type=F32)
    u = jnp.dot(hb, wu_ref[...], preferred_element_type=F32)
    act = (_silu(g) * u).astype(BF16)
    y = x + 0.5 * jnp.dot(act, wd_ref[...], preferred_element_type=F32)
    if final:
        y = _rms(y, fg_ref[...])
    o_ref[...] = y


def _ffn(x, w, *, tm, pre=None, final_g=None):
    m, d = x.shape
    g, wg, wu, wd = w
    row = lambda c: pl.BlockSpec((tm, c), lambda i: (i, 0))
    args, specs = [x], [row(d)]
    if pre is not None:
        a, b, wa, wb = pre
        args += [a, b, wa, wb]
        specs += [row(a.shape[1]), row(b.shape[1]), _const_spec(wa.shape), _const_spec(wb.shape)]
    args += [g, wg, wu, wd]
    specs += [_const_spec(g.shape), _const_spec(wg.shape), _const_spec(wu.shape), _const_spec(wd.shape)]
    if final_g is not None:
        args.append(final_g)
        specs.append(_const_spec(final_g.shape))
    return pl.pallas_call(
        functools.partial(_ffn_body, pre=pre is not None, final=final_g is not None),
        grid=(m // tm,), in_specs=specs, out_specs=row(d),
        out_shape=jax.ShapeDtypeStruct((m, d), F32),
        compiler_params=_params(("parallel",)), name="ffn")(*args)


def _even_prompt_body(x_ref, g_ref, wz_ref, wxbc_ref, wdt_ref, wga_ref, wgb_ref,
                      cw_ref, cb_ref, dtb_ref, alog_ref, dfull_ref, sn_ref, sel_ref,
                      cfw_ref, cfb_ref, cfg_ref, cfbeta_ref, woy_ref, woc_ref,
                      xo_ref, st_ref, xtail_ref, utail_ref,
                      xbuf, ubuf, s_ref, ybuf, *, nt):
    t = pl.program_id(1)
    T = TILE
    C = SSD_CHUNK

    @pl.when(t == 0)
    def _():
        xbuf[0:8, :] = jnp.zeros((8, SSD_CONV_DIM), F32)
        ubuf[0:32, :] = jnp.zeros((32, CF_CH), F32)
        s_ref[...] = jnp.zeros(s_ref.shape, F32)

    x = x_ref[0]
    hb = _rms(x, g_ref[...]).astype(BF16)
    z = jnp.dot(hb, wz_ref[...], preferred_element_type=F32)
    xbc = jnp.dot(hb, wxbc_ref[...], preferred_element_type=F32)
    dtr = jnp.dot(hb, wdt_ref[...], preferred_element_type=F32)
    ga = jnp.dot(hb, wga_ref[...], preferred_element_type=F32)
    gb = jnp.dot(hb, wgb_ref[...], preferred_element_type=F32)

    xbuf[8:8 + T, :] = xbc
    acc = cw_ref[0:1, :] * xbuf[5:5 + T, :]
    for k in range(1, SSD_CONV):
        acc = acc + cw_ref[k:k + 1, :] * xbuf[5 + k:5 + k + T, :]
    tail8 = xbuf[T:T + 8, :]
    xtail_ref[0] = tail8
    xbuf[0:8, :] = tail8
    xc = _silu(acc + cb_ref[...])
    xs = xc[:, 0:SSD_INNER]
    bm = xc[:, SSD_INNER:SSD_INNER + LANES]
    cm = xc[:, SSD_INNER + LANES:SSD_INNER + 2 * LANES]
    dt = _softplus(dtr + dtb_ref[...])
    dta = dt * (-jnp.exp(alog_ref[...]))
    sel = sel_ref[...]

    ri = lax.broadcasted_iota(jnp.int32, (C, C), 0)
    ci = lax.broadcasted_iota(jnp.int32, (C, C), 1)
    tri = ri >= ci
    trib = tri.astype(BF16)
    low = ci < SSD_HEAD_DIM

    for c in range(T // C):
        r0 = c * C
        xs_c, bm_c, cm_c = xs[r0:r0 + C], bm[r0:r0 + C], cm[r0:r0 + C]
        dt_c = dt[r0:r0 + C]
        cum = _dot3_r(trib, _split3(dta[r0:r0 + C]))
        cum_t = cum.T
        dt_t = dt_c.T
        cum_last = cum[C - 1:C, :]
        ecum_full = _dot3_l(jnp.exp(cum), sel)
        tail_full = _dot3_l(jnp.exp(cum_last - cum) * dt_c, sel)
        dlast_full = _dot3_l(jnp.broadcast_to(jnp.exp(cum_last), (8, LANES)), sel)[0:1]
        bm_t = bm_c.T
        cmb = cm_c.astype(BF16)
        for grp in range(SSD_GROUPS):
            bm_tg = jnp.where((ri // SSD_STATE) == grp, bm_t, 0.0).astype(BF16)
            gmat = jnp.dot(cmb, bm_tg, preferred_element_type=F32)
            pairs = SSD_HEADS // SSD_GROUPS // 2
            for pp in range(pairs):
                p = grp * pairs + pp
                lanes = slice(p * LANES, (p + 1) * LANES)
                xp = xs_c[:, lanes]
                xpb = xp.astype(BF16)
                ys = []
                for e in range(2):
                    h = 2 * p + e
                    seg = cum[:, h:h + 1] - cum_t[h:h + 1, :]
                    wm = gmat * jnp.exp(jnp.where(tri, seg, NEG)) * dt_t[h:h + 1, :]
                    ys.append(jnp.dot(wm.astype(BF16), xpb, preferred_element_type=F32))
                yp = jnp.where(low, ys[0], ys[1])
                sp = s_ref[p]
                yp = yp + jnp.dot(cmb, sp.astype(BF16), preferred_element_type=F32) * ecum_full[:, lanes]
                s_ref[p] = dlast_full[:, lanes] * sp + jnp.dot(
                    bm_tg, (xp * tail_full[:, lanes]).astype(BF16), preferred_element_type=F32)
                ybuf[r0:r0 + C, lanes] = yp

    @pl.when(t == nt - 1)
    def _():
        for p in range(SSD_HEADS // 2):
            st_ref[0, p] = s_ref[p].T

    y = (ybuf[...] + dfull_ref[...] * xs) * _silu(z)
    gw = SSD_INNER // SSD_GROUPS
    yn = []
    for grp in range(SSD_GROUPS):
        yg = y[:, grp * gw:(grp + 1) * gw]
        yn.append(_rms(yg, sn_ref[:, grp * gw:(grp + 1) * gw]).astype(BF16))

    u = ga * _sigmoid(gb)
    ubuf[32:32 + T, :] = u
    cacc = cfw_ref[0:1, :] * ubuf[2:2 + T, :]
    for k in range(1, CF_WIDTH):
        cacc = cacc + cfw_ref[k:k + 1, :] * ubuf[2 + k:2 + k + T, :]
    tail32 = ubuf[T:T + 32, :]
    utail_ref[0] = tail32
    ubuf[0:32, :] = tail32
    c32 = cacc + cfb_ref[...]
    mu = jnp.mean(c32, axis=-1, keepdims=True)
    var = jnp.mean(jnp.square(c32 - mu), axis=-1, keepdims=True)
    c32 = _silu((c32 - mu) * lax.rsqrt(var + NORM_EPS) * cfg_ref[...] + cfbeta_ref[...])

    out = jnp.dot(c32.astype(BF16), woc_ref[...], preferred_element_type=F32)
    for grp in range(SSD_GROUPS):
        out = out + jnp.dot(yn[grp], woy_ref[grp * gw:(grp + 1) * gw, :], preferred_element_type=F32)
    xo_ref[0] = x + out


def _even_prompt(x, w):
    b, l, d = x.shape
    nt = l // TILE
    consts = [w[k] for k in ("g", "wz", "wxbc", "wdt", "wga", "wgb", "cw", "cb", "dtb", "alog", "dfull", "sn",
                             "sel", "cfw", "cfb", "cfg", "cfbeta", "woy", "woc")]
    out_shape = (jax.ShapeDtypeStruct((b, l, d), F32),
                 jax.ShapeDtypeStruct((b, SSD_HEADS // 2, LANES, LANES), F32),
                 jax.ShapeDtypeStruct((b, 8, SSD_CONV_DIM), F32),
                 jax.ShapeDtypeStruct((b, 32, CF_CH), F32))
    out_specs = (pl.BlockSpec((1, TILE, d), lambda i, t: (i, t, 0)),
                 pl.BlockSpec((1, SSD_HEADS // 2, LANES, LANES), lambda i, t: (i, 0, 0, 0)),
                 pl.BlockSpec((1, 8, SSD_CONV_DIM), lambda i, t: (i, 0, 0)),
                 pl.BlockSpec((1, 32, CF_CH), lambda i, t: (i, 0, 0)))
    xo, st, xtail, utail = pl.pallas_call(
        functools.partial(_even_prompt_body, nt=nt),
        grid=(b, nt),
        in_specs=[pl.BlockSpec((1, TILE, d), lambda i, t: (i, t, 0))] + [_const_spec(c.shape) for c in consts],
        out_specs=out_specs, out_shape=out_shape,
        scratch_shapes=[pltpu.VMEM((TILE + 8, SSD_CONV_DIM), F32), pltpu.VMEM((TILE + 32, CF_CH), F32),
                        pltpu.VMEM((SSD_HEADS // 2, LANES, LANES), F32), pltpu.VMEM((TILE, SSD_INNER), F32)],
        compiler_params=_params(("arbitrary", "arbitrary")), name="even_prompt")(x, *consts)
    half = SSD_HEADS // 2 // SSD_GROUPS
    parts = []
    for p in range(SSD_HEADS // 2):
        grp = p // half
        for e in range(2):
            parts.append(st[:, p, e * 64:(e + 1) * 64, grp * 64:(grp + 1) * 64])
    state = jnp.stack(parts, axis=1)
    return xo, state, xtail[:, 8 - (SSD_CONV - 1):], utail[:, 32 - (CF_WIDTH - 1):]


def _hg_levels(T):
    t = np.arange(T)
    le = (t[None, :] <= t[:, None]).astype(np.float32)
    ds, ms = [], []
    m = 1
    while m < T:
        rb = (t // (2 * m)) * 2 * m + m - 1
        ds.append(le - (t[None, :] <= rb[:, None]).astype(np.float32))
        same = (t[:, None] // (2 * m)) == (t[None, :] // (2 * m))
        ms.append((same & ((t[:, None] % (2 * m)) >= m) & ((t[None, :] % (2 * m)) < m)).astype(np.float32))
        m *= 2
    return (jnp.asarray(le, BF16), jnp.asarray(np.stack(ds), BF16), jnp.asarray(np.stack(ms), BF16))


def _rope_tables(pos):
    half = ROPE_DIM // 2
    inv = ROPE_THETA ** (-jnp.arange(half, dtype=F32) / half)
    ang = pos.astype(F32)[:, None] * inv[None, :]
    cos, sin = jnp.cos(ang), jnp.sin(ang)
    n = pos.shape[0]
    one = jnp.ones((n, MOBA_HEAD_DIM - ROPE_DIM), F32)
    zero = jnp.zeros((n, MOBA_HEAD_DIM - ROPE_DIM), F32)
    zh = jnp.zeros((n, half), F32)
    c = jnp.concatenate([cos, cos, one], axis=1)
    s1 = jnp.concatenate([-sin, zh, zero], axis=1)
    s2 = jnp.concatenate([zh, sin, zero], axis=1)
    return tuple(jnp.concatenate([a, a], axis=1) for a in (c, s1, s2))


def _rope(x, c, s1, s2):
    outs = []
    for j in range(x.shape[1] // LANES):
        blk = x[:, j * LANES:(j + 1) * LANES]
        outs.append(blk * c + pltpu.roll(blk, LANES - ROPE_DIM // 2, 1) * s1 + pltpu.roll(blk, ROPE_DIM // 2, 1) * s2)
    return outs


def _hg_lower_bound(lb_ref, layer):
    a = lb_ref[...]
    mx = jnp.max(a, axis=0, keepdims=True)
    e = jnp.exp(a - mx)
    return jnp.sum(e[1:layer + 1], axis=0, keepdims=True) / jnp.sum(e, axis=0, keepdims=True)


def _odd_prompt_body(x_ref, g_ref, win_ref, c_ref, s1_ref, s2_ref, lb_ref, gn_ref, tril_ref, dm_ref, mm_ref,
                     q_ref, kb_ref, vt_ref, km_ref, ko_ref, vo_ref, o_ref, hs_ref,
                     st_ref, *, nt, layer):
    t = pl.program_id(1)
    T = TILE
    W = MOBA_WIDTH

    @pl.when(t == 0)
    def _():
        st_ref[...] = jnp.zeros(st_ref.shape, F32)

    x = x_ref[0]
    hb = _rms(x, g_ref[...]).astype(BF16)
    proj = jnp.dot(hb, win_ref[...], preferred_element_type=F32)

    c, s1, s2 = c_ref[...], s1_ref[...], s2_ref[...]
    qb = _rope(proj[:, 0:W], c, s1, s2)
    kb = _rope(proj[:, W:2 * W], c, s1, s2)
    v = proj[:, 2 * W:3 * W]
    for j in range(W // LANES):
        lanes = slice(j * LANES, (j + 1) * LANES)
        q_ref[0, :, lanes] = qb[j]
        kb_ref[0, :, lanes] = kb[j].astype(BF16)
        km_ref[0, 0, :, lanes] = jnp.mean(kb[j], axis=0, keepdims=True)
        kt = kb[j].T
        vt = v[:, lanes].T
        vt_ref[0, lanes, :] = vt.astype(BF16)
        for e in range(2):
            ko_ref[0, 2 * j + e] = kt[e * 64:(e + 1) * 64, :]
            vo_ref[0, 2 * j + e] = vt[e * 64:(e + 1) * 64, :]

    lb = _hg_lower_bound(lb_ref, layer)
    hq = _silu(proj[:, 3 * W:3 * W + HG_WIDTH])
    f = lb + (1.0 - lb) * _sigmoid(proj[:, 3 * W + HG_WIDTH:3 * W + 2 * HG_WIDTH])
    hv = proj[:, 3 * W + 2 * HG_WIDTH:3 * W + 3 * HG_WIDTH]
    hgate = proj[:, 3 * W + 3 * HG_WIDTH:3 * W + 4 * HG_WIDTH]
    kk = 1.0 - f
    lf3 = _split3(jnp.log(f))
    cum = _dot3_r(tril_ref[...], lf3)
    nlev = dm_ref.shape[0]
    ri = lax.broadcasted_iota(jnp.int32, (T, T), 0)
    ci = lax.broadcasted_iota(jnp.int32, (T, T), 1)
    eye = ri == ci
    amat = []
    for h in range(HG_HEADS):
        lanes = slice(h * HG_K, (h + 1) * HG_K)
        diag = jnp.sum(hq[:, lanes] * kk[:, lanes], axis=1, keepdims=True)
        amat.append(jnp.where(eye, diag, 0.0))
    for lev in range(nlev):
        e_all = jnp.exp(-jnp.abs(_dot3_r(dm_ref[lev], lf3)))
        mk = mm_ref[lev].astype(F32)
        for h in range(HG_HEADS):
            lanes = slice(h * HG_K, (h + 1) * HG_K)
            a_m = _dot_nt(hq[:, lanes] * e_all[:, lanes], kk[:, lanes] * e_all[:, lanes])
            amat[h] = amat[h] + a_m * mk
    cum_last = cum[T - 1:T, :]
    ecum = jnp.exp(cum)
    ktail = kk * jnp.exp(cum_last - cum)
    elast = jnp.exp(cum_last)
    for h in range(HG_HEADS):
        lanes = slice(h * HG_K, (h + 1) * HG_K)
        vh = hv[:, lanes]
        vhb = vh.astype(BF16)
        st = st_ref[h]
        o = jnp.dot(amat[h].astype(BF16), vhb, preferred_element_type=F32)
        o = o + _dot_nt(hq[:, lanes] * ecum[:, lanes], st)
        st_new = elast[:, lanes] * st + jnp.dot(vh.T.astype(BF16), ktail[:, lanes].astype(BF16),
                                                preferred_element_type=F32)
        st_ref[h] = st_new
        o_ref[0, :, lanes] = _rms(o, gn_ref[:, lanes]) * _silu(hgate[:, lanes])

    @pl.when(t == nt - 1)
    def _():
        for h in range(HG_HEADS):
            hs_ref[0, h] = st_ref[h].T


def _odd_prompt_proj(x, w, layer):
    b, l, d = x.shape
    nt = l // TILE
    pos = jnp.arange(l, dtype=jnp.int32)
    c, s1, s2 = _rope_tables(pos)
    tril, dm, mm = _hg_levels(TILE)
    consts_a = [w["g"], w["win"]]
    consts_b = [w["lb"], w["gn"], tril, dm, mm]
    tab = pl.BlockSpec((TILE, LANES), lambda i, t: (t, 0))
    W = MOBA_WIDTH
    out_shape = (jax.ShapeDtypeStruct((b, l, W), F32),
                 jax.ShapeDtypeStruct((b, l, W), BF16),
                 jax.ShapeDtypeStruct((b, W, l), BF16),
                 jax.ShapeDtypeStruct((b, nt, 1, W), F32),
                 jax.ShapeDtypeStruct((b, MOBA_HEADS, MOBA_HEAD_DIM, l), F32),
                 jax.ShapeDtypeStruct((b, MOBA_HEADS, MOBA_HEAD_DIM, l), F32),
                 jax.ShapeDtypeStruct((b, l, HG_WIDTH), F32),
                 jax.ShapeDtypeStruct((b, HG_HEADS, HG_K, HG_V), F32))
    out_specs = (pl.BlockSpec((1, TILE, W), lambda i, t: (i, t, 0)),
                 pl.BlockSpec((1, TILE, W), lambda i, t: (i, t, 0)),
                 pl.BlockSpec((1, W, TILE), lambda i, t: (i, 0, t)),
                 pl.BlockSpec((1, 1, 1, W), lambda i, t: (i, t, 0, 0)),
                 pl.BlockSpec((1, MOBA_HEADS, MOBA_HEAD_DIM, TILE), lambda i, t: (i, 0, 0, t)),
                 pl.BlockSpec((1, MOBA_HEADS, MOBA_HEAD_DIM, TILE), lambda i, t: (i, 0, 0, t)),
                 pl.BlockSpec((1, TILE, HG_WIDTH), lambda i, t: (i, t, 0)),
                 pl.BlockSpec((1, HG_HEADS, HG_K, HG_V), lambda i, t: (i, 0, 0, 0)))
    return pl.pallas_call(
        functools.partial(_odd_prompt_body, nt=nt, layer=layer),
        grid=(b, nt),
        in_specs=([pl.BlockSpec((1, TILE, d), lambda i, t: (i, t, 0))] + [_const_spec(a.shape) for a in consts_a]
                  + [tab, tab, tab] + [_const_spec(a.shape) for a in consts_b]),
        out_specs=out_specs, out_shape=out_shape,
        scratch_shapes=[pltpu.VMEM((HG_HEADS, HG_V, HG_K), F32)],
        compiler_params=_params(("arbitrary", "arbitrary")), name="odd_prompt_proj")(
            x, *consts_a, c, s1, s2, *consts_b)


def _moba_prompt_body(q_ref, k_ref, vt_ref, km_ref, o_ref, sel_ref):
    i = pl.program_id(2)
    T = TILE
    hd = MOBA_HEAD_DIM
    q = q_ref[0]
    km = km_ref[0, 0]
    lane = lax.broadcasted_iota(jnp.int32, (T, LANES), 1)
    blk = lax.broadcasted_iota(jnp.int32, (LANES, T), 0)
    elig = blk < i
    qs = []
    for e in range(2):
        qe = jnp.where((lane // hd) == e, q, 0.0)
        gate = lax.dot_general(km, qe, (((1,), (1,)), ((), ())), precision=lax.Precision.HIGHEST,
                               preferred_element_type=F32)
        gcur = jnp.where(elig, gate, -jnp.inf)
        selm = jnp.zeros((LANES, T), F32)
        for _ in range(MOBA_TOPK):
            mx = jnp.max(gcur, axis=0, keepdims=True)
            first = jnp.min(jnp.where(gcur == mx, blk, LANES), axis=0, keepdims=True)
            hit = blk == first
            selm = jnp.where(hit & elig, 1.0, selm)
            gcur = jnp.where(hit, -jnp.inf, gcur)
        sel_ref[e] = selm
        qs.append((qe * (hd ** -0.5)).astype(BF16))

    ri = lax.broadcasted_iota(jnp.int32, (T, T), 0)
    ci = lax.broadcasted_iota(jnp.int32, (T, T), 1)
    own = pl.multiple_of(i * T, T)
    kown = k_ref[0, pl.ds(own, T), :]
    state = []
    for e in range(2):
        s = jnp.where(ri <= ci, _dot_nt(kown, qs[e]), NEG)
        m0 = jnp.max(s, axis=0, keepdims=True)
        p0 = jnp.exp(s - m0)
        state += [m0, jnp.sum(p0, axis=0, keepdims=True),
                  jnp.dot(vt_ref[0, e * hd:(e + 1) * hd, pl.ds(own, T)], p0.astype(BF16),
                          preferred_element_type=F32)]

    def body(n2, carry):
        off = pl.multiple_of(n2 * 2 * T, 2 * T)
        kk = k_ref[0, pl.ds(off, 2 * T), :]
        out = []
        for e in range(2):
            m, l, acc = carry[3 * e:3 * e + 3]
            s = _dot_nt(kk, qs[e])
            sa = jnp.where(sel_ref[e, pl.ds(2 * n2, 1), :] > 0.5, s[:T], NEG)
            sb = jnp.where(sel_ref[e, pl.ds(2 * n2 + 1, 1), :] > 0.5, s[T:], NEG)
            m_new = jnp.maximum(m, jnp.maximum(jnp.max(sa, axis=0, keepdims=True),
                                               jnp.max(sb, axis=0, keepdims=True)))
            alpha = jnp.exp(m - m_new)
            pa = jnp.exp(sa - m_new)
            pb = jnp.exp(sb - m_new)
            l = alpha * l + jnp.sum(pa, axis=0, keepdims=True) + jnp.sum(pb, axis=0, keepdims=True)
            p = jnp.concatenate([pa.astype(BF16), pb.astype(BF16)], axis=0)
            acc = alpha * acc + jnp.dot(vt_ref[0, e * hd:(e + 1) * hd, pl.ds(off, 2 * T)], p,
                                        preferred_element_type=F32)
            out += [m_new, l, acc]
        return tuple(out)

    fin = lax.fori_loop(0, (i + 1) // 2, body, tuple(state))
    o_ref[0] = jnp.concatenate([fin[2] / fin[1], fin[5] / fin[4]], axis=0).T


def _moba_prompt(q, kb, vt, kmean):
    b, l, w = q.shape
    nq = l // TILE
    npair = w // LANES
    nb = kmean.shape[1]
    km = kmean.reshape(b, nb, npair, LANES).transpose(0, 2, 1, 3)
    km = jnp.pad(km, ((0, 0), (0, 0), (0, LANES - nb), (0, 0)))
    return pl.pallas_call(
        _moba_prompt_body,
        grid=(b, npair, nq),
        in_specs=[pl.BlockSpec((1, TILE, LANES), lambda i, p, t: (i, t, p)),
                  pl.BlockSpec((1, l, LANES), lambda i, p, t: (i, 0, p)),
                  pl.BlockSpec((1, LANES, l), lambda i, p, t: (i, p, 0)),
                  pl.BlockSpec((1, 1, LANES, LANES), lambda i, p, t: (i, p, 0, 0))],
        out_specs=pl.BlockSpec((1, TILE, LANES), lambda i, p, t: (i, t, p)),
        out_shape=jax.ShapeDtypeStruct((b, l, w), F32),
        scratch_shapes=[pltpu.VMEM((2, LANES, TILE), F32)],
        compiler_params=_params(("arbitrary", "arbitrary", "arbitrary")), name="moba_prompt")(q, kb, vt, km)


def _even_sample_proj_body(x_ref, g_ref, wz_ref, wxbc_ref, wdt_ref, wga_ref, wgb_ref, cw_ref, cb_ref, dtb_ref,
                           xbuf_ref, cfw_ref, cfb_ref, cfg_ref, cfbeta_ref, ubuf_ref,
                           z_ref, xc_ref, dt_ref, xnew_ref, c_ref, unew_ref):
    x = x_ref[...]
    hb = _rms(x, g_ref[...]).astype(BF16)
    z_ref[...] = jnp.dot(hb, wz_ref[...], preferred_element_type=F32)
    xbc = jnp.dot(hb, wxbc_ref[...], preferred_element_type=F32)
    dtr = jnp.dot(hb, wdt_ref[...], preferred_element_type=F32)
    ga = jnp.dot(hb, wga_ref[...], preferred_element_type=F32)
    gb = jnp.dot(hb, wgb_ref[...], preferred_element_type=F32)
    k1 = SSD_CONV - 1
    acc = cw_ref[k1:k1 + 1, :] * xbc
    for k in range(k1):
        acc = acc + cw_ref[k:k + 1, :] * xbuf_ref[k]
        if k > 0:
            xnew_ref[k - 1] = xbuf_ref[k]
    xnew_ref[k1 - 1] = xbc
    xc_ref[...] = _silu(acc + cb_ref[...])
    dt_ref[...] = _softplus(dtr + dtb_ref[...])
    u = ga * _sigmoid(gb)
    k2 = CF_WIDTH - 1
    cacc = cfw_ref[k2:k2 + 1, :] * u
    for k in range(k2):
        cacc = cacc + cfw_ref[k:k + 1, :] * ubuf_ref[k]
        if k > 0:
            unew_ref[k - 1] = ubuf_ref[k]
    unew_ref[k2 - 1] = u
    c32 = cacc + cfb_ref[...]
    mu = jnp.mean(c32, axis=-1, keepdims=True)
    var = jnp.mean(jnp.square(c32 - mu), axis=-1, keepdims=True)
    c_ref[...] = _silu((c32 - mu) * lax.rsqrt(var + NORM_EPS) * cfg_ref[...] + cfbeta_ref[...])


def _expand_mats(n_outer, n_inner):
    j = np.arange(n_outer * n_inner)
    rep = (j[None, :] // n_inner == np.arange(n_outer)[:, None]).astype(np.float32)
    til = (j[None, :] % n_inner == np.arange(n_inner)[:, None]).astype(np.float32)
    return jnp.asarray(rep, BF16), jnp.asarray(til, BF16)


def _ssd_step_body(s_ref, x_ref, dt_ref, alog_ref, b_ref, c_ref, rep_ref, til_ref, so_ref, y_ref):
    dt = dt_ref[...]
    decay = jnp.exp(dt * (-jnp.exp(alog_ref[...])))
    xrep = _dot3_l(x_ref[...] * dt, rep_ref[...])
    btil = _dot3_l(b_ref[...], til_ref[...])
    ctil = _dot3_l(c_ref[...], til_ref[...])
    s_new = decay * s_ref[...] + xrep * btil
    so_ref[...] = s_new
    hi, mid, lo = _split3(s_new * ctil)
    rep = rep_ref[...]
    nt = (((1,), (1,)), ((), ()))
    y_ref[...] = (lax.dot_general(hi, rep, nt, preferred_element_type=F32)
                  + lax.dot_general(mid, rep, nt, preferred_element_type=F32)
                  + lax.dot_general(lo, rep, nt, preferred_element_type=F32))


def _even_sample_out_body(x_ref, y_ref, xs_ref, z_ref, dfull_ref, sn_ref, c_ref, woy_ref, woc_ref, o_ref):
    y = (y_ref[...] + dfull_ref[...] * xs_ref[...]) * _silu(z_ref[...])
    gw = SSD_INNER // SSD_GROUPS
    out = _dot(c_ref[...], woc_ref[...])
    for grp in range(SSD_GROUPS):
        lanes = slice(grp * gw, (grp + 1) * gw)
        out = out + _dot(_rms(y[:, lanes], sn_ref[:, lanes]), woy_ref[lanes, :])
    o_ref[...] = x_ref[...] + out


def _call(body, out_shape, *args, name):
    return pl.pallas_call(body, out_shape=out_shape, compiler_params=_params(None), name=name)(*args)


def _even_sample(x, w, s_ssd, buf_ssd, buf_cf):
    nb = x.shape[0]
    f = lambda *s: jax.ShapeDtypeStruct(s, F32)
    z, xc, dt, xnew, c, unew = _call(
        _even_sample_proj_body,
        (f(nb, SSD_INNER), f(nb, SSD_CONV_DIM), f(nb, LANES), f(SSD_CONV - 1, nb, SSD_CONV_DIM), f(nb, CF_CH),
         f(CF_WIDTH - 1, nb, CF_CH)),
        x, w["g"], w["wz"], w["wxbc"], w["wdt"], w["wga"], w["wgb"], w["cw"], w["cb"], w["dtb"],
        jnp.swapaxes(buf_ssd, 0, 1), w["cfw"], w["cfb"], w["cfg"], w["cfbeta"], jnp.swapaxes(buf_cf, 0, 1),
        name="even_sample_proj")
    xs = xc[:, :SSD_INNER]
    rows = nb * SSD_HEADS
    rep_heads = SSD_HEADS // SSD_GROUPS
    grp = lambda a: jnp.repeat(a.reshape(nb, SSD_GROUPS, SSD_STATE), rep_heads, axis=1).reshape(rows, SSD_STATE)
    bm = grp(xc[:, SSD_INNER:SSD_INNER + SSD_GROUPS * SSD_STATE])
    cm = grp(xc[:, SSD_INNER + SSD_GROUPS * SSD_STATE:])
    rep, til = _expand_mats(SSD_HEAD_DIM, SSD_STATE)
    pn = SSD_HEAD_DIM * SSD_STATE
    rb = 128
    rowspec = lambda cdim: pl.BlockSpec((rb, cdim), lambda i: (i, 0))
    s_new, y = pl.pallas_call(
        _ssd_step_body, grid=(rows // rb,),
        in_specs=[rowspec(pn), rowspec(SSD_HEAD_DIM), rowspec(1), rowspec(1), rowspec(SSD_STATE), rowspec(SSD_STATE),
                  _const_spec(rep.shape), _const_spec(til.shape)],
        out_specs=(rowspec(pn), rowspec(SSD_HEAD_DIM)),
        out_shape=(f(rows, pn), f(rows, SSD_HEAD_DIM)),
        compiler_params=_params(("parallel",)), name="ssd_step")(
            s_ssd.reshape(rows, pn), xs.reshape(rows, SSD_HEAD_DIM), dt[:, :SSD_HEADS].reshape(rows, 1),
            jnp.tile(w["alog"][0, :SSD_HEADS], nb).reshape(rows, 1), bm, cm, rep, til)
    xo = _call(_even_sample_out_body, f(nb, x.shape[1]),
               x, y.reshape(nb, SSD_INNER), xs, z, w["dfull"], w["sn"], c, w["woy"], w["woc"],
               name="even_sample_out")
    return (xo, s_new.reshape(s_ssd.shape), jnp.swapaxes(xnew, 0, 1), jnp.swapaxes(unew, 0, 1))


def _odd_sample_proj_body(x_ref, g_ref, win_ref, c_ref, s1_ref, s2_ref, lb_ref,
                          q_ref, k_ref, v_ref, hq_ref, f_ref, hv_ref, hg_ref, *, layer):
    W = MOBA_WIDTH
    hb = _rms(x_ref[...], g_ref[...]).astype(BF16)
    proj = jnp.dot(hb, win_ref[...], preferred_element_type=F32)
    c, s1, s2 = c_ref[...], s1_ref[...], s2_ref[...]
    qb = _rope(proj[:, 0:W], c, s1, s2)
    kb = _rope(proj[:, W:2 * W], c, s1, s2)
    for j in range(W // LANES):
        q_ref[:, j * LANES:(j + 1) * LANES] = qb[j]
        k_ref[:, j * LANES:(j + 1) * LANES] = kb[j]
    v_ref[...] = proj[:, 2 * W:3 * W]
    lb = _hg_lower_bound(lb_ref, layer)
    hq_ref[...] = _silu(proj[:, 3 * W:3 * W + HG_WIDTH])
    f_ref[...] = lb + (1.0 - lb) * _sigmoid(proj[:, 3 * W + HG_WIDTH:3 * W + 2 * HG_WIDTH])
    hv_ref[...] = proj[:, 3 * W + 2 * HG_WIDTH:3 * W + 3 * HG_WIDTH]
    hg_ref[...] = proj[:, 3 * W + 3 * HG_WIDTH:3 * W + 4 * HG_WIDTH]


def _hg_step_body(s_ref, q_ref, f_ref, v_ref, rep_ref, til_ref, so_ref, o_ref):
    f = f_ref[...]
    frep = _dot3_l(f, rep_ref[...])
    krep = _dot3_l(1.0 - f, rep_ref[...])
    qrep = _dot3_l(q_ref[...], rep_ref[...])
    vtil = _dot3_l(v_ref[...], til_ref[...])
    s_new = frep * s_ref[...] + krep * vtil
    so_ref[...] = s_new
    hi, mid, lo = _split3(s_new * qrep)
    til = til_ref[...]
    nt = (((1,), (1,)), ((), ()))
    o_ref[...] = (lax.dot_general(hi, til, nt, preferred_element_type=F32)
                  + lax.dot_general(mid, til, nt, preferred_element_type=F32)
                  + lax.dot_general(lo, til, nt, preferred_element_type=F32))


GATE_CHUNK = 16


def _gate_copies(pt_ref, ck_ref, kbuf, sem, layer, seq, chunk, slot):
    return [pltpu.make_async_copy(ck_ref.at[layer, pt_ref[seq, chunk * GATE_CHUNK + j]], kbuf.at[slot, j],
                                  sem.at[slot]) for j in range(GATE_CHUNK)]


def _sample_gate_body(pt_ref, ck_ref, q_ref, idx_ref, kbuf, sem, g_ref, *, layer, n_chunks, n_seq):
    b = pl.program_id(0)
    c = pl.program_id(1)
    step = b * n_chunks + c
    slot = step % 2
    ppb = MOBA_BLOCK // PAGE_SIZE
    nblk = n_chunks * GATE_CHUNK // ppb

    @pl.when(step == 0)
    def _():
        for cp in _gate_copies(pt_ref, ck_ref, kbuf, sem, layer, b, c, slot):
            cp.start()

    nxt = step + 1

    @pl.when(nxt < n_seq * n_chunks)
    def _():
        for cp in _gate_copies(pt_ref, ck_ref, kbuf, sem, layer, nxt // n_chunks, nxt % n_chunks, 1 - slot):
            cp.start()

    for cp in _gate_copies(pt_ref, ck_ref, kbuf, sem, layer, b, c, slot):
        cp.wait()

    @pl.when(c == 0)
    def _():
        g_ref[...] = jnp.zeros(g_ref.shape, F32)

    lane = lax.broadcasted_iota(jnp.int32, (MOBA_HEADS, LANES), 1)
    sub = lax.broadcasted_iota(jnp.int32, (MOBA_HEADS, LANES), 0)
    g = g_ref[...]
    for j in range(GATE_CHUNK):
        tile = jnp.zeros((MOBA_HEADS, LANES), F32)
        for h in range(MOBA_HEADS):
            t = jnp.sum(kbuf[slot, j, h] * q_ref[0, h], axis=0, keepdims=True)
            tile = jnp.where(sub == h, t, tile)
        col = jnp.sum(tile, axis=1, keepdims=True)
        g = jnp.where(lane == (c * GATE_CHUNK + j) // ppb, g + col, g)
    g_ref[...] = g

    @pl.when(c == n_chunks - 1)
    def _():
        gcur = jnp.where(lane < nblk, g * (1.0 / MOBA_BLOCK), -jnp.inf)
        out = jnp.zeros((MOBA_HEADS, LANES), jnp.int32)
        for r in range(MOBA_TOPK):
            mx = jnp.max(gcur, axis=1, keepdims=True)
            first = jnp.min(jnp.where(gcur == mx, lane, LANES), axis=1, keepdims=True)
            out = jnp.where(lane == r, first, out)
            gcur = jnp.where(lane == first, -jnp.inf, gcur)
        idx_ref[0] = out


def _attn_copies(idx_ref, pt_ref, ck_ref, cv_ref, kbuf, vbuf, sem, layer, seq, slot):
    ppb = MOBA_BLOCK // PAGE_SIZE
    cps = []
    for h in range(MOBA_HEADS):
        for r in range(MOBA_TOPK):
            blk = idx_ref[seq, h * MOBA_TOPK + r]
            for pg in range(ppb):
                phys = pt_ref[seq, blk * ppb + pg]
                j = r * ppb + pg
                cps.append(pltpu.make_async_copy(ck_ref.at[layer, phys, h], kbuf.at[slot, h, j], sem.at[0, slot]))
                cps.append(pltpu.make_async_copy(cv_ref.at[layer, phys, h], vbuf.at[slot, h, j], sem.at[1, slot]))
    return cps


def _sample_attn_body(idx_ref, pt_ref, ck_ref, cv_ref, q_ref, kn_ref, vn_ref, o_ref, kbuf, vbuf, sem, *,
                      layer, n_seq):
    b = pl.program_id(0)
    slot = b % 2
    scale = MOBA_HEAD_DIM ** -0.5
    nsel = MOBA_TOPK * (MOBA_BLOCK // PAGE_SIZE)

    @pl.when(b == 0)
    def _():
        for cp in _attn_copies(idx_ref, pt_ref, ck_ref, cv_ref, kbuf, vbuf, sem, layer, b, slot):
            cp.start()

    @pl.when(b + 1 < n_seq)
    def _():
        for cp in _attn_copies(idx_ref, pt_ref, ck_ref, cv_ref, kbuf, vbuf, sem, layer, b + 1, 1 - slot):
            cp.start()

    for cp in _attn_copies(idx_ref, pt_ref, ck_ref, cv_ref, kbuf, vbuf, sem, layer, b, slot):
        cp.wait()

    for h in range(MOBA_HEADS):
        qc = q_ref[0, h]
        s_self = jnp.sum(qc * kn_ref[0, h], axis=0, keepdims=True) * scale
        ss = [jnp.sum(kbuf[slot, h, j] * qc, axis=0, keepdims=True) * scale for j in range(nsel)]
        m = s_self
        for s in ss:
            m = jnp.maximum(m, jnp.max(s, axis=1, keepdims=True))
        p_self = jnp.exp(s_self - m)
        l = p_self
        acc = jnp.zeros((MOBA_HEAD_DIM, LANES), F32)
        for j in range(nsel):
            p = jnp.exp(ss[j] - m)
            l = l + jnp.sum(p, axis=1, keepdims=True)
            acc = acc + vbuf[slot, h, j] * p
        o_ref[0, h] = (vn_ref[0, h] * p_self + jnp.sum(acc, axis=1, keepdims=True)) / l


def _odd_sample_out_body(x_ref, att_ref, o_ref, hg_ref, gn_ref, woa_ref, woh_ref, xo_ref):
    out = _dot(att_ref[...], woa_ref[...])
    o = o_ref[...]
    gated = []
    for h in range(HG_HEADS):
        lanes = slice(h * HG_V, (h + 1) * HG_V)
        gated.append(_rms(o[:, lanes], gn_ref[:, lanes]) * _silu(hg_ref[:, lanes]))
    out = out + _dot(jnp.concatenate(gated, axis=1), woh_ref[...])
    xo_ref[...] = x_ref[...] + out


def _odd_sample(x, w, layer, s_hg, cache_kt, cache_vt, cache_layer, page_table, past_len):
    nb = x.shape[0]
    f = lambda *s: jax.ShapeDtypeStruct(s, F32)
    n_pages = page_table.shape[1]
    ppb = MOBA_BLOCK // PAGE_SIZE
    nblk = n_pages // ppb
    assert n_pages % ppb == 0 and nblk >= MOBA_TOPK and nblk < LANES
    c, s1, s2 = _rope_tables(jnp.full((1,), past_len, jnp.int32))
    q, k, v, hq, fg, hv, hg = _call(
        functools.partial(_odd_sample_proj_body, layer=layer),
        tuple(f(nb, MOBA_WIDTH) for _ in range(7)),
        x, w["g"], w["win"], c, s1, s2, w["lb"], name="odd_sample_proj")

    rows = nb * HG_HEADS
    rep, til = _expand_mats(HG_K, HG_V)
    kv = HG_K * HG_V
    rb = 32
    rowspec = lambda cdim: pl.BlockSpec((rb, cdim), lambda i: (i, 0))
    s_new, o = pl.pallas_call(
        _hg_step_body, grid=(rows // rb,),
        in_specs=[rowspec(kv), rowspec(HG_K), rowspec(HG_K), rowspec(HG_V), _const_spec(rep.shape),
                  _const_spec(til.shape)],
        out_specs=(rowspec(kv), rowspec(HG_V)), out_shape=(f(rows, kv), f(rows, HG_V)),
        compiler_params=_params(("parallel",)), name="hg_step")(
            s_hg.reshape(rows, kv), hq.reshape(rows, HG_K), fg.reshape(rows, HG_K), hv.reshape(rows, HG_V), rep, til)

    hd = MOBA_HEAD_DIM
    assert n_pages % GATE_CHUNK == 0 and GATE_CHUNK % ppb == 0
    n_chunks = n_pages // GATE_CHUNK
    col = lambda a: jnp.broadcast_to(a.reshape(nb, MOBA_HEADS, hd, 1), (nb, MOBA_HEADS, hd, LANES))
    qcol, kcol, vcol = col(q), col(k), col(v)
    colspec = lambda nidx: pl.BlockSpec((1, MOBA_HEADS, hd, LANES), lambda i, *_: (i, 0, 0, 0))
    idx = pl.pallas_call(
        functools.partial(_sample_gate_body, layer=cache_layer, n_chunks=n_chunks, n_seq=nb),
        grid_spec=pltpu.PrefetchScalarGridSpec(
            num_scalar_prefetch=1, grid=(nb, n_chunks),
            in_specs=[pl.BlockSpec(memory_space=pl.ANY), colspec(1)],
            out_specs=pl.BlockSpec((1, MOBA_HEADS, LANES), lambda i, c, pt: (i, 0, 0)),
            scratch_shapes=[pltpu.VMEM((2, GATE_CHUNK, MOBA_HEADS, hd, PAGE_SIZE), F32),
                            pltpu.SemaphoreType.DMA((2,)), pltpu.VMEM((MOBA_HEADS, LANES), F32)]),
        out_shape=jax.ShapeDtypeStruct((nb, MOBA_HEADS, LANES), jnp.int32),
        compiler_params=_params(("arbitrary", "arbitrary")), name="sample_gate")(page_table, cache_kt, qcol)
    nsel = MOBA_TOPK * ppb
    att = pl.pallas_call(
        functools.partial(_sample_attn_body, layer=cache_layer, n_seq=nb),
        grid_spec=pltpu.PrefetchScalarGridSpec(
            num_scalar_prefetch=2, grid=(nb,),
            in_specs=[pl.BlockSpec(memory_space=pl.ANY), pl.BlockSpec(memory_space=pl.ANY),
                      colspec(2), colspec(2), colspec(2)],
            out_specs=colspec(2),
            scratch_shapes=[pltpu.VMEM((2, MOBA_HEADS, nsel, hd, PAGE_SIZE), F32),
                            pltpu.VMEM((2, MOBA_HEADS, nsel, hd, PAGE_SIZE), F32),
                            pltpu.SemaphoreType.DMA((2, 2))]),
        out_shape=f(nb, MOBA_HEADS, hd, LANES),
        compiler_params=_params(("arbitrary",)), name="sample_attn")(
            idx[:, :, :MOBA_TOPK].reshape(nb, MOBA_HEADS * MOBA_TOPK), page_table, cache_kt, cache_vt,
            qcol, kcol, vcol)
    att = att[:, :, :, 0].reshape(nb, MOBA_WIDTH)
    kn = k.reshape(nb, MOBA_HEADS, 1, hd)
    vn = v.reshape(nb, MOBA_HEADS, 1, hd)

    xo = _call(_odd_sample_out_body, f(nb, x.shape[1]),
               x, att, o.reshape(nb, HG_WIDTH), hg, w["gn"], w["woa"], w["woh"], name="odd_sample_out")
    return xo, s_new.reshape(s_hg.shape), kn, vn


def _row(a):
    return a.reshape(1, -1).astype(F32)


def _pad_lanes(a, n=LANES):
    return jnp.pad(a, [(0, 0)] * (a.ndim - 1) + [(0, n - a.shape[-1])])


def kernel(x_prompt, x_sample, state_ssd, state_ssd_conv, state_cf_conv, cache_k, cache_v, page_table, state_hg,
           ffn1_norm, ffn1_w_gu, ffn1_w_down, mix_norm, ffn2_norm, ffn2_w_gu, ffn2_w_down, final_norm,
           even_w_in, ssd_conv_w, ssd_conv_b, ssd_dt_bias, ssd_a_log, ssd_d, ssd_norm,
           cf_dw_w, cf_dw_b, cf_ln_g, cf_ln_b, even_w_out,
           odd_w_in, hg_lower_bound, hg_norm, odd_w_out):
    depth = ffn1_norm.shape[0]
    bp, lp, d = x_prompt.shape
    nb = x_sample.shape[0]
    assert x_sample.shape[1] == 1 and lp % TILE == 0 and lp // TILE <= LANES
    d_ff = ffn1_w_down.shape[1]
    past_len = page_table.shape[1] * PAGE_SIZE

    def ffn_w(norm, w_gu, w_down, l):
        return (_row(norm[l]), w_gu[l, :, :d_ff].astype(BF16), w_gu[l, :, d_ff:].astype(BF16), w_down[l].astype(BF16))

    def even_w(e, l):
        wi = even_w_in[e]
        o1 = SSD_INNER
        o2 = o1 + SSD_CONV_DIM
        o3 = o2 + SSD_HEADS
        sel = (np.arange(LANES)[:, None] == np.arange(SSD_INNER)[None, :] // SSD_HEAD_DIM).astype(np.float32)
        return dict(
            g=_row(mix_norm[l]), wz=wi[:, :o1].astype(BF16), wxbc=wi[:, o1:o2].astype(BF16),
            wdt=_pad_lanes(wi[:, o2:o3]).astype(BF16), wga=wi[:, o3:o3 + CF_CH].astype(BF16),
            wgb=wi[:, o3 + CF_CH:].astype(BF16), cw=ssd_conv_w[e], cb=_row(ssd_conv_b[e]),
            dtb=_pad_lanes(_row(ssd_dt_bias[e])), alog=_pad_lanes(_row(ssd_a_log[e])),
            dfull=_row(jnp.repeat(ssd_d[e], SSD_HEAD_DIM)), sn=_row(ssd_norm[e]), sel=jnp.asarray(sel, BF16),
            cfw=cf_dw_w[e], cfb=_row(cf_dw_b[e]), cfg=_row(cf_ln_g[e]), cfbeta=_row(cf_ln_b[e]),
            woy=even_w_out[e, :SSD_INNER].astype(BF16), woc=even_w_out[e, SSD_INNER:].astype(BF16))

    def odd_w(o, l):
        return dict(g=_row(mix_norm[l]), win=odd_w_in[o].astype(BF16), lb=hg_lower_bound.astype(F32),
                    gn=_row(hg_norm[o]), woa=odd_w_out[o, :MOBA_WIDTH].astype(BF16),
                    woh=odd_w_out[o, MOBA_WIDTH:].astype(BF16))

    cache_kt = jnp.swapaxes(cache_k, -1, -2)
    cache_vt = jnp.swapaxes(cache_v, -1, -2)
    xp = x_prompt.reshape(bp * lp, d)
    xs = x_sample.reshape(nb, d)
    ssd_p, ssd_s, sc_p, sc_s, cf_p, cf_s = [], [], [], [], [], []
    k_p, k_s, v_p, v_s, hg_p, hg_s = [], [], [], [], [], []
    pre_p = pre_s = None
    for l in range(depth):
        w1 = ffn_w(ffn1_norm, ffn1_w_gu, ffn1_w_down, l)
        w2 = ffn_w(ffn2_norm, ffn2_w_gu, ffn2_w_down, l)
        fin = _row(final_norm) if l == depth - 1 else None
        xp = _ffn(xp, w1, tm=TILE)
        xs = _ffn(xs, w1, tm=nb)
        if l % 2 == 0:
            e = l // 2
            w = even_w(e, l)
            xp3, st, xt, ut = _even_prompt(xp.reshape(bp, lp, d), w)
            xp = xp3.reshape(bp * lp, d)
            ssd_p.append(st)
            sc_p.append(xt)
            cf_p.append(ut)
            xs, st, xt, ut = _even_sample(xs, w, state_ssd[e], state_ssd_conv[e], state_cf_conv[e])
            ssd_s.append(st)
            sc_s.append(xt)
            cf_s.append(ut)
            xp = _ffn(xp, w2, tm=TILE, final_g=fin)
        else:
            o = l // 2
            w = odd_w(o, l)
            q, kb, vt, kmean, ko, vo, ohg, hst = _odd_prompt_proj(xp.reshape(bp, lp, d), w, l)
            att = _moba_prompt(q, kb, vt, kmean[:, :, 0, :])
            k_p.append(jnp.swapaxes(ko, -1, -2))
            v_p.append(jnp.swapaxes(vo, -1, -2))
            hg_p.append(hst)
            xp = _ffn(xp, w2, tm=TILE, final_g=fin,
                      pre=(att.reshape(bp * lp, MOBA_WIDTH), ohg.reshape(bp * lp, HG_WIDTH), w["woa"], w["woh"]))
            xs, hst, kn, vn = _odd_sample(xs, w, l, state_hg[o], cache_kt, cache_vt, o, page_table, past_len)
            k_s.append(kn)
            v_s.append(vn)
            hg_s.append(hst)
        xs = _ffn(xs, w2, tm=nb, final_g=fin)
    st = jnp.stack
    return (xp.reshape(bp, lp, d), xs.reshape(nb, 1, d), st(ssd_p), st(ssd_s), st(sc_p), st(sc_s), st(cf_p), st(cf_s),
            st(k_p), st(k_s), st(v_p), st(v_s), st(hg_p), st(hg_s))
```

```python
import functools
import math

import numpy as np
import jax
import jax.numpy as jnp
from jax import lax
from jax.experimental import pallas as pl
from jax.experimental.pallas import tpu as pltpu

F32 = jnp.float32
BF16 = jnp.bfloat16

SSD_HEADS = 16
SSD_HEAD_DIM = 64
SSD_INNER = SSD_HEADS * SSD_HEAD_DIM
SSD_GROUPS = 2
SSD_STATE = 64
SSD_CONV = 4
SSD_CONV_DIM = SSD_INNER + 2 * SSD_GROUPS * SSD_STATE
CF_CH = 512
CF_WIDTH = 31
MOBA_HEADS = 8
MOBA_HEAD_DIM = 64
MOBA_WIDTH = MOBA_HEADS * MOBA_HEAD_DIM
MOBA_BLOCK = 256
MOBA_TOPK = 3
ROPE_DIM = MOBA_HEAD_DIM // 4
ROPE_THETA = 500000.0
HG_HEADS = 4
HG_K = 128
HG_V = 128
HG_WIDTH = HG_HEADS * HG_K
PAGE_SIZE = 128
NORM_EPS = 1e-6
NEG = -1e30

LANES = 128
SSD_CHUNK = 128
TILE = 256
VT_ROWS = MOBA_HEAD_DIM + 16
VMEM_LIMIT = 56 * 1024 * 1024


def _sigmoid(x):
    return 1.0 / (1.0 + jnp.exp(-x))


def _silu(x):
    return x * _sigmoid(x)


def _softplus(x):
    return jnp.maximum(x, 0.0) + jnp.log1p(jnp.exp(-jnp.abs(x)))


def _rms(x, g):
    return x * lax.rsqrt(jnp.mean(x * x, axis=-1, keepdims=True) + NORM_EPS) * g


def _dot(a, b):
    return jnp.dot(a.astype(BF16), b.astype(BF16), preferred_element_type=F32)


def _dot_nt(a, b):
    return lax.dot_general(a.astype(BF16), b.astype(BF16), (((1,), (1,)), ((), ())),
                           preferred_element_type=F32)


def _split3(a):
    hi = a.astype(BF16)
    r = a - hi.astype(F32)
    mid = r.astype(BF16)
    lo = (r - mid.astype(F32)).astype(BF16)
    return hi, mid, lo


def _dot3_l(a, m):
    hi, mid, lo = _split3(a)
    return (jnp.dot(hi, m, preferred_element_type=F32) + jnp.dot(mid, m, preferred_element_type=F32)
            + jnp.dot(lo, m, preferred_element_type=F32))


def _dot3_r(m, parts):
    hi, mid, lo = parts
    return (jnp.dot(m, hi, preferred_element_type=F32) + jnp.dot(m, mid, preferred_element_type=F32)
            + jnp.dot(m, lo, preferred_element_type=F32))


def _const_spec(shape):
    n = len(shape)
    return pl.BlockSpec(shape, lambda *_: (0,) * n, pipeline_mode=pl.Buffered(1))


def _params(sem):
    return pltpu.CompilerParams(dimension_semantics=sem, vmem_limit_bytes=VMEM_LIMIT)


def _ffn_body(*refs, pre, final):
    it = iter(refs)
    x_ref = next(it)
    if pre:
        a_ref, b_ref, wa_ref, wb_ref = next(it), next(it), next(it), next(it)
    g_ref, wg_ref, wu_ref, wd_ref = next(it), next(it), next(it), next(it)
    if final:
        fg_ref = next(it)
    o_ref = next(it)
    x = x_ref[...]
    if pre:
        x = x + _dot(a_ref[...], wa_ref[...]) + _dot(b_ref[...], wb_ref[...])
    hb = _rms(x, g_ref[...]).astype(BF16)
    g = jnp.dot(hb, wg_ref[...], preferred_element_type=F32)
    u = jnp.dot(hb, wu_ref[...], preferred_element_type=F32)
    act = (_silu(g) * u).astype(BF16)
    y = x + 0.5 * jnp.dot(act, wd_ref[...], preferred_element_type=F32)
    if final:
        y = _rms(y, fg_ref[...])
    o_ref[...] = y


def _ffn(x, w, *, tm, pre=None, final_g=None):
    m, d = x.shape
    g, wg, wu, wd = w
    row = lambda c: pl.BlockSpec((tm, c), lambda i: (i, 0))
    args, specs = [x], [row(d)]
    if pre is not None:
        a, b, wa, wb = pre
        args += [a, b, wa, wb]
        specs += [row(a.shape[1]), row(b.shape[1]), _const_spec(wa.shape), _const_spec(wb.shape)]
    args += [g, wg, wu, wd]
    specs += [_const_spec(g.shape), _const_spec(wg.shape), _const_spec(wu.shape), _const_spec(wd.shape)]
    if final_g is not None:
        args.append(final_g)
        specs.append(_const_spec(final_g.shape))
    return pl.pallas_call(
        functools.partial(_ffn_body, pre=pre is not None, final=final_g is not None),
        grid=(m // tm,), in_specs=specs, out_specs=row(d),
        out_shape=jax.ShapeDtypeStruct((m, d), F32),
        compiler_params=_params(("parallel",)), name="ffn")(*args)


def _even_prompt_body(x_ref, g_ref, wz_ref, wxbc_ref, wdt_ref, wga_ref, wgb_ref,
                      cw_ref, cb_ref, dtb_ref, alog_ref, dfull_ref, sn_ref, sel_ref,
                      cfw_ref, cfb_ref, cfg_ref, cfbeta_ref, woy_ref, woc_ref,
                      xo_ref, st_ref, xtail_ref, utail_ref,
                      xbuf, ubuf, s_ref, ybuf, *, nt):
    t = pl.program_id(1)
    T = TILE
    C = SSD_CHUNK

    @pl.when(t == 0)
    def _():
        xbuf[0:8, :] = jnp.zeros((8, SSD_CONV_DIM), F32)
        ubuf[0:32, :] = jnp.zeros((32, CF_CH), F32)
        s_ref[...] = jnp.zeros(s_ref.shape, F32)

    x = x_ref[0]
    hb = _rms(x, g_ref[...]).astype(BF16)
    z = jnp.dot(hb, wz_ref[...], preferred_element_type=F32)
    xbc = jnp.dot(hb, wxbc_ref[...], preferred_element_type=F32)
    dtr = jnp.dot(hb, wdt_ref[...], preferred_element_type=F32)
    ga = jnp.dot(hb, wga_ref[...], preferred_element_type=F32)
    gb = jnp.dot(hb, wgb_ref[...], preferred_element_type=F32)

    xbuf[8:8 + T, :] = xbc
    acc = cw_ref[0:1, :] * xbuf[5:5 + T, :]
    for k in range(1, SSD_CONV):
        acc = acc + cw_ref[k:k + 1, :] * xbuf[5 + k:5 + k + T, :]
    tail8 = xbuf[T:T + 8, :]
    xtail_ref[0] = tail8
    xbuf[0:8, :] = tail8
    xc = _silu(acc + cb_ref[...])
    xs = xc[:, 0:SSD_INNER]
    bm = xc[:, SSD_INNER:SSD_INNER + LANES]
    cm = xc[:, SSD_INNER + LANES:SSD_INNER + 2 * LANES]
    dt = _softplus(dtr + dtb_ref[...])
    dta = dt * (-jnp.exp(alog_ref[...]))
    sel = sel_ref[...]

    ri = lax.broadcasted_iota(jnp.int32, (C, C), 0)
    ci = lax.broadcasted_iota(jnp.int32, (C, C), 1)
    tri = ri >= ci
    trib = tri.astype(BF16)
    low = ci < SSD_HEAD_DIM

    for c in range(T // C):
        r0 = c * C
        xs_c, bm_c, cm_c = xs[r0:r0 + C], bm[r0:r0 + C], cm[r0:r0 + C]
        dt_c = dt[r0:r0 + C]
        cum = _dot3_r(trib, _split3(dta[r0:r0 + C]))
        cum_t = cum.T
        dt_t = dt_c.T
        cum_last = cum[C - 1:C, :]
        ecum_full = _dot3_l(jnp.exp(cum), sel)
        tail_full = _dot3_l(jnp.exp(cum_last - cum) * dt_c, sel)
        dlast_full = _dot3_l(jnp.broadcast_to(jnp.exp(cum_last), (8, LANES)), sel)[0:1]
        bm_t = bm_c.T
        cmb = cm_c.astype(BF16)
        for grp in range(SSD_GROUPS):
            bm_tg = jnp.where((ri // SSD_STATE) == grp, bm_t, 0.0).astype(BF16)
            gmat = jnp.dot(cmb, bm_tg, preferred_element_type=F32)
            pairs = SSD_HEADS // SSD_GROUPS // 2
            for pp in range(pairs):
                p = grp * pairs + pp
                lanes = slice(p * LANES, (p + 1) * LANES)
                xp = xs_c[:, lanes]
                xpb = xp.astype(BF16)
                ys = []
                for e in range(2):
                    h = 2 * p + e
                    seg = cum[:, h:h + 1] - cum_t[h:h + 1, :]
                    wm = gmat * jnp.exp(jnp.where(tri, seg, NEG)) * dt_t[h:h + 1, :]
                    ys.append(jnp.dot(wm.astype(BF16), xpb, preferred_element_type=F32))
                yp = jnp.where(low, ys[0], ys[1])
                sp = s_ref[p]
                yp = yp + jnp.dot(cmb, sp.astype(BF16), preferred_element_type=F32) * ecum_full[:, lanes]
                s_ref[p] = dlast_full[:, lanes] * sp + jnp.dot(
                    bm_tg, (xp * tail_full[:, lanes]).astype(BF16), preferred_element_type=F32)
                ybuf[r0:r0 + C, lanes] = yp

    @pl.when(t == nt - 1)
    def _():
        for p in range(SSD_HEADS // 2):
            st_ref[0, p] = s_ref[p].T

    y = (ybuf[...] + dfull_ref[...] * xs) * _silu(z)
    gw = SSD_INNER // SSD_GROUPS
    yn = []
    for grp in range(SSD_GROUPS):
        yg = y[:, grp * gw:(grp + 1) * gw]
        yn.append(_rms(yg, sn_ref[:, grp * gw:(grp + 1) * gw]).astype(BF16))

    u = ga * _sigmoid(gb)
    ubuf[32:32 + T, :] = u
    base = 32 - (CF_WIDTH - 1)
    cacc = None
    for r in range(8):
        offs = [o for o in range(base, base + CF_WIDTH) if o % 8 == r]
        if not offs:
            continue
        ur = ubuf[r:max(offs) + T, :]
        part = None
        for o in offs:
            term = cfw_ref[o - base:o - base + 1, :] * ur[o - r:o - r + T]
            part = term if part is None else part + term
        cacc = part if cacc is None else cacc + part
    tail32 = ubuf[T:T + 32, :]
    utail_ref[0] = tail32
    ubuf[0:32, :] = tail32
    c32 = cacc + cfb_ref[...]
    mu = jnp.mean(c32, axis=-1, keepdims=True)
    var = jnp.mean(jnp.square(c32 - mu), axis=-1, keepdims=True)
    c32 = _silu((c32 - mu) * lax.rsqrt(var + NORM_EPS) * cfg_ref[...] + cfbeta_ref[...])

    out = jnp.dot(c32.astype(BF16), woc_ref[...], preferred_element_type=F32)
    for grp in range(SSD_GROUPS):
        out = out + jnp.dot(yn[grp], woy_ref[grp * gw:(grp + 1) * gw, :], preferred_element_type=F32)
    xo_ref[0] = x + out


def _even_prompt(x, w):
    b, l, d = x.shape
    nt = l // TILE
    consts = [w[k] for k in ("g", "wz", "wxbc", "wdt", "wga", "wgb", "cw", "cb", "dtb", "alog", "dfull", "sn",
                             "sel", "cfw", "cfb", "cfg", "cfbeta", "woy", "woc")]
    out_shape = (jax.ShapeDtypeStruct((b, l, d), F32),
                 jax.ShapeDtypeStruct((b, SSD_HEADS // 2, LANES, LANES), F32),
                 jax.ShapeDtypeStruct((b, 8, SSD_CONV_DIM), F32),
                 jax.ShapeDtypeStruct((b, 32, CF_CH), F32))
    out_specs = (pl.BlockSpec((1, TILE, d), lambda i, t: (i, t, 0)),
                 pl.BlockSpec((1, SSD_HEADS // 2, LANES, LANES), lambda i, t: (i, 0, 0, 0)),
                 pl.BlockSpec((1, 8, SSD_CONV_DIM), lambda i, t: (i, 0, 0)),
                 pl.BlockSpec((1, 32, CF_CH), lambda i, t: (i, 0, 0)))
    xo, st, xtail, utail = pl.pallas_call(
        functools.partial(_even_prompt_body, nt=nt),
        grid=(b, nt),
        in_specs=[pl.BlockSpec((1, TILE, d), lambda i, t: (i, t, 0))] + [_const_spec(c.shape) for c in consts],
        out_specs=out_specs, out_shape=out_shape,
        scratch_shapes=[pltpu.VMEM((TILE + 8, SSD_CONV_DIM), F32), pltpu.VMEM((TILE + 32, CF_CH), F32),
                        pltpu.VMEM((SSD_HEADS // 2, LANES, LANES), F32), pltpu.VMEM((TILE, SSD_INNER), F32)],
        compiler_params=_params(("arbitrary", "arbitrary")), name="even_prompt")(x, *consts)
    half = SSD_HEADS // 2 // SSD_GROUPS
    st = st.reshape(b, SSD_HEADS // 2, 2, SSD_HEAD_DIM, SSD_GROUPS, SSD_STATE)
    state = jnp.concatenate([st[:, grp * half:(grp + 1) * half, :, :, grp, :] for grp in range(SSD_GROUPS)], axis=1)
    state = state.reshape(b, SSD_HEADS, SSD_HEAD_DIM, SSD_STATE)
    return xo, state, xtail[:, 8 - (SSD_CONV - 1):], utail[:, 32 - (CF_WIDTH - 1):]


def _hg_levels(T):
    t = np.arange(T)
    le = (t[None, :] <= t[:, None]).astype(np.float32)
    ds, ms = [], []
    m = 1
    while m < T:
        rb = (t // (2 * m)) * 2 * m + m - 1
        if m < 8:
            ds.append(le - (t[None, :] <= rb[:, None]).astype(np.float32))
        same = (t[:, None] // (2 * m)) == (t[None, :] // (2 * m))
        ms.append((same & ((t[:, None] % (2 * m)) >= m) & ((t[None, :] % (2 * m)) < m)).astype(np.float32))
        m *= 2
    return (jnp.asarray(le, BF16), jnp.asarray(np.stack(ds), BF16), jnp.asarray(np.stack(ms), BF16))


def _rope_tables(pos):
    half = ROPE_DIM // 2
    inv = ROPE_THETA ** (-np.arange(half, dtype=np.float64) / half)
    ang = pos.astype(np.float64)[:, None] * inv[None, :]
    cos, sin = np.cos(ang), np.sin(ang)
    n = pos.shape[0]
    one = np.ones((n, MOBA_HEAD_DIM - ROPE_DIM))
    zero = np.zeros((n, MOBA_HEAD_DIM - ROPE_DIM))
    zh = np.zeros((n, half))
    c = np.concatenate([cos, cos, one], axis=1)
    s1 = np.concatenate([-sin, zh, zero], axis=1)
    s2 = np.concatenate([zh, sin, zero], axis=1)
    return tuple(jnp.asarray(np.concatenate([a, a], axis=1), F32) for a in (c, s1, s2))


def _rope(x, c, s1, s2):
    outs = []
    for j in range(x.shape[1] // LANES):
        blk = x[:, j * LANES:(j + 1) * LANES]
        outs.append(blk * c + pltpu.roll(blk, LANES - ROPE_DIM // 2, 1) * s1 + pltpu.roll(blk, ROPE_DIM // 2, 1) * s2)
    return outs


def _hg_lower_bound(lb_ref, layer):
    a = lb_ref[...]
    mx = jnp.max(a, axis=0, keepdims=True)
    e = jnp.exp(a - mx)
    return jnp.sum(e[1:layer + 1], axis=0, keepdims=True) / jnp.sum(e, axis=0, keepdims=True)


def _odd_prompt_body(x_ref, g_ref, win_ref, c_ref, s1_ref, s2_ref, lb_ref, gn_ref, tril_ref, dm_ref, mm_ref,
                     q_ref, kb_ref, vt_ref, km_ref, ko_ref, vo_ref, o_ref, hs_ref,
                     st_ref, *, nt, layer):
    t = pl.program_id(1)
    T = TILE
    W = MOBA_WIDTH

    @pl.when(t == 0)
    def _():
        st_ref[...] = jnp.zeros(st_ref.shape, F32)

    x = x_ref[0]
    hb = _rms(x, g_ref[...]).astype(BF16)
    proj = jnp.dot(hb, win_ref[...], preferred_element_type=F32)

    c, s1, s2 = c_ref[...], s1_ref[...], s2_ref[...]
    qb = _rope(proj[:, 0:W], c, s1, s2)
    kb = _rope(proj[:, W:2 * W], c, s1, s2)
    v = proj[:, 2 * W:3 * W]
    for j in range(W // LANES):
        lanes = slice(j * LANES, (j + 1) * LANES)
        q_ref[0, :, lanes] = qb[j]
        kb_ref[0, :, lanes] = kb[j].astype(BF16)
        km_ref[0, 0, :, lanes] = jnp.mean(kb[j], axis=0, keepdims=True)
        kt = kb[j].T
        vt = v[:, lanes].T
        for e in range(2):
            ko_ref[0, 2 * j + e] = kt[e * 64:(e + 1) * 64, :]
            vo_ref[0, 2 * j + e] = vt[e * 64:(e + 1) * 64, :]
            r0 = (2 * j + e) * VT_ROWS
            vt_ref[0, r0:r0 + 64, :] = vt[e * 64:(e + 1) * 64, :].astype(BF16)
            vt_ref[0, r0 + 64:r0 + VT_ROWS, :] = (
                lax.broadcasted_iota(jnp.int32, (VT_ROWS - 64, T), 0) == 0).astype(BF16)

    lb = _hg_lower_bound(lb_ref, layer)
    hq = _silu(proj[:, 3 * W:3 * W + HG_WIDTH])
    f = lb + (1.0 - lb) * _sigmoid(proj[:, 3 * W + HG_WIDTH:3 * W + 2 * HG_WIDTH])
    hv = proj[:, 3 * W + 2 * HG_WIDTH:3 * W + 3 * HG_WIDTH]
    hgate = proj[:, 3 * W + 3 * HG_WIDTH:3 * W + 4 * HG_WIDTH]
    kk = 1.0 - f
    lf3 = _split3(jnp.log(f))
    cum = _dot3_r(tril_ref[...], lf3)
    nlev = mm_ref.shape[0]
    ri = lax.broadcasted_iota(jnp.int32, (T, T), 0)
    ci = lax.broadcasted_iota(jnp.int32, (T, T), 1)
    eye = ri == ci
    amat = []
    for h in range(HG_HEADS):
        lanes = slice(h * HG_K, (h + 1) * HG_K)
        diag = jnp.sum(hq[:, lanes] * kk[:, lanes], axis=1, keepdims=True)
        amat.append(jnp.where(eye, diag, 0.0))
    for lev in range(nlev):
        m = 1 << lev
        if lev < dm_ref.shape[0]:
            d = _dot3_r(dm_ref[lev], lf3)
        else:
            refs = [jnp.broadcast_to(cum[r0 + m - 1:r0 + m, :], (2 * m, cum.shape[1])) for r0 in range(0, T, 2 * m)]
            d = cum - (refs[0] if len(refs) == 1 else jnp.concatenate(refs, axis=0))
        e_all = jnp.exp(-jnp.abs(d))
        mk = mm_ref[lev].astype(F32)
        for h in range(HG_HEADS):
            lanes = slice(h * HG_K, (h + 1) * HG_K)
            a_m = _dot_nt(hq[:, lanes] * e_all[:, lanes], kk[:, lanes] * e_all[:, lanes])
            amat[h] = amat[h] + a_m * mk
    cum_last = cum[T - 1:T, :]
    ecum = jnp.exp(cum)
    ktail = kk * jnp.exp(cum_last - cum)
    elast = jnp.exp(cum_last)
    for h in range(HG_HEADS):
        lanes = slice(h * HG_K, (h + 1) * HG_K)
        vh = hv[:, lanes]
        vhb = vh.astype(BF16)
        st = st_ref[h]
        o = jnp.dot(amat[h].astype(BF16), vhb, preferred_element_type=F32)
        o = o + _dot_nt(hq[:, lanes] * ecum[:, lanes], st)
        st_new = elast[:, lanes] * st + jnp.dot(vh.T.astype(BF16), ktail[:, lanes].astype(BF16),
                                                preferred_element_type=F32)
        st_ref[h] = st_new
        o_ref[0, :, lanes] = _rms(o, gn_ref[:, lanes]) * _silu(hgate[:, lanes])

    @pl.when(t == nt - 1)
    def _():
        for h in range(HG_HEADS):
            hs_ref[0, h] = st_ref[h].T


def _odd_prompt_proj(x, w, layer):
    b, l, d = x.shape
    nt = l // TILE
    c, s1, s2 = _rope_tables(np.arange(l))
    tril, dm, mm = _hg_levels(TILE)
    consts_a = [w["g"], w["win"]]
    consts_b = [w["lb"], w["gn"], tril, dm, mm]
    tab = pl.BlockSpec((TILE, LANES), lambda i, t: (t, 0))
    W = MOBA_WIDTH
    out_shape = (jax.ShapeDtypeStruct((b, l, W), F32),
                 jax.ShapeDtypeStruct((b, l, W), BF16),
                 jax.ShapeDtypeStruct((b, MOBA_HEADS * VT_ROWS, l), BF16),
                 jax.ShapeDtypeStruct((b, nt, 1, W), F32),
                 jax.ShapeDtypeStruct((b, MOBA_HEADS, MOBA_HEAD_DIM, l), F32),
                 jax.ShapeDtypeStruct((b, MOBA_HEADS, MOBA_HEAD_DIM, l), F32),
                 jax.ShapeDtypeStruct((b, l, HG_WIDTH), F32),
                 jax.ShapeDtypeStruct((b, HG_HEADS, HG_K, HG_V), F32))
    out_specs = (pl.BlockSpec((1, TILE, W), lambda i, t: (i, t, 0)),
                 pl.BlockSpec((1, TILE, W), lambda i, t: (i, t, 0)),
                 pl.BlockSpec((1, MOBA_HEADS * VT_ROWS, TILE), lambda i, t: (i, 0, t)),
                 pl.BlockSpec((1, 1, 1, W), lambda i, t: (i, t, 0, 0)),
                 pl.BlockSpec((1, MOBA_HEADS, MOBA_HEAD_DIM, TILE), lambda i, t: (i, 0, 0, t)),
                 pl.BlockSpec((1, MOBA_HEADS, MOBA_HEAD_DIM, TILE), lambda i, t: (i, 0, 0, t)),
                 pl.BlockSpec((1, TILE, HG_WIDTH), lambda i, t: (i, t, 0)),
                 pl.BlockSpec((1, HG_HEADS, HG_K, HG_V), lambda i, t: (i, 0, 0, 0)))
    return pl.pallas_call(
        functools.partial(_odd_prompt_body, nt=nt, layer=layer),
        grid=(b, nt),
        in_specs=([pl.BlockSpec((1, TILE, d), lambda i, t: (i, t, 0))] + [_const_spec(a.shape) for a in consts_a]
                  + [tab, tab, tab] + [_const_spec(a.shape) for a in consts_b]),
        out_specs=out_specs, out_shape=out_shape,
        scratch_shapes=[pltpu.VMEM((HG_HEADS, HG_V, HG_K), F32)],
        compiler_params=_params(("arbitrary", "arbitrary")), name="odd_prompt_proj")(
            x, *consts_a, c, s1, s2, *consts_b)


def _moba_prompt_body(q_ref, k_ref, vt_ref, km_ref, o_ref, sel_ref, s_ref, p_ref):
    i = pl.program_id(2)
    T = TILE
    hd = MOBA_HEAD_DIM
    q = q_ref[0]
    km = km_ref[0, 0]
    nbp = km.shape[0]
    lane = lax.broadcasted_iota(jnp.int32, (T, LANES), 1)
    qes = [jnp.where((lane // hd) == e, q, 0.0) for e in range(2)]
    qs = [(qe * (hd ** -0.5 * math.log2(math.e))).astype(BF16) for qe in qes]
    n_trips = (i + 1) // 2
    last_blk = k_ref.shape[1] // T - 1

    def v_ext(e, off, width):
        return vt_ref[0, e * VT_ROWS:(e + 1) * VT_ROWS, pl.ds(off, width)]

    def scores(b, slot):
        off = pl.multiple_of(jnp.minimum(b, last_blk) * T, T)
        kk = k_ref[0, pl.ds(off, T), :]
        for e in range(2):
            s_ref[slot, e] = _dot_nt(kk, qs[e])

    scores(i, 1)
    scores(0, 0)
    p_ref[...] = jnp.zeros(p_ref.shape, BF16)

    blk = lax.broadcasted_iota(jnp.int32, (nbp, T), 0)
    elig = blk < i
    kh = km.astype(BF16)
    kl = (km - kh.astype(F32)).astype(BF16)
    for e in range(2):
        qh = qes[e].astype(BF16)
        ql = (qes[e] - qh.astype(F32)).astype(BF16)
        gate = _dot_nt(kh, qh) + _dot_nt(kh, ql) + _dot_nt(kl, qh)
        gcur = jnp.where(elig, gate, -jnp.inf)
        selm = jnp.zeros((nbp, T), F32)
        for _ in range(MOBA_TOPK):
            mx = jnp.max(gcur, axis=0, keepdims=True)
            first = jnp.min(jnp.where(gcur == mx, blk, nbp), axis=0, keepdims=True)
            hit = blk == first
            selm = jnp.where(hit & elig, 1.0, selm)
            gcur = jnp.where(hit, -jnp.inf, gcur)
        sel_ref[e] = selm

    ri = lax.broadcasted_iota(jnp.int32, (T, T), 0)
    ci = lax.broadcasted_iota(jnp.int32, (T, T), 1)
    own = pl.multiple_of(i * T, T)
    state = []
    for e in range(2):
        s = jnp.where(ri <= ci, s_ref[1, e], NEG)
        m0 = jnp.max(s, axis=0, keepdims=True)
        p0 = jnp.exp2((s - m0).astype(BF16))
        state += [m0, jnp.dot(v_ext(e, own, T), p0, preferred_element_type=F32)]


    def pv(b, slot, e):
        off = pl.multiple_of(jnp.clip(b, 0, last_blk) * T, T)
        return jnp.dot(v_ext(e, off, T), p_ref[slot, e], preferred_element_type=F32)

    def softmax_step(b, slot, e, m, acc, prev):
        s = jnp.where(sel_ref[e, pl.ds(b, 1), :] > 0.5, s_ref[slot, e], NEG)
        m_new = jnp.maximum(m, jnp.max(s, axis=0, keepdims=True))
        p_ref[slot, e] = jnp.exp2((s - m_new).astype(BF16))
        return m_new, jnp.exp2(m - m_new) * (acc + prev)

    def body(u, carry):
        carry = list(carry)
        scores(2 * u + 1, 1)
        for e in range(2):
            prev = pv(2 * u - 1, 1, e)
            carry[2 * e], carry[2 * e + 1] = softmax_step(2 * u, 0, e, carry[2 * e], carry[2 * e + 1], prev)
        scores(2 * u + 2, 0)
        for e in range(2):
            prev = pv(2 * u, 0, e)
            carry[2 * e], carry[2 * e + 1] = softmax_step(2 * u + 1, 1, e, carry[2 * e], carry[2 * e + 1], prev)
        return tuple(carry)

    fin = lax.fori_loop(0, n_trips, body, tuple(state))
    outs = []
    for e in range(2):
        tot = fin[2 * e + 1] + pv(2 * n_trips - 1, 1, e)
        outs.append(tot[0:hd] / tot[hd:hd + 1])
    o_ref[0] = jnp.concatenate(outs, axis=0).T


def _moba_prompt(q, kb, vt, kmean):
    b, l, w = q.shape
    nq = l // TILE
    npair = w // LANES
    nb = kmean.shape[1]
    km = kmean.reshape(b, nb, npair, LANES).transpose(0, 2, 1, 3)
    nbp = -(-nb // 8) * 8
    km = jnp.pad(km, ((0, 0), (0, 0), (0, nbp - nb), (0, 0)))
    return pl.pallas_call(
        _moba_prompt_body,
        grid=(b, npair, nq),
        in_specs=[pl.BlockSpec((1, TILE, LANES), lambda i, p, t: (i, t, p)),
                  pl.BlockSpec((1, l, LANES), lambda i, p, t: (i, 0, p)),
                  pl.BlockSpec((1, 2 * VT_ROWS, l), lambda i, p, t: (i, p, 0)),
                  pl.BlockSpec((1, 1, nbp, LANES), lambda i, p, t: (i, p, 0, 0))],
        out_specs=pl.BlockSpec((1, TILE, LANES), lambda i, p, t: (i, t, p)),
        out_shape=jax.ShapeDtypeStruct((b, l, w), F32),
        scratch_shapes=[pltpu.VMEM((2, nbp, TILE), F32), pltpu.VMEM((2, 2, TILE, TILE), F32),
                        pltpu.VMEM((2, 2, TILE, TILE), BF16)],
        compiler_params=_params(("arbitrary", "arbitrary", "arbitrary")), name="moba_prompt")(q, kb, vt, km)


def _even_sample_proj_body(x_ref, g_ref, wz_ref, wxbc_ref, wdt_ref, wga_ref, wgb_ref, cw_ref, cb_ref, dtb_ref,
                           xbuf_ref, cfw_ref, cfb_ref, cfg_ref, cfbeta_ref, ubuf_ref,
                           z_ref, xc_ref, dt_ref, xnew_ref, c_ref, unew_ref):
    x = x_ref[...]
    hb = _rms(x, g_ref[...]).astype(BF16)
    z_ref[...] = jnp.dot(hb, wz_ref[...], preferred_element_type=F32)
    xbc = jnp.dot(hb, wxbc_ref[...], preferred_element_type=F32)
    dtr = jnp.dot(hb, wdt_ref[...], preferred_element_type=F32)
    ga = jnp.dot(hb, wga_ref[...], preferred_element_type=F32)
    gb = jnp.dot(hb, wgb_ref[...], preferred_element_type=F32)
    k1 = SSD_CONV - 1
    acc = cw_ref[k1:k1 + 1, :] * xbc
    for k in range(k1):
        acc = acc + cw_ref[k:k + 1, :] * xbuf_ref[k]
        if k > 0:
            xnew_ref[k - 1] = xbuf_ref[k]
    xnew_ref[k1 - 1] = xbc
    xc_ref[...] = _silu(acc + cb_ref[...])
    dt_ref[...] = _softplus(dtr + dtb_ref[...])
    u = ga * _sigmoid(gb)
    k2 = CF_WIDTH - 1
    cacc = cfw_ref[k2:k2 + 1, :] * u
    for k in range(k2):
        cacc = cacc + cfw_ref[k:k + 1, :] * ubuf_ref[k]
        if k > 0:
            unew_ref[k - 1] = ubuf_ref[k]
    unew_ref[k2 - 1] = u
    c32 = cacc + cfb_ref[...]
    mu = jnp.mean(c32, axis=-1, keepdims=True)
    var = jnp.mean(jnp.square(c32 - mu), axis=-1, keepdims=True)
    c_ref[...] = _silu((c32 - mu) * lax.rsqrt(var + NORM_EPS) * cfg_ref[...] + cfbeta_ref[...])


def _expand_mats(n_outer, n_inner):
    j = np.arange(n_outer * n_inner)
    rep = (j[None, :] // n_inner == np.arange(n_outer)[:, None]).astype(np.float32)
    til = (j[None, :] % n_inner == np.arange(n_inner)[:, None]).astype(np.float32)
    return jnp.asarray(rep, BF16), jnp.asarray(til, BF16)


def _ssd_step_body(s_ref, x_ref, dt_ref, alog_ref, b_ref, c_ref, rep_ref, til_ref, so_ref, y_ref):
    dt = dt_ref[...]
    decay = jnp.exp(dt * (-jnp.exp(alog_ref[...])))
    xrep = _dot3_l(x_ref[...] * dt, rep_ref[...])
    btil = _dot3_l(b_ref[...], til_ref[...])
    ctil = _dot3_l(c_ref[...], til_ref[...])
    s_new = decay * s_ref[...] + xrep * btil
    so_ref[...] = s_new
    hi, mid, lo = _split3(s_new * ctil)
    rep = rep_ref[...]
    nt = (((1,), (1,)), ((), ()))
    y_ref[...] = (lax.dot_general(hi, rep, nt, preferred_element_type=F32)
                  + lax.dot_general(mid, rep, nt, preferred_element_type=F32)
                  + lax.dot_general(lo, rep, nt, preferred_element_type=F32))


def _even_sample_out_body(x_ref, y_ref, xs_ref, z_ref, dfull_ref, sn_ref, c_ref, woy_ref, woc_ref, o_ref):
    y = (y_ref[...] + dfull_ref[...] * xs_ref[...]) * _silu(z_ref[...])
    gw = SSD_INNER // SSD_GROUPS
    out = _dot(c_ref[...], woc_ref[...])
    for grp in range(SSD_GROUPS):
        lanes = slice(grp * gw, (grp + 1) * gw)
        out = out + _dot(_rms(y[:, lanes], sn_ref[:, lanes]), woy_ref[lanes, :])
    o_ref[...] = x_ref[...] + out


def _call(body, out_shape, *args, name):
    return pl.pallas_call(body, out_shape=out_shape, compiler_params=_params(None), name=name)(*args)


def _even_sample(x, w, s_ssd, buf_ssd, buf_cf):
    nb = x.shape[0]
    f = lambda *s: jax.ShapeDtypeStruct(s, F32)
    z, xc, dt, xnew, c, unew = _call(
        _even_sample_proj_body,
        (f(nb, SSD_INNER), f(nb, SSD_CONV_DIM), f(nb, LANES), f(SSD_CONV - 1, nb, SSD_CONV_DIM), f(nb, CF_CH),
         f(CF_WIDTH - 1, nb, CF_CH)),
        x, w["g"], w["wz"], w["wxbc"], w["wdt"], w["wga"], w["wgb"], w["cw"], w["cb"], w["dtb"],
        jnp.swapaxes(buf_ssd, 0, 1), w["cfw"], w["cfb"], w["cfg"], w["cfbeta"], jnp.swapaxes(buf_cf, 0, 1),
        name="even_sample_proj")
    xs = xc[:, :SSD_INNER]
    rows = nb * SSD_HEADS
    rep_heads = SSD_HEADS // SSD_GROUPS
    grp = lambda a: jnp.repeat(a.reshape(nb, SSD_GROUPS, SSD_STATE), rep_heads, axis=1).reshape(rows, SSD_STATE)
    bm = grp(xc[:, SSD_INNER:SSD_INNER + SSD_GROUPS * SSD_STATE])
    cm = grp(xc[:, SSD_INNER + SSD_GROUPS * SSD_STATE:])
    rep, til = _expand_mats(SSD_HEAD_DIM, SSD_STATE)
    pn = SSD_HEAD_DIM * SSD_STATE
    rb = 128
    rowspec = lambda cdim: pl.BlockSpec((rb, cdim), lambda i: (i, 0))
    s_new, y = pl.pallas_call(
        _ssd_step_body, grid=(rows // rb,),
        in_specs=[rowspec(pn), rowspec(SSD_HEAD_DIM), rowspec(1), rowspec(1), rowspec(SSD_STATE), rowspec(SSD_STATE),
                  _const_spec(rep.shape), _const_spec(til.shape)],
        out_specs=(rowspec(pn), rowspec(SSD_HEAD_DIM)),
        out_shape=(f(rows, pn), f(rows, SSD_HEAD_DIM)),
        compiler_params=_params(("parallel",)), name="ssd_step")(
            s_ssd.reshape(rows, pn), xs.reshape(rows, SSD_HEAD_DIM), dt[:, :SSD_HEADS].reshape(rows, 1),
            jnp.tile(w["alog"][0, :SSD_HEADS], nb).reshape(rows, 1), bm, cm, rep, til)
    xo = _call(_even_sample_out_body, f(nb, x.shape[1]),
               x, y.reshape(nb, SSD_INNER), xs, z, w["dfull"], w["sn"], c, w["woy"], w["woc"],
               name="even_sample_out")
    return (xo, s_new.reshape(s_ssd.shape), jnp.swapaxes(xnew, 0, 1), jnp.swapaxes(unew, 0, 1))


def _odd_sample_proj_body(x_ref, g_ref, win_ref, c_ref, s1_ref, s2_ref, lb_ref,
                          q_ref, k_ref, v_ref, hq_ref, f_ref, hv_ref, hg_ref, *, layer):
    W = MOBA_WIDTH
    hb = _rms(x_ref[...], g_ref[...]).astype(BF16)
    proj = jnp.dot(hb, win_ref[...], preferred_element_type=F32)
    c, s1, s2 = c_ref[...], s1_ref[...], s2_ref[...]
    qb = _rope(proj[:, 0:W], c, s1, s2)
    kb = _rope(proj[:, W:2 * W], c, s1, s2)
    for j in range(W // LANES):
        q_ref[:, j * LANES:(j + 1) * LANES] = qb[j]
        k_ref[:, j * LANES:(j + 1) * LANES] = kb[j]
    v_ref[...] = proj[:, 2 * W:3 * W]
    lb = _hg_lower_bound(lb_ref, layer)
    hq_ref[...] = _silu(proj[:, 3 * W:3 * W + HG_WIDTH])
    f_ref[...] = lb + (1.0 - lb) * _sigmoid(proj[:, 3 * W + HG_WIDTH:3 * W + 2 * HG_WIDTH])
    hv_ref[...] = proj[:, 3 * W + 2 * HG_WIDTH:3 * W + 3 * HG_WIDTH]
    hg_ref[...] = proj[:, 3 * W + 3 * HG_WIDTH:3 * W + 4 * HG_WIDTH]


def _hg_step_body(s_ref, q_ref, f_ref, v_ref, rep_ref, til_ref, so_ref, o_ref):
    f = f_ref[...]
    frep = _dot3_l(f, rep_ref[...])
    krep = _dot3_l(1.0 - f, rep_ref[...])
    qrep = _dot3_l(q_ref[...], rep_ref[...])
    vtil = _dot3_l(v_ref[...], til_ref[...])
    s_new = frep * s_ref[...] + krep * vtil
    so_ref[...] = s_new
    hi, mid, lo = _split3(s_new * qrep)
    til = til_ref[...]
    nt = (((1,), (1,)), ((), ()))
    o_ref[...] = (lax.dot_general(hi, til, nt, preferred_element_type=F32)
                  + lax.dot_general(mid, til, nt, preferred_element_type=F32)
                  + lax.dot_general(lo, til, nt, preferred_element_type=F32))


GATE_CHUNK = 16


def _gate_copies(pt_ref, ck_ref, kbuf, sem, layer, seq, chunk, slot):
    return [pltpu.make_async_copy(ck_ref.at[layer, pt_ref[seq, chunk * GATE_CHUNK + j]], kbuf.at[slot, j],
                                  sem.at[slot]) for j in range(GATE_CHUNK)]


def _sample_gate_body(pt_ref, ck_ref, q_ref, idx_ref, kbuf, sem, g_ref, *, layer, n_chunks, n_seq):
    b = pl.program_id(0)
    c = pl.program_id(1)
    step = b * n_chunks + c
    slot = step % 2
    ppb = MOBA_BLOCK // PAGE_SIZE
    nblk = n_chunks * GATE_CHUNK // ppb

    @pl.when(step == 0)
    def _():
        for cp in _gate_copies(pt_ref, ck_ref, kbuf, sem, layer, b, c, slot):
            cp.start()

    nxt = step + 1

    @pl.when(nxt < n_seq * n_chunks)
    def _():
        for cp in _gate_copies(pt_ref, ck_ref, kbuf, sem, layer, nxt // n_chunks, nxt % n_chunks, 1 - slot):
            cp.start()

    for cp in _gate_copies(pt_ref, ck_ref, kbuf, sem, layer, b, c, slot):
        cp.wait()

    @pl.when(c == 0)
    def _():
        g_ref[...] = jnp.zeros(g_ref.shape, F32)

    lane = lax.broadcasted_iota(jnp.int32, (MOBA_HEADS, LANES), 1)
    sub = lax.broadcasted_iota(jnp.int32, (MOBA_HEADS, LANES), 0)
    g = g_ref[...]
    for j in range(GATE_CHUNK):
        tile = jnp.zeros((MOBA_HEADS, LANES), F32)
        for h in range(MOBA_HEADS):
            t = jnp.sum(kbuf[slot, j, h] * q_ref[0, h], axis=0, keepdims=True)
            tile = jnp.where(sub == h, t, tile)
        col = jnp.sum(tile, axis=1, keepdims=True)
        g = jnp.where(lane == (c * GATE_CHUNK + j) // ppb, g + col, g)
    g_ref[...] = g

    @pl.when(c == n_chunks - 1)
    def _():
        gcur = jnp.where(lane < nblk, g * (1.0 / MOBA_BLOCK), -jnp.inf)
        out = jnp.zeros((MOBA_HEADS, LANES), jnp.int32)
        for r in range(MOBA_TOPK):
            mx = jnp.max(gcur, axis=1, keepdims=True)
            first = jnp.min(jnp.where(gcur == mx, lane, LANES), axis=1, keepdims=True)
            out = jnp.where(lane == r, first, out)
            gcur = jnp.where(lane == first, -jnp.inf, gcur)
        idx_ref[0] = out


def _attn_copies(idx_ref, pt_ref, ck_ref, cv_ref, kbuf, vbuf, sem, layer, seq, slot):
    ppb = MOBA_BLOCK // PAGE_SIZE
    cps = []
    for h in range(MOBA_HEADS):
        for r in range(MOBA_TOPK):
            blk = idx_ref[seq, h * MOBA_TOPK + r]
            for pg in range(ppb):
                phys = pt_ref[seq, blk * ppb + pg]
                j = r * ppb + pg
                cps.append(pltpu.make_async_copy(ck_ref.at[layer, phys, h], kbuf.at[slot, h, j], sem.at[0, slot]))
                cps.append(pltpu.make_async_copy(cv_ref.at[layer, phys, h], vbuf.at[slot, h, j], sem.at[1, slot]))
    return cps


def _sample_attn_body(idx_ref, pt_ref, ck_ref, cv_ref, q_ref, kn_ref, vn_ref, o_ref, kbuf, vbuf, sem, *,
                      layer, n_seq):
    b = pl.program_id(0)
    slot = b % 2
    scale = MOBA_HEAD_DIM ** -0.5
    nsel = MOBA_TOPK * (MOBA_BLOCK // PAGE_SIZE)

    @pl.when(b == 0)
    def _():
        for cp in _attn_copies(idx_ref, pt_ref, ck_ref, cv_ref, kbuf, vbuf, sem, layer, b, slot):
            cp.start()

    @pl.when(b + 1 < n_seq)
    def _():
        for cp in _attn_copies(idx_ref, pt_ref, ck_ref, cv_ref, kbuf, vbuf, sem, layer, b + 1, 1 - slot):
            cp.start()

    for cp in _attn_copies(idx_ref, pt_ref, ck_ref, cv_ref, kbuf, vbuf, sem, layer, b, slot):
        cp.wait()

    for h in range(MOBA_HEADS):
        qc = q_ref[0, h]
        s_self = jnp.sum(qc * kn_ref[0, h], axis=0, keepdims=True) * scale
        ss = [jnp.sum(kbuf[slot, h, j] * qc, axis=0, keepdims=True) * scale for j in range(nsel)]
        m = s_self
        for s in ss:
            m = jnp.maximum(m, jnp.max(s, axis=1, keepdims=True))
        p_self = jnp.exp(s_self - m)
        l = p_self
        acc = jnp.zeros((MOBA_HEAD_DIM, LANES), F32)
        for j in range(nsel):
            p = jnp.exp(ss[j] - m)
            l = l + jnp.sum(p, axis=1, keepdims=True)
            acc = acc + vbuf[slot, h, j] * p
        o_ref[0, h] = (vn_ref[0, h] * p_self + jnp.sum(acc, axis=1, keepdims=True)) / l


def _odd_sample_out_body(x_ref, att_ref, o_ref, hg_ref, gn_ref, woa_ref, woh_ref, xo_ref):
    out = _dot(att_ref[...], woa_ref[...])
    o = o_ref[...]
    gated = []
    for h in range(HG_HEADS):
        lanes = slice(h * HG_V, (h + 1) * HG_V)
        gated.append(_rms(o[:, lanes], gn_ref[:, lanes]) * _silu(hg_ref[:, lanes]))
    out = out + _dot(jnp.concatenate(gated, axis=1), woh_ref[...])
    xo_ref[...] = x_ref[...] + out


def _odd_sample(x, w, layer, s_hg, cache_kt, cache_vt, cache_layer, page_table, past_len):
    nb = x.shape[0]
    f = lambda *s: jax.ShapeDtypeStruct(s, F32)
    n_pages = page_table.shape[1]
    ppb = MOBA_BLOCK // PAGE_SIZE
    nblk = n_pages // ppb
    assert n_pages % ppb == 0 and nblk >= MOBA_TOPK and nblk < LANES
    c, s1, s2 = _rope_tables(np.full((1,), past_len))
    q, k, v, hq, fg, hv, hg = _call(
        functools.partial(_odd_sample_proj_body, layer=layer),
        tuple(f(nb, MOBA_WIDTH) for _ in range(7)),
        x, w["g"], w["win"], c, s1, s2, w["lb"], name="odd_sample_proj")

    rows = nb * HG_HEADS
    rep, til = _expand_mats(HG_K, HG_V)
    kv = HG_K * HG_V
    rb = 32
    rowspec = lambda cdim: pl.BlockSpec((rb, cdim), lambda i: (i, 0))
    s_new, o = pl.pallas_call(
        _hg_step_body, grid=(rows // rb,),
        in_specs=[rowspec(kv), rowspec(HG_K), rowspec(HG_K), rowspec(HG_V), _const_spec(rep.shape),
                  _const_spec(til.shape)],
        out_specs=(rowspec(kv), rowspec(HG_V)), out_shape=(f(rows, kv), f(rows, HG_V)),
        compiler_params=_params(("parallel",)), name="hg_step")(
            s_hg.reshape(rows, kv), hq.reshape(rows, HG_K), fg.reshape(rows, HG_K), hv.reshape(rows, HG_V), rep, til)

    hd = MOBA_HEAD_DIM
    assert n_pages % GATE_CHUNK == 0 and GATE_CHUNK % ppb == 0
    n_chunks = n_pages // GATE_CHUNK
    col = lambda a: jnp.broadcast_to(a.reshape(nb, MOBA_HEADS, hd, 1), (nb, MOBA_HEADS, hd, LANES))
    qcol, kcol, vcol = col(q), col(k), col(v)
    colspec = lambda nidx: pl.BlockSpec((1, MOBA_HEADS, hd, LANES), lambda i, *_: (i, 0, 0, 0))
    idx = pl.pallas_call(
        functools.partial(_sample_gate_body, layer=cache_layer, n_chunks=n_chunks, n_seq=nb),
        grid_spec=pltpu.PrefetchScalarGridSpec(
            num_scalar_prefetch=1, grid=(nb, n_chunks),
            in_specs=[pl.BlockSpec(memory_space=pl.ANY), colspec(1)],
            out_specs=pl.BlockSpec((1, MOBA_HEADS, LANES), lambda i, c, pt: (i, 0, 0)),
            scratch_shapes=[pltpu.VMEM((2, GATE_CHUNK, MOBA_HEADS, hd, PAGE_SIZE), F32),
                            pltpu.SemaphoreType.DMA((2,)), pltpu.VMEM((MOBA_HEADS, LANES), F32)]),
        out_shape=jax.ShapeDtypeStruct((nb, MOBA_HEADS, LANES), jnp.int32),
        compiler_params=_params(("arbitrary", "arbitrary")), name="sample_gate")(page_table, cache_kt, qcol)
    nsel = MOBA_TOPK * ppb
    att = pl.pallas_call(
        functools.partial(_sample_attn_body, layer=cache_layer, n_seq=nb),
        grid_spec=pltpu.PrefetchScalarGridSpec(
            num_scalar_prefetch=2, grid=(nb,),
            in_specs=[pl.BlockSpec(memory_space=pl.ANY), pl.BlockSpec(memory_space=pl.ANY),
                      colspec(2), colspec(2), colspec(2)],
            out_specs=colspec(2),
            scratch_shapes=[pltpu.VMEM((2, MOBA_HEADS, nsel, hd, PAGE_SIZE), F32),
                            pltpu.VMEM((2, MOBA_HEADS, nsel, hd, PAGE_SIZE), F32),
                            pltpu.SemaphoreType.DMA((2, 2))]),
        out_shape=f(nb, MOBA_HEADS, hd, LANES),
        compiler_params=_params(("arbitrary",)), name="sample_attn")(
            idx[:, :, :MOBA_TOPK].reshape(nb, MOBA_HEADS * MOBA_TOPK), page_table, cache_kt, cache_vt,
            qcol, kcol, vcol)
    att = att[:, :, :, 0].reshape(nb, MOBA_WIDTH)
    kn = k.reshape(nb, MOBA_HEADS, 1, hd)
    vn = v.reshape(nb, MOBA_HEADS, 1, hd)

    xo = _call(_odd_sample_out_body, f(nb, x.shape[1]),
               x, att, o.reshape(nb, HG_WIDTH), hg, w["gn"], w["woa"], w["woh"], name="odd_sample_out")
    return xo, s_new.reshape(s_hg.shape), kn, vn


def _row(a):
    return a.reshape(1, -1).astype(F32)


def _pad_lanes(a, n=LANES):
    return jnp.pad(a, [(0, 0)] * (a.ndim - 1) + [(0, n - a.shape[-1])])


def kernel(x_prompt, x_sample, state_ssd, state_ssd_conv, state_cf_conv, cache_k, cache_v, page_table, state_hg,
           ffn1_norm, ffn1_w_gu, ffn1_w_down, mix_norm, ffn2_norm, ffn2_w_gu, ffn2_w_down, final_norm,
           even_w_in, ssd_conv_w, ssd_conv_b, ssd_dt_bias, ssd_a_log, ssd_d, ssd_norm,
           cf_dw_w, cf_dw_b, cf_ln_g, cf_ln_b, even_w_out,
           odd_w_in, hg_lower_bound, hg_norm, odd_w_out):
    depth = ffn1_norm.shape[0]
    bp, lp, d = x_prompt.shape
    nb = x_sample.shape[0]
    assert x_sample.shape[1] == 1 and lp % TILE == 0 and lp // TILE <= LANES
    d_ff = ffn1_w_down.shape[1]
    past_len = page_table.shape[1] * PAGE_SIZE

    def ffn_w(norm, w_gu, w_down, l):
        return (_row(norm[l]), w_gu[l, :, :d_ff].astype(BF16), w_gu[l, :, d_ff:].astype(BF16), w_down[l].astype(BF16))

    def even_w(e, l):
        wi = even_w_in[e]
        o1 = SSD_INNER
        o2 = o1 + SSD_CONV_DIM
        o3 = o2 + SSD_HEADS
        sel = (np.arange(LANES)[:, None] == np.arange(SSD_INNER)[None, :] // SSD_HEAD_DIM).astype(np.float32)
        return dict(
            g=_row(mix_norm[l]), wz=wi[:, :o1].astype(BF16), wxbc=wi[:, o1:o2].astype(BF16),
            wdt=_pad_lanes(wi[:, o2:o3]).astype(BF16), wga=wi[:, o3:o3 + CF_CH].astype(BF16),
            wgb=wi[:, o3 + CF_CH:].astype(BF16), cw=ssd_conv_w[e], cb=_row(ssd_conv_b[e]),
            dtb=_pad_lanes(_row(ssd_dt_bias[e])), alog=_pad_lanes(_row(ssd_a_log[e])),
            dfull=_row(jnp.repeat(ssd_d[e], SSD_HEAD_DIM)), sn=_row(ssd_norm[e]), sel=jnp.asarray(sel, BF16),
            cfw=cf_dw_w[e], cfb=_row(cf_dw_b[e]), cfg=_row(cf_ln_g[e]), cfbeta=_row(cf_ln_b[e]),
            woy=even_w_out[e, :SSD_INNER].astype(BF16), woc=even_w_out[e, SSD_INNER:].astype(BF16))

    def odd_w(o, l):
        return dict(g=_row(mix_norm[l]), win=odd_w_in[o].astype(BF16), lb=hg_lower_bound.astype(F32),
                    gn=_row(hg_norm[o]), woa=odd_w_out[o, :MOBA_WIDTH].astype(BF16),
                    woh=odd_w_out[o, MOBA_WIDTH:].astype(BF16))

    cache_kt = jnp.swapaxes(cache_k, -1, -2)
    cache_vt = jnp.swapaxes(cache_v, -1, -2)
    xp = x_prompt.reshape(bp * lp, d)
    xs = x_sample.reshape(nb, d)
    ssd_p, ssd_s, sc_p, sc_s, cf_p, cf_s = [], [], [], [], [], []
    k_p, k_s, v_p, v_s, hg_p, hg_s = [], [], [], [], [], []
    pre_p = pre_s = None
    for l in range(depth):
        w1 = ffn_w(ffn1_norm, ffn1_w_gu, ffn1_w_down, l)
        w2 = ffn_w(ffn2_norm, ffn2_w_gu, ffn2_w_down, l)
        fin = _row(final_norm) if l == depth - 1 else None
        xp = _ffn(xp, w1, tm=TILE)
        xs = _ffn(xs, w1, tm=nb)
        if l % 2 == 0:
            e = l // 2
            w = even_w(e, l)
            xp3, st, xt, ut = _even_prompt(xp.reshape(bp, lp, d), w)
            xp = xp3.reshape(bp * lp, d)
            ssd_p.append(st)
            sc_p.append(xt)
            cf_p.append(ut)
            xs, st, xt, ut = _even_sample(xs, w, state_ssd[e], state_ssd_conv[e], state_cf_conv[e])
            ssd_s.append(st)
            sc_s.append(xt)
            cf_s.append(ut)
            xp = _ffn(xp, w2, tm=TILE, final_g=fin)
        else:
            o = l // 2
            w = odd_w(o, l)
            q, kb, vt, kmean, ko, vo, ohg, hst = _odd_prompt_proj(xp.reshape(bp, lp, d), w, l)
            att = _moba_prompt(q, kb, vt, kmean[:, :, 0, :])
            k_p.append(jnp.swapaxes(ko, -1, -2))
            v_p.append(jnp.swapaxes(vo, -1, -2))
            hg_p.append(hst)
            xp = _ffn(xp, w2, tm=TILE, final_g=fin,
                      pre=(att.reshape(bp * lp, MOBA_WIDTH), ohg.reshape(bp * lp, HG_WIDTH), w["woa"], w["woh"]))
            xs, hst, kn, vn = _odd_sample(xs, w, l, state_hg[o], cache_kt, cache_vt, o, page_table, past_len)
            k_s.append(kn)
            v_s.append(vn)
            hg_s.append(hst)
        xs = _ffn(xs, w2, tm=nb, final_g=fin)
    st = jnp.stack
    return (xp.reshape(bp, lp, d), xs.reshape(nb, 1, d), st(ssd_p), st(ssd_s), st(sc_p), st(sc_s), st(cf_p), st(cf_s),
            st(k_p), st(k_s), st(v_p), st(v_s), st(hg_p), st(hg_s))
```

```python
import functools
import math

import numpy as np
import jax
import jax.numpy as jnp
from jax import lax
from jax.experimental import pallas as pl
from jax.experimental.pallas import tpu as pltpu

F32 = jnp.float32
BF16 = jnp.bfloat16

SSD_HEADS = 16
SSD_HEAD_DIM = 64
SSD_INNER = SSD_HEADS * SSD_HEAD_DIM
SSD_GROUPS = 2
SSD_STATE = 64
SSD_CONV = 4
SSD_CONV_DIM = SSD_INNER + 2 * SSD_GROUPS * SSD_STATE
CF_CH = 512
CF_WIDTH = 31
MOBA_HEADS = 8
MOBA_HEAD_DIM = 64
MOBA_WIDTH = MOBA_HEADS * MOBA_HEAD_DIM
MOBA_BLOCK = 256
MOBA_TOPK = 3
ROPE_DIM = MOBA_HEAD_DIM // 4
ROPE_THETA = 500000.0
HG_HEADS = 4
HG_K = 128
HG_V = 128
HG_WIDTH = HG_HEADS * HG_K
PAGE_SIZE = 128
NORM_EPS = 1e-6
NEG = -1e30

LANES = 128
SSD_CHUNK = 128
TILE = 256
VT_ROWS = MOBA_HEAD_DIM + 16
VMEM_LIMIT = 56 * 1024 * 1024


def _sigmoid(x):
    return 1.0 / (1.0 + jnp.exp(-x))


def _silu(x):
    return x * _sigmoid(x)


def _softplus(x):
    return jnp.maximum(x, 0.0) + jnp.log1p(jnp.exp(-jnp.abs(x)))


def _rms(x, g):
    return x * lax.rsqrt(jnp.mean(x * x, axis=-1, keepdims=True) + NORM_EPS) * g


def _dot(a, b):
    return jnp.dot(a.astype(BF16), b.astype(BF16), preferred_element_type=F32)


def _dot_nt(a, b):
    return lax.dot_general(a.astype(BF16), b.astype(BF16), (((1,), (1,)), ((), ())),
                           preferred_element_type=F32)


def _split3(a):
    hi = a.astype(BF16)
    r = a - hi.astype(F32)
    mid = r.astype(BF16)
    lo = (r - mid.astype(F32)).astype(BF16)
    return hi, mid, lo


def _dot3_l(a, m):
    hi, mid, lo = _split3(a)
    return (jnp.dot(hi, m, preferred_element_type=F32) + jnp.dot(mid, m, preferred_element_type=F32)
            + jnp.dot(lo, m, preferred_element_type=F32))


def _dot3_r(m, parts):
    hi, mid, lo = parts
    return (jnp.dot(m, hi, preferred_element_type=F32) + jnp.dot(m, mid, preferred_element_type=F32)
            + jnp.dot(m, lo, preferred_element_type=F32))


def _const_spec(shape):
    n = len(shape)
    return pl.BlockSpec(shape, lambda *_: (0,) * n, pipeline_mode=pl.Buffered(1))


def _params(sem):
    return pltpu.CompilerParams(dimension_semantics=sem, vmem_limit_bytes=VMEM_LIMIT)


GATE_CHUNK = 16


def _gate_copies(pt_ref, ck_ref, kbuf, sem, layer, seq, chunk, slot):
    return [pltpu.make_async_copy(ck_ref.at[layer, pt_ref[seq, chunk * GATE_CHUNK + j]], kbuf.at[slot, j],
                                  sem.at[slot]) for j in range(GATE_CHUNK)]


def _gate_fetch(pt_ref, ck_ref, kbuf, sem, *, layer, seq0, n_chunks, n_steps):
    step = pl.program_id(0)
    slot = step % 2
    seq = seq0 + step // n_chunks
    c = step % n_chunks

    @pl.when(step == 0)
    def _():
        for cp in _gate_copies(pt_ref, ck_ref, kbuf, sem, layer, seq, c, slot):
            cp.start()

    nxt = step + 1

    @pl.when(nxt < n_steps)
    def _():
        for cp in _gate_copies(pt_ref, ck_ref, kbuf, sem, layer, seq0 + nxt // n_chunks, nxt % n_chunks, 1 - slot):
            cp.start()

    for cp in _gate_copies(pt_ref, ck_ref, kbuf, sem, layer, seq, c, slot):
        cp.wait()
    return slot, c


def _gate_accumulate(kbuf, q_ref, g_ref, slot, c):
    ppb = MOBA_BLOCK // PAGE_SIZE

    @pl.when(c == 0)
    def _():
        g_ref[...] = jnp.zeros(g_ref.shape, F32)

    lane = lax.broadcasted_iota(jnp.int32, (MOBA_HEADS, LANES), 1)
    sub = lax.broadcasted_iota(jnp.int32, (MOBA_HEADS, LANES), 0)
    g = g_ref[...]
    for jb in range(GATE_CHUNK // ppb):
        tile = jnp.zeros((MOBA_HEADS, LANES), F32)
        for h in range(MOBA_HEADS):
            ksum = kbuf[slot, jb * ppb, h]
            for r in range(1, ppb):
                ksum = ksum + kbuf[slot, jb * ppb + r, h]
            t = jnp.sum(ksum * q_ref[0, h], axis=0, keepdims=True)
            tile = jnp.where(sub == h, t, tile)
        col = jnp.sum(tile, axis=1, keepdims=True)
        g = jnp.where(lane == c * (GATE_CHUNK // ppb) + jb, g + col, g)
    g_ref[...] = g


def _gate_finish(g_ref, idx_ref, c, n_chunks):
    nblk = n_chunks * GATE_CHUNK // (MOBA_BLOCK // PAGE_SIZE)

    @pl.when(c == n_chunks - 1)
    def _():
        lane = lax.broadcasted_iota(jnp.int32, (MOBA_HEADS, LANES), 1)
        gcur = jnp.where(lane < nblk, g_ref[...] * (1.0 / MOBA_BLOCK), -jnp.inf)
        out = jnp.zeros((MOBA_HEADS, LANES), jnp.int32)
        for r in range(MOBA_TOPK):
            mx = jnp.max(gcur, axis=1, keepdims=True)
            first = jnp.min(jnp.where(gcur == mx, lane, LANES), axis=1, keepdims=True)
            out = jnp.where(lane == r, first, out)
            gcur = jnp.where(lane == first, -jnp.inf, gcur)
        idx_ref[0] = out


def _gate_scratch():
    return [pltpu.VMEM((2, GATE_CHUNK, MOBA_HEADS, MOBA_HEAD_DIM, PAGE_SIZE), F32), pltpu.SemaphoreType.DMA((2,)),
            pltpu.VMEM((MOBA_HEADS, LANES), F32)]


def _sample_gate_body(pt_ref, ck_ref, q_ref, idx_ref, kbuf, sem, g_ref, **job):
    slot, c = _gate_fetch(pt_ref, ck_ref, kbuf, sem, **job)
    _gate_accumulate(kbuf, q_ref, g_ref, slot, c)
    _gate_finish(g_ref, idx_ref, c, job["n_chunks"])


def _sample_gate(page_table, cache_kt, qcol, layer):
    nb, n_pages = page_table.shape
    n_chunks = n_pages // GATE_CHUNK
    job = dict(layer=layer, seq0=0, n_chunks=n_chunks, n_steps=nb * n_chunks)
    return pl.pallas_call(
        functools.partial(_sample_gate_body, **job),
        grid_spec=pltpu.PrefetchScalarGridSpec(
            num_scalar_prefetch=1, grid=(nb * n_chunks,),
            in_specs=[pl.BlockSpec(memory_space=pl.ANY),
                      pl.BlockSpec((1,) + qcol.shape[1:], lambda s, pt: (s // n_chunks, 0, 0, 0))],
            out_specs=pl.BlockSpec((1, MOBA_HEADS, LANES), lambda s, pt: (s // n_chunks, 0, 0)),
            scratch_shapes=_gate_scratch()),
        out_shape=jax.ShapeDtypeStruct((nb, MOBA_HEADS, LANES), jnp.int32),
        compiler_params=_params(("arbitrary",)), name="sample_gate")(page_table, cache_kt, qcol)


def _ffn_body(*refs, pre, final, job):
    it = iter(refs)
    if job is not None:
        pt_ref = next(it)
    x_ref = next(it)
    if pre:
        a_ref, b_ref, wa_ref, wb_ref = next(it), next(it), next(it), next(it)
    g_ref, wgu_ref, wd_ref = next(it), next(it), next(it)
    if final:
        fg_ref = next(it)
    if job is not None:
        ck_ref, q_ref = next(it), next(it)
    o_ref = next(it)
    if job is not None:
        idx_ref, kbuf, sem, gacc_ref = next(it), next(it), next(it), next(it)
        slot, c = _gate_fetch(pt_ref, ck_ref, kbuf, sem, **job)
        _gate_accumulate(kbuf, q_ref, gacc_ref, slot, c)
    d_ff = wd_ref.shape[1]
    x = x_ref[...]
    if pre:
        x = x + _dot(a_ref[...], wa_ref[...]) + _dot(b_ref[...], wb_ref[...])
    hb = _rms(x, g_ref[...]).astype(BF16)
    g = jnp.dot(hb, wgu_ref[0, :, :d_ff], preferred_element_type=F32)
    u = jnp.dot(hb, wgu_ref[0, :, d_ff:], preferred_element_type=F32)
    act = (_silu(g) * u).astype(BF16)
    y = x + 0.5 * jnp.dot(act, wd_ref[0], preferred_element_type=F32)
    if final:
        y = _rms(y, fg_ref[...])
    o_ref[...] = y
    if job is not None:
        _gate_finish(gacc_ref, idx_ref, c, job["n_chunks"])


def _ffn(x, w, *, tm, pre=None, final_g=None, gate=None):
    m, d = x.shape
    g, wgu, wd, layer = w
    steps = m // tm
    row = lambda c: pl.BlockSpec((tm, c), lambda i, *_: (i, 0))
    layer_spec = lambda a: pl.BlockSpec((1,) + a.shape[1:], lambda i, *_: (layer, 0, 0),
                                        pipeline_mode=pl.Buffered(1))
    args, specs = [x], [row(d)]
    if pre is not None:
        a, b, wa, wb = pre
        args += [a, b, wa, wb]
        specs += [row(a.shape[1]), row(b.shape[1]), _const_spec(wa.shape), _const_spec(wb.shape)]
    args += [g, wgu, wd]
    specs += [_const_spec(g.shape), layer_spec(wgu), layer_spec(wd)]
    if final_g is not None:
        args.append(final_g)
        specs.append(_const_spec(final_g.shape))
    out_shape = jax.ShapeDtypeStruct((m, d), F32)
    if gate is None:
        return pl.pallas_call(
            functools.partial(_ffn_body, pre=pre is not None, final=final_g is not None, job=None),
            grid=(steps,), in_specs=specs, out_specs=row(d), out_shape=out_shape,
            compiler_params=_params(("parallel",)), name="ffn")(*args)
    page_table, cache_kt, qcol, cache_layer, seq0 = gate
    n_chunks = page_table.shape[1] // GATE_CHUNK
    n_seq = steps // n_chunks
    assert n_seq * n_chunks == steps
    job = dict(layer=cache_layer, seq0=seq0, n_chunks=n_chunks, n_steps=steps)
    specs += [pl.BlockSpec(memory_space=pl.ANY),
              pl.BlockSpec((1,) + qcol.shape[1:], lambda i, pt: (seq0 + i // n_chunks, 0, 0, 0))]
    return pl.pallas_call(
        functools.partial(_ffn_body, pre=pre is not None, final=final_g is not None, job=job),
        grid_spec=pltpu.PrefetchScalarGridSpec(
            num_scalar_prefetch=1, grid=(steps,), in_specs=specs,
            out_specs=(row(d), pl.BlockSpec((1, MOBA_HEADS, LANES), lambda i, pt: (i // n_chunks, 0, 0))),
            scratch_shapes=_gate_scratch()),
        out_shape=(out_shape, jax.ShapeDtypeStruct((n_seq, MOBA_HEADS, LANES), jnp.int32)),
        compiler_params=_params(("arbitrary",)), name="ffn_gate")(page_table, *args, cache_kt, qcol)


def _even_prompt_body(x_ref, g_ref, wz_ref, wxbc_ref, wdt_ref, wga_ref, wgb_ref,
                      cw_ref, cb_ref, dtb_ref, alog_ref, dfull_ref, sn_ref, sel_ref,
                      cfw_ref, cfb_ref, cfg_ref, cfbeta_ref, woy_ref, woc_ref,
                      xo_ref, st_ref, xtail_ref, utail_ref,
                      xbuf, ubuf, s_ref, ybuf, *, nt):
    t = pl.program_id(1)
    T = TILE
    C = SSD_CHUNK

    @pl.when(t == 0)
    def _():
        xbuf[0:8, :] = jnp.zeros((8, SSD_CONV_DIM), F32)
        ubuf[0:32, :] = jnp.zeros((32, CF_CH), F32)
        s_ref[...] = jnp.zeros(s_ref.shape, F32)

    x = x_ref[0]
    hb = _rms(x, g_ref[...]).astype(BF16)
    z = jnp.dot(hb, wz_ref[...], preferred_element_type=F32)
    xbc = jnp.dot(hb, wxbc_ref[...], preferred_element_type=F32)
    dtr = jnp.dot(hb, wdt_ref[...], preferred_element_type=F32)
    ga = jnp.dot(hb, wga_ref[...], preferred_element_type=F32)
    gb = jnp.dot(hb, wgb_ref[...], preferred_element_type=F32)

    xbuf[8:8 + T, :] = xbc
    acc = cw_ref[0:1, :] * xbuf[5:5 + T, :]
    for k in range(1, SSD_CONV):
        acc = acc + cw_ref[k:k + 1, :] * xbuf[5 + k:5 + k + T, :]
    tail8 = xbuf[T:T + 8, :]
    xtail_ref[0] = tail8
    xbuf[0:8, :] = tail8
    xc = _silu(acc + cb_ref[...])
    xs = xc[:, 0:SSD_INNER]
    bm = xc[:, SSD_INNER:SSD_INNER + LANES]
    cm = xc[:, SSD_INNER + LANES:SSD_INNER + 2 * LANES]
    dt = _softplus(dtr + dtb_ref[...])
    dta = dt * (-jnp.exp(alog_ref[...]))
    sel = sel_ref[...]

    ri = lax.broadcasted_iota(jnp.int32, (C, C), 0)
    ci = lax.broadcasted_iota(jnp.int32, (C, C), 1)
    tri = ri >= ci
    trib = tri.astype(BF16)
    low = ci < SSD_HEAD_DIM

    for c in range(T // C):
        r0 = c * C
        xs_c, bm_c, cm_c = xs[r0:r0 + C], bm[r0:r0 + C], cm[r0:r0 + C]
        dt_c = dt[r0:r0 + C]
        cum = _dot3_r(trib, _split3(dta[r0:r0 + C]))
        cum_t = cum.T
        dt_t = dt_c.T
        cum_last = cum[C - 1:C, :]
        ecum_full = _dot3_l(jnp.exp(cum), sel)
        tail_full = _dot3_l(jnp.exp(cum_last - cum) * dt_c, sel)
        dlast_full = _dot3_l(jnp.broadcast_to(jnp.exp(cum_last), (8, LANES)), sel)[0:1]
        bm_t = bm_c.T
        cmb = cm_c.astype(BF16)
        for grp in range(SSD_GROUPS):
            bm_tg = jnp.where((ri // SSD_STATE) == grp, bm_t, 0.0).astype(BF16)
            gmat = jnp.dot(cmb, bm_tg, preferred_element_type=F32)
            pairs = SSD_HEADS // SSD_GROUPS // 2
            for pp in range(pairs):
                p = grp * pairs + pp
                lanes = slice(p * LANES, (p + 1) * LANES)
                xp = xs_c[:, lanes]
                xpb = xp.astype(BF16)
                ys = []
                for e in range(2):
                    h = 2 * p + e
                    seg = cum[:, h:h + 1] - cum_t[h:h + 1, :]
                    wm = gmat * jnp.exp(jnp.where(tri, seg, NEG)) * dt_t[h:h + 1, :]
                    ys.append(jnp.dot(wm.astype(BF16), xpb, preferred_element_type=F32))
                yp = jnp.where(low, ys[0], ys[1])
                sp = s_ref[p]
                yp = yp + jnp.dot(cmb, sp.astype(BF16), preferred_element_type=F32) * ecum_full[:, lanes]
                s_ref[p] = dlast_full[:, lanes] * sp + jnp.dot(
                    bm_tg, (xp * tail_full[:, lanes]).astype(BF16), preferred_element_type=F32)
                ybuf[r0:r0 + C, lanes] = yp

    @pl.when(t == nt - 1)
    def _():
        for p in range(SSD_HEADS // 2):
            st_ref[0, p] = s_ref[p].T

    y = (ybuf[...] + dfull_ref[...] * xs) * _silu(z)
    gw = SSD_INNER // SSD_GROUPS
    yn = []
    for grp in range(SSD_GROUPS):
        yg = y[:, grp * gw:(grp + 1) * gw]
        yn.append(_rms(yg, sn_ref[:, grp * gw:(grp + 1) * gw]).astype(BF16))

    u = ga * _sigmoid(gb)
    ubuf[32:32 + T, :] = u
    base = 32 - (CF_WIDTH - 1)
    cacc = None
    for r in range(8):
        offs = [o for o in range(base, base + CF_WIDTH) if o % 8 == r]
        if not offs:
            continue
        ur = ubuf[r:max(offs) + T, :]
        part = None
        for o in offs:
            term = cfw_ref[o - base:o - base + 1, :] * ur[o - r:o - r + T]
            part = term if part is None else part + term
        cacc = part if cacc is None else cacc + part
    tail32 = ubuf[T:T + 32, :]
    utail_ref[0] = tail32
    ubuf[0:32, :] = tail32
    c32 = cacc + cfb_ref[...]
    mu = jnp.mean(c32, axis=-1, keepdims=True)
    var = jnp.mean(jnp.square(c32 - mu), axis=-1, keepdims=True)
    c32 = _silu((c32 - mu) * lax.rsqrt(var + NORM_EPS) * cfg_ref[...] + cfbeta_ref[...])

    out = jnp.dot(c32.astype(BF16), woc_ref[...], preferred_element_type=F32)
    for grp in range(SSD_GROUPS):
        out = out + jnp.dot(yn[grp], woy_ref[grp * gw:(grp + 1) * gw, :], preferred_element_type=F32)
    xo_ref[0] = x + out


def _even_prompt(x, w):
    b, l, d = x.shape
    nt = l // TILE
    consts = [w[k] for k in ("g", "wz", "wxbc", "wdt", "wga", "wgb", "cw", "cb", "dtb", "alog", "dfull", "sn",
                             "sel", "cfw", "cfb", "cfg", "cfbeta", "woy", "woc")]
    out_shape = (jax.ShapeDtypeStruct((b, l, d), F32),
                 jax.ShapeDtypeStruct((b, SSD_HEADS // 2, LANES, LANES), F32),
                 jax.ShapeDtypeStruct((b, 8, SSD_CONV_DIM), F32),
                 jax.ShapeDtypeStruct((b, 32, CF_CH), F32))
    out_specs = (pl.BlockSpec((1, TILE, d), lambda i, t: (i, t, 0)),
                 pl.BlockSpec((1, SSD_HEADS // 2, LANES, LANES), lambda i, t: (i, 0, 0, 0)),
                 pl.BlockSpec((1, 8, SSD_CONV_DIM), lambda i, t: (i, 0, 0)),
                 pl.BlockSpec((1, 32, CF_CH), lambda i, t: (i, 0, 0)))
    xo, st, xtail, utail = pl.pallas_call(
        functools.partial(_even_prompt_body, nt=nt),
        grid=(b, nt),
        in_specs=[pl.BlockSpec((1, TILE, d), lambda i, t: (i, t, 0))] + [_const_spec(c.shape) for c in consts],
        out_specs=out_specs, out_shape=out_shape,
        scratch_shapes=[pltpu.VMEM((TILE + 8, SSD_CONV_DIM), F32), pltpu.VMEM((TILE + 32, CF_CH), F32),
                        pltpu.VMEM((SSD_HEADS // 2, LANES, LANES), F32), pltpu.VMEM((TILE, SSD_INNER), F32)],
        compiler_params=_params(("arbitrary", "arbitrary")), name="even_prompt")(x, *consts)
    half = SSD_HEADS // 2 // SSD_GROUPS
    st = st.reshape(b, SSD_HEADS // 2, 2, SSD_HEAD_DIM, SSD_GROUPS, SSD_STATE)
    state = jnp.concatenate([st[:, grp * half:(grp + 1) * half, :, :, grp, :] for grp in range(SSD_GROUPS)], axis=1)
    state = state.reshape(b, SSD_HEADS, SSD_HEAD_DIM, SSD_STATE)
    return xo, state, xtail[:, 8 - (SSD_CONV - 1):], utail[:, 32 - (CF_WIDTH - 1):]


def _hg_levels(T):
    t = np.arange(T)
    le = (t[None, :] <= t[:, None]).astype(np.float32)
    ds, ms = [], []
    m = 1
    while m < T:
        rb = (t // (2 * m)) * 2 * m + m - 1
        if m < 8:
            ds.append(le - (t[None, :] <= rb[:, None]).astype(np.float32))
        same = (t[:, None] // (2 * m)) == (t[None, :] // (2 * m))
        ms.append((same & ((t[:, None] % (2 * m)) >= m) & ((t[None, :] % (2 * m)) < m)).astype(np.float32))
        m *= 2
    return (jnp.asarray(le, BF16), jnp.asarray(np.stack(ds), BF16), jnp.asarray(np.stack(ms), BF16))


def _rope_tables(pos):
    half = ROPE_DIM // 2
    inv = ROPE_THETA ** (-np.arange(half, dtype=np.float64) / half)
    ang = pos.astype(np.float64)[:, None] * inv[None, :]
    cos, sin = np.cos(ang), np.sin(ang)
    n = pos.shape[0]
    one = np.ones((n, MOBA_HEAD_DIM - ROPE_DIM))
    zero = np.zeros((n, MOBA_HEAD_DIM - ROPE_DIM))
    zh = np.zeros((n, half))
    c = np.concatenate([cos, cos, one], axis=1)
    s1 = np.concatenate([-sin, zh, zero], axis=1)
    s2 = np.concatenate([zh, sin, zero], axis=1)
    return tuple(jnp.asarray(np.concatenate([a, a], axis=1), F32) for a in (c, s1, s2))


def _rope(x, c, s1, s2):
    outs = []
    for j in range(x.shape[1] // LANES):
        blk = x[:, j * LANES:(j + 1) * LANES]
        outs.append(blk * c + pltpu.roll(blk, LANES - ROPE_DIM // 2, 1) * s1 + pltpu.roll(blk, ROPE_DIM // 2, 1) * s2)
    return outs


def _hg_lower_bound(lb_ref, layer):
    a = lb_ref[...]
    mx = jnp.max(a, axis=0, keepdims=True)
    e = jnp.exp(a - mx)
    return jnp.sum(e[1:layer + 1], axis=0, keepdims=True) / jnp.sum(e, axis=0, keepdims=True)


def _odd_prompt_body(x_ref, g_ref, win_ref, c_ref, s1_ref, s2_ref, lb_ref, gn_ref, tril_ref, dm_ref, mm_ref,
                     q_ref, kb_ref, vt_ref, km_ref, ko_ref, vo_ref, o_ref, hs_ref,
                     st_ref, *, nt, layer):
    t = pl.program_id(1)
    T = TILE
    W = MOBA_WIDTH

    @pl.when(t == 0)
    def _():
        st_ref[...] = jnp.zeros(st_ref.shape, F32)

    x = x_ref[0]
    hb = _rms(x, g_ref[...]).astype(BF16)
    proj = jnp.dot(hb, win_ref[...], preferred_element_type=F32)

    c, s1, s2 = c_ref[...], s1_ref[...], s2_ref[...]
    qb = _rope(proj[:, 0:W], c, s1, s2)
    kb = _rope(proj[:, W:2 * W], c, s1, s2)
    v = proj[:, 2 * W:3 * W]
    for j in range(W // LANES):
        lanes = slice(j * LANES, (j + 1) * LANES)
        q_ref[0, :, lanes] = qb[j]
        kb_ref[0, :, lanes] = kb[j].astype(BF16)
        km_ref[0, 0, :, lanes] = jnp.mean(kb[j], axis=0, keepdims=True)
        kt = kb[j].T
        vt = v[:, lanes].T
        for e in range(2):
            ko_ref[0, 2 * j + e] = kt[e * 64:(e + 1) * 64, :]
            vo_ref[0, 2 * j + e] = vt[e * 64:(e + 1) * 64, :]
            r0 = (2 * j + e) * VT_ROWS
            vt_ref[0, r0:r0 + 64, :] = vt[e * 64:(e + 1) * 64, :].astype(BF16)
            vt_ref[0, r0 + 64:r0 + VT_ROWS, :] = (
                lax.broadcasted_iota(jnp.int32, (VT_ROWS - 64, T), 0) == 0).astype(BF16)

    lb = _hg_lower_bound(lb_ref, layer)
    hq = _silu(proj[:, 3 * W:3 * W + HG_WIDTH])
    f = lb + (1.0 - lb) * _sigmoid(proj[:, 3 * W + HG_WIDTH:3 * W + 2 * HG_WIDTH])
    hv = proj[:, 3 * W + 2 * HG_WIDTH:3 * W + 3 * HG_WIDTH]
    hgate = proj[:, 3 * W + 3 * HG_WIDTH:3 * W + 4 * HG_WIDTH]
    kk = 1.0 - f
    lf3 = _split3(jnp.log(f))
    cum = _dot3_r(tril_ref[...], lf3)
    nlev = mm_ref.shape[0]
    ri = lax.broadcasted_iota(jnp.int32, (T, T), 0)
    ci = lax.broadcasted_iota(jnp.int32, (T, T), 1)
    eye = ri == ci
    amat = []
    for h in range(HG_HEADS):
        lanes = slice(h * HG_K, (h + 1) * HG_K)
        diag = jnp.sum(hq[:, lanes] * kk[:, lanes], axis=1, keepdims=True)
        amat.append(jnp.where(eye, diag, 0.0))
    for lev in range(nlev):
        m = 1 << lev
        if lev < dm_ref.shape[0]:
            d = _dot3_r(dm_ref[lev], lf3)
        else:
            refs = [jnp.broadcast_to(cum[r0 + m - 1:r0 + m, :], (2 * m, cum.shape[1])) for r0 in range(0, T, 2 * m)]
            d = cum - (refs[0] if len(refs) == 1 else jnp.concatenate(refs, axis=0))
        e_all = jnp.exp(-jnp.abs(d))
        mk = mm_ref[lev].astype(F32)
        for h in range(HG_HEADS):
            lanes = slice(h * HG_K, (h + 1) * HG_K)
            a_m = _dot_nt(hq[:, lanes] * e_all[:, lanes], kk[:, lanes] * e_all[:, lanes])
            amat[h] = amat[h] + a_m * mk
    cum_last = cum[T - 1:T, :]
    ecum = jnp.exp(cum)
    ktail = kk * jnp.exp(cum_last - cum)
    elast = jnp.exp(cum_last)
    for h in range(HG_HEADS):
        lanes = slice(h * HG_K, (h + 1) * HG_K)
        vh = hv[:, lanes]
        vhb = vh.astype(BF16)
        st = st_ref[h]
        o = jnp.dot(amat[h].astype(BF16), vhb, preferred_element_type=F32)
        o = o + _dot_nt(hq[:, lanes] * ecum[:, lanes], st)
        st_new = elast[:, lanes] * st + jnp.dot(vh.T.astype(BF16), ktail[:, lanes].astype(BF16),
                                                preferred_element_type=F32)
        st_ref[h] = st_new
        o_ref[0, :, lanes] = _rms(o, gn_ref[:, lanes]) * _silu(hgate[:, lanes])

    @pl.when(t == nt - 1)
    def _():
        for h in range(HG_HEADS):
            hs_ref[0, h] = st_ref[h].T


def _odd_prompt_proj(x, w, layer):
    b, l, d = x.shape
    nt = l // TILE
    c, s1, s2 = _rope_tables(np.arange(l))
    tril, dm, mm = _hg_levels(TILE)
    consts_a = [w["g"], w["win"]]
    consts_b = [w["lb"], w["gn"], tril, dm, mm]
    tab = pl.BlockSpec((TILE, LANES), lambda i, t: (t, 0))
    W = MOBA_WIDTH
    out_shape = (jax.ShapeDtypeStruct((b, l, W), F32),
                 jax.ShapeDtypeStruct((b, l, W), BF16),
                 jax.ShapeDtypeStruct((b, MOBA_HEADS * VT_ROWS, l), BF16),
                 jax.ShapeDtypeStruct((b, nt, 1, W), F32),
                 jax.ShapeDtypeStruct((b, MOBA_HEADS, MOBA_HEAD_DIM, l), F32),
                 jax.ShapeDtypeStruct((b, MOBA_HEADS, MOBA_HEAD_DIM, l), F32),
                 jax.ShapeDtypeStruct((b, l, HG_WIDTH), F32),
                 jax.ShapeDtypeStruct((b, HG_HEADS, HG_K, HG_V), F32))
    out_specs = (pl.BlockSpec((1, TILE, W), lambda i, t: (i, t, 0)),
                 pl.BlockSpec((1, TILE, W), lambda i, t: (i, t, 0)),
                 pl.BlockSpec((1, MOBA_HEADS * VT_ROWS, TILE), lambda i, t: (i, 0, t)),
                 pl.BlockSpec((1, 1, 1, W), lambda i, t: (i, t, 0, 0)),
                 pl.BlockSpec((1, MOBA_HEADS, MOBA_HEAD_DIM, TILE), lambda i, t: (i, 0, 0, t)),
                 pl.BlockSpec((1, MOBA_HEADS, MOBA_HEAD_DIM, TILE), lambda i, t: (i, 0, 0, t)),
                 pl.BlockSpec((1, TILE, HG_WIDTH), lambda i, t: (i, t, 0)),
                 pl.BlockSpec((1, HG_HEADS, HG_K, HG_V), lambda i, t: (i, 0, 0, 0)))
    return pl.pallas_call(
        functools.partial(_odd_prompt_body, nt=nt, layer=layer),
        grid=(b, nt),
        in_specs=([pl.BlockSpec((1, TILE, d), lambda i, t: (i, t, 0))] + [_const_spec(a.shape) for a in consts_a]
                  + [tab, tab, tab] + [_const_spec(a.shape) for a in consts_b]),
        out_specs=out_specs, out_shape=out_shape,
        scratch_shapes=[pltpu.VMEM((HG_HEADS, HG_V, HG_K), F32)],
        compiler_params=_params(("arbitrary", "arbitrary")), name="odd_prompt_proj")(
            x, *consts_a, c, s1, s2, *consts_b)


def _moba_prompt_body(q_ref, k_ref, vt_ref, km_ref, o_ref, sel_ref, s_ref, p_ref):
    i = pl.program_id(2)
    T = TILE
    hd = MOBA_HEAD_DIM
    q = q_ref[0]
    km = km_ref[0, 0]
    nbp = km.shape[0]
    lane = lax.broadcasted_iota(jnp.int32, (T, LANES), 1)
    qes = [jnp.where((lane // hd) == e, q, 0.0) for e in range(2)]
    qs = [(qe * (hd ** -0.5 * math.log2(math.e))).astype(BF16) for qe in qes]
    n_trips = (i + 1) // 2
    last_blk = k_ref.shape[1] // T - 1

    def v_ext(e, off, width):
        return vt_ref[0, e * VT_ROWS:(e + 1) * VT_ROWS, pl.ds(off, width)]

    def scores(b, slot):
        off = pl.multiple_of(jnp.minimum(b, last_blk) * T, T)
        kk = k_ref[0, pl.ds(off, T), :]
        for e in range(2):
            s_ref[slot, e] = _dot_nt(kk, qs[e])

    scores(i, 1)
    scores(0, 0)
    p_ref[...] = jnp.zeros(p_ref.shape, BF16)

    blk = lax.broadcasted_iota(jnp.int32, (nbp, T), 0)
    elig = blk < i
    kh = km.astype(BF16)
    kl = (km - kh.astype(F32)).astype(BF16)
    for e in range(2):
        qh = qes[e].astype(BF16)
        ql = (qes[e] - qh.astype(F32)).astype(BF16)
        gate = _dot_nt(kh, qh) + _dot_nt(kh, ql) + _dot_nt(kl, qh)
        gcur = jnp.where(elig, gate, -jnp.inf)
        selm = jnp.zeros((nbp, T), F32)
        for _ in range(MOBA_TOPK):
            mx = jnp.max(gcur, axis=0, keepdims=True)
            first = jnp.min(jnp.where(gcur == mx, blk, nbp), axis=0, keepdims=True)
            hit = blk == first
            selm = jnp.where(hit & elig, 1.0, selm)
            gcur = jnp.where(hit, -jnp.inf, gcur)
        sel_ref[e] = selm

    ri = lax.broadcasted_iota(jnp.int32, (T, T), 0)
    ci = lax.broadcasted_iota(jnp.int32, (T, T), 1)
    own = pl.multiple_of(i * T, T)
    state = []
    for e in range(2):
        s = jnp.where(ri <= ci, s_ref[1, e], NEG)
        m0 = jnp.max(s, axis=0, keepdims=True)
        p0 = jnp.exp2((s - m0).astype(BF16))
        state += [m0, jnp.dot(v_ext(e, own, T), p0, preferred_element_type=F32)]


    def pv(b, slot, e):
        off = pl.multiple_of(jnp.clip(b, 0, last_blk) * T, T)
        return jnp.dot(v_ext(e, off, T), p_ref[slot, e], preferred_element_type=F32)

    def softmax_step(b, slot, e, m, acc, prev):
        s = jnp.where(sel_ref[e, pl.ds(b, 1), :] > 0.5, s_ref[slot, e], NEG)
        m_new = jnp.maximum(m, jnp.max(s, axis=0, keepdims=True))
        p_ref[slot, e] = jnp.exp2((s - m_new).astype(BF16))
        return m_new, jnp.exp2(m - m_new) * (acc + prev)

    def body(u, carry):
        carry = list(carry)
        scores(2 * u + 1, 1)
        for e in range(2):
            prev = pv(2 * u - 1, 1, e)
            carry[2 * e], carry[2 * e + 1] = softmax_step(2 * u, 0, e, carry[2 * e], carry[2 * e + 1], prev)
        scores(2 * u + 2, 0)
        for e in range(2):
            prev = pv(2 * u, 0, e)
            carry[2 * e], carry[2 * e + 1] = softmax_step(2 * u + 1, 1, e, carry[2 * e], carry[2 * e + 1], prev)
        return tuple(carry)

    fin = lax.fori_loop(0, n_trips, body, tuple(state))
    outs = []
    for e in range(2):
        tot = fin[2 * e + 1] + pv(2 * n_trips - 1, 1, e)
        outs.append(tot[0:hd] / tot[hd:hd + 1])
    o_ref[0] = jnp.concatenate(outs, axis=0).T


def _moba_prompt(q, kb, vt, kmean):
    b, l, w = q.shape
    nq = l // TILE
    npair = w // LANES
    nb = kmean.shape[1]
    km = kmean.reshape(b, nb, npair, LANES).transpose(0, 2, 1, 3)
    nbp = -(-nb // 8) * 8
    km = jnp.pad(km, ((0, 0), (0, 0), (0, nbp - nb), (0, 0)))
    return pl.pallas_call(
        _moba_prompt_body,
        grid=(b, npair, nq),
        in_specs=[pl.BlockSpec((1, TILE, LANES), lambda i, p, t: (i, t, p)),
                  pl.BlockSpec((1, l, LANES), lambda i, p, t: (i, 0, p)),
                  pl.BlockSpec((1, 2 * VT_ROWS, l), lambda i, p, t: (i, p, 0)),
                  pl.BlockSpec((1, 1, nbp, LANES), lambda i, p, t: (i, p, 0, 0))],
        out_specs=pl.BlockSpec((1, TILE, LANES), lambda i, p, t: (i, t, p)),
        out_shape=jax.ShapeDtypeStruct((b, l, w), F32),
        scratch_shapes=[pltpu.VMEM((2, nbp, TILE), F32), pltpu.VMEM((2, 2, TILE, TILE), F32),
                        pltpu.VMEM((2, 2, TILE, TILE), BF16)],
        compiler_params=_params(("arbitrary", "arbitrary", "arbitrary")), name="moba_prompt")(q, kb, vt, km)


def _even_sample_proj_body(x_ref, g_ref, wz_ref, wxbc_ref, wdt_ref, wga_ref, wgb_ref, cw_ref, cb_ref, dtb_ref,
                           xbuf_ref, cfw_ref, cfb_ref, cfg_ref, cfbeta_ref, ubuf_ref,
                           z_ref, xc_ref, dt_ref, xnew_ref, c_ref, unew_ref):
    x = x_ref[...]
    hb = _rms(x, g_ref[...]).astype(BF16)
    z_ref[...] = jnp.dot(hb, wz_ref[...], preferred_element_type=F32)
    xbc = jnp.dot(hb, wxbc_ref[...], preferred_element_type=F32)
    dtr = jnp.dot(hb, wdt_ref[...], preferred_element_type=F32)
    ga = jnp.dot(hb, wga_ref[...], preferred_element_type=F32)
    gb = jnp.dot(hb, wgb_ref[...], preferred_element_type=F32)
    k1 = SSD_CONV - 1
    acc = cw_ref[k1:k1 + 1, :] * xbc
    for k in range(k1):
        acc = acc + cw_ref[k:k + 1, :] * xbuf_ref[k]
        if k > 0:
            xnew_ref[k - 1] = xbuf_ref[k]
    xnew_ref[k1 - 1] = xbc
    xc_ref[...] = _silu(acc + cb_ref[...])
    dt_ref[...] = _softplus(dtr + dtb_ref[...])
    u = ga * _sigmoid(gb)
    k2 = CF_WIDTH - 1
    cacc = cfw_ref[k2:k2 + 1, :] * u
    for k in range(k2):
        cacc = cacc + cfw_ref[k:k + 1, :] * ubuf_ref[k]
        if k > 0:
            unew_ref[k - 1] = ubuf_ref[k]
    unew_ref[k2 - 1] = u
    c32 = cacc + cfb_ref[...]
    mu = jnp.mean(c32, axis=-1, keepdims=True)
    var = jnp.mean(jnp.square(c32 - mu), axis=-1, keepdims=True)
    c_ref[...] = _silu((c32 - mu) * lax.rsqrt(var + NORM_EPS) * cfg_ref[...] + cfbeta_ref[...])


STEP_ROWS = 128
HG_STEP_ROWS = 32


def _ssd_step_body(s_ref, xb_ref, dt_ref, alog_ref, b_ref, c_ref, so_ref, yt_ref):
    rows = s_ref.shape[0]
    dt = dt_ref[...]
    decay = jnp.exp(dt * (-jnp.exp(alog_ref[...])))
    bdt = b_ref[...] * dt
    cc = c_ref[...]
    for r in range(rows):
        s_new = decay[r:r + 1, :] * s_ref[r] + xb_ref[r] * bdt[r:r + 1, :]
        so_ref[r] = s_new
        yt_ref[:, r:r + 1] = jnp.sum(s_new * cc[r:r + 1, :], axis=1, keepdims=True)


def _even_sample_out_body(x_ref, y_ref, xs_ref, z_ref, dfull_ref, sn_ref, c_ref, woy_ref, woc_ref, o_ref):
    y = (y_ref[...] + dfull_ref[...] * xs_ref[...]) * _silu(z_ref[...])
    gw = SSD_INNER // SSD_GROUPS
    out = _dot(c_ref[...], woc_ref[...])
    for grp in range(SSD_GROUPS):
        lanes = slice(grp * gw, (grp + 1) * gw)
        out = out + _dot(_rms(y[:, lanes], sn_ref[:, lanes]), woy_ref[lanes, :])
    o_ref[...] = x_ref[...] + out


def _call(body, out_shape, *args, name):
    return pl.pallas_call(body, out_shape=out_shape, compiler_params=_params(None), name=name)(*args)


def _even_sample(x, w, s_ssd, buf_ssd, buf_cf):
    nb = x.shape[0]
    f = lambda *s: jax.ShapeDtypeStruct(s, F32)
    z, xc, dt, xnew, c, unew = _call(
        _even_sample_proj_body,
        (f(nb, SSD_INNER), f(nb, SSD_CONV_DIM), f(nb, LANES), f(SSD_CONV - 1, nb, SSD_CONV_DIM), f(nb, CF_CH),
         f(CF_WIDTH - 1, nb, CF_CH)),
        x, w["g"], w["wz"], w["wxbc"], w["wdt"], w["wga"], w["wgb"], w["cw"], w["cb"], w["dtb"],
        jnp.swapaxes(buf_ssd, 0, 1), w["cfw"], w["cfb"], w["cfg"], w["cfbeta"], jnp.swapaxes(buf_cf, 0, 1),
        name="even_sample_proj")
    xs = xc[:, :SSD_INNER]
    rows = nb * SSD_HEADS
    rep_heads = SSD_HEADS // SSD_GROUPS
    grp = lambda a: jnp.repeat(a.reshape(nb, SSD_GROUPS, SSD_STATE), rep_heads, axis=1).reshape(rows, SSD_STATE)
    bm = grp(xc[:, SSD_INNER:SSD_INNER + SSD_GROUPS * SSD_STATE])
    cm = grp(xc[:, SSD_INNER + SSD_GROUPS * SSD_STATE:])
    rb = STEP_ROWS
    assert rows % rb == 0
    rowspec = pl.BlockSpec((rb, SSD_STATE), lambda i: (i, 0))
    colspec = pl.BlockSpec((SSD_HEAD_DIM, rb), lambda i: (0, i))
    sspec = pl.BlockSpec((rb, SSD_HEAD_DIM, SSD_STATE), lambda i: (i, 0, 0))
    per_row = lambda a: jnp.broadcast_to(a.reshape(rows, 1), (rows, SSD_STATE))
    s_new, yt = pl.pallas_call(
        _ssd_step_body, grid=(rows // rb,),
        in_specs=[sspec, sspec, rowspec, rowspec, rowspec, rowspec],
        out_specs=(sspec, colspec),
        out_shape=(f(rows, SSD_HEAD_DIM, SSD_STATE), f(SSD_HEAD_DIM, rows)),
        compiler_params=_params(("parallel",)), name="ssd_step")(
            s_ssd.reshape(rows, SSD_HEAD_DIM, SSD_STATE),
            jnp.broadcast_to(xs.reshape(rows, SSD_HEAD_DIM, 1), (rows, SSD_HEAD_DIM, SSD_STATE)),
            per_row(dt[:, :SSD_HEADS]), per_row(jnp.tile(w["alog"][0, :SSD_HEADS], nb)), bm, cm)
    xo = _call(_even_sample_out_body, f(nb, x.shape[1]),
               x, yt.T.reshape(nb, SSD_INNER), xs, z, w["dfull"], w["sn"], c, w["woy"], w["woc"],
               name="even_sample_out")
    return (xo, s_new.reshape(s_ssd.shape), jnp.swapaxes(xnew, 0, 1), jnp.swapaxes(unew, 0, 1))


def _odd_sample_proj_body(x_ref, g_ref, win_ref, c_ref, s1_ref, s2_ref, lb_ref,
                          q_ref, k_ref, v_ref, hq_ref, f_ref, hv_ref, hg_ref, *, layer):
    W = MOBA_WIDTH
    hb = _rms(x_ref[...], g_ref[...]).astype(BF16)
    proj = jnp.dot(hb, win_ref[...], preferred_element_type=F32)
    c, s1, s2 = c_ref[...], s1_ref[...], s2_ref[...]
    qb = _rope(proj[:, 0:W], c, s1, s2)
    kb = _rope(proj[:, W:2 * W], c, s1, s2)
    for j in range(W // LANES):
        q_ref[:, j * LANES:(j + 1) * LANES] = qb[j]
        k_ref[:, j * LANES:(j + 1) * LANES] = kb[j]
    v_ref[...] = proj[:, 2 * W:3 * W]
    lb = _hg_lower_bound(lb_ref, layer)
    hq_ref[...] = _silu(proj[:, 3 * W:3 * W + HG_WIDTH])
    f_ref[...] = lb + (1.0 - lb) * _sigmoid(proj[:, 3 * W + HG_WIDTH:3 * W + 2 * HG_WIDTH])
    hv_ref[...] = proj[:, 3 * W + 2 * HG_WIDTH:3 * W + 3 * HG_WIDTH]
    hg_ref[...] = proj[:, 3 * W + 3 * HG_WIDTH:3 * W + 4 * HG_WIDTH]


def _hg_step_body(s_ref, qt_ref, ft_ref, v_ref, so_ref, o_ref):
    qt, ft, vv = qt_ref[0], ft_ref[0], v_ref[...]
    for r in range(s_ref.shape[0]):
        fc = ft[:, r:r + 1]
        s_new = fc * s_ref[r] + (1.0 - fc) * vv[r:r + 1, :]
        so_ref[r] = s_new
        o_ref[r:r + 1, :] = jnp.sum(s_new * qt[:, r:r + 1], axis=0, keepdims=True)


def _attn_copies(idx_ref, pt_ref, ck_ref, cv_ref, kbuf, vbuf, sem, layer, seq, slot):
    ppb = MOBA_BLOCK // PAGE_SIZE
    cps = []
    for h in range(MOBA_HEADS):
        for r in range(MOBA_TOPK):
            blk = idx_ref[seq, h * MOBA_TOPK + r]
            for pg in range(ppb):
                phys = pt_ref[seq, blk * ppb + pg]
                j = r * ppb + pg
                cps.append(pltpu.make_async_copy(ck_ref.at[layer, phys, h], kbuf.at[slot, h, j], sem.at[0, slot]))
                cps.append(pltpu.make_async_copy(cv_ref.at[layer, phys, h], vbuf.at[slot, h, j], sem.at[1, slot]))
    return cps


def _sample_attn_body(idx_ref, pt_ref, ck_ref, cv_ref, q_ref, kn_ref, vn_ref, o_ref, kbuf, vbuf, sem, *,
                      layer, n_seq):
    b = pl.program_id(0)
    slot = b % 2
    scale = MOBA_HEAD_DIM ** -0.5
    nsel = MOBA_TOPK * (MOBA_BLOCK // PAGE_SIZE)

    @pl.when(b == 0)
    def _():
        for cp in _attn_copies(idx_ref, pt_ref, ck_ref, cv_ref, kbuf, vbuf, sem, layer, b, slot):
            cp.start()

    @pl.when(b + 1 < n_seq)
    def _():
        for cp in _attn_copies(idx_ref, pt_ref, ck_ref, cv_ref, kbuf, vbuf, sem, layer, b + 1, 1 - slot):
            cp.start()

    for cp in _attn_copies(idx_ref, pt_ref, ck_ref, cv_ref, kbuf, vbuf, sem, layer, b, slot):
        cp.wait()

    for h in range(MOBA_HEADS):
        qc = q_ref[0, h]
        s_self = jnp.sum(qc * kn_ref[0, h], axis=0, keepdims=True) * scale
        ss = [jnp.sum(kbuf[slot, h, j] * qc, axis=0, keepdims=True) * scale for j in range(nsel)]
        m = s_self
        for s in ss:
            m = jnp.maximum(m, jnp.max(s, axis=1, keepdims=True))
        p_self = jnp.exp(s_self - m)
        l = p_self
        acc = jnp.zeros((MOBA_HEAD_DIM, LANES), F32)
        for j in range(nsel):
            p = jnp.exp(ss[j] - m)
            l = l + jnp.sum(p, axis=1, keepdims=True)
            acc = acc + vbuf[slot, h, j] * p
        o_ref[0, h] = (vn_ref[0, h] * p_self + jnp.sum(acc, axis=1, keepdims=True)) / l


def _odd_sample_out_body(x_ref, att_ref, o_ref, hg_ref, gn_ref, woa_ref, woh_ref, xo_ref):
    out = _dot(att_ref[...], woa_ref[...])
    o = o_ref[...]
    gated = []
    for h in range(HG_HEADS):
        lanes = slice(h * HG_V, (h + 1) * HG_V)
        gated.append(_rms(o[:, lanes], gn_ref[:, lanes]) * _silu(hg_ref[:, lanes]))
    out = out + _dot(jnp.concatenate(gated, axis=1), woh_ref[...])
    xo_ref[...] = x_ref[...] + out


def _odd_sample_pre(x, w, layer, s_hg, page_table, past_len):
    nb = x.shape[0]
    f = lambda *s: jax.ShapeDtypeStruct(s, F32)
    n_pages = page_table.shape[1]
    ppb = MOBA_BLOCK // PAGE_SIZE
    nblk = n_pages // ppb
    assert n_pages % ppb == 0 and nblk >= MOBA_TOPK and nblk < LANES
    assert n_pages % GATE_CHUNK == 0 and GATE_CHUNK % ppb == 0
    c, s1, s2 = _rope_tables(np.full((1,), past_len))
    q, k, v, hq, fg, hv, hg = _call(
        functools.partial(_odd_sample_proj_body, layer=layer),
        tuple(f(nb, MOBA_WIDTH) for _ in range(7)),
        x, w["g"], w["win"], c, s1, s2, w["lb"], name="odd_sample_proj")

    rows = nb * HG_HEADS
    rb = HG_STEP_ROWS
    assert rows % rb == 0
    cols = lambda a: _pad_lanes(jnp.swapaxes(a.reshape(rows // rb, rb, HG_K), 1, 2))
    sspec = pl.BlockSpec((rb, HG_K, HG_V), lambda i: (i, 0, 0))
    cspec = pl.BlockSpec((1, HG_K, LANES), lambda i: (i, 0, 0))
    rspec = pl.BlockSpec((rb, HG_V), lambda i: (i, 0))
    s_new, o = pl.pallas_call(
        _hg_step_body, grid=(rows // rb,),
        in_specs=[sspec, cspec, cspec, rspec], out_specs=(sspec, rspec),
        out_shape=(f(rows, HG_K, HG_V), f(rows, HG_V)),
        compiler_params=_params(("parallel",)), name="hg_step")(
            s_hg.reshape(rows, HG_K, HG_V), cols(hq), cols(fg), hv.reshape(rows, HG_V))

    hd = MOBA_HEAD_DIM
    col = lambda a: jnp.broadcast_to(a.reshape(nb, MOBA_HEADS, hd, 1), (nb, MOBA_HEADS, hd, LANES))
    return dict(qcol=col(q), kcol=col(k), vcol=col(v), kn=k.reshape(nb, MOBA_HEADS, 1, hd),
                vn=v.reshape(nb, MOBA_HEADS, 1, hd), o=o.reshape(nb, HG_WIDTH), hg=hg,
                state=s_new.reshape(s_hg.shape))


def _odd_sample_post(x, w, pre, idx, cache_kt, cache_vt, cache_layer, page_table):
    nb = x.shape[0]
    f = lambda *s: jax.ShapeDtypeStruct(s, F32)
    hd = MOBA_HEAD_DIM
    nsel = MOBA_TOPK * (MOBA_BLOCK // PAGE_SIZE)
    colspec = lambda nidx: pl.BlockSpec((1, MOBA_HEADS, hd, LANES), lambda i, *_: (i, 0, 0, 0))
    att = pl.pallas_call(
        functools.partial(_sample_attn_body, layer=cache_layer, n_seq=nb),
        grid_spec=pltpu.PrefetchScalarGridSpec(
            num_scalar_prefetch=2, grid=(nb,),
            in_specs=[pl.BlockSpec(memory_space=pl.ANY), pl.BlockSpec(memory_space=pl.ANY),
                      colspec(2), colspec(2), colspec(2)],
            out_specs=colspec(2),
            scratch_shapes=[pltpu.VMEM((2, MOBA_HEADS, nsel, hd, PAGE_SIZE), F32),
                            pltpu.VMEM((2, MOBA_HEADS, nsel, hd, PAGE_SIZE), F32),
                            pltpu.SemaphoreType.DMA((2, 2))]),
        out_shape=f(nb, MOBA_HEADS, hd, LANES),
        compiler_params=_params(("arbitrary",)), name="sample_attn")(
            idx[:, :, :MOBA_TOPK].reshape(nb, MOBA_HEADS * MOBA_TOPK), page_table, cache_kt, cache_vt,
            pre["qcol"], pre["kcol"], pre["vcol"])
    att = att[:, :, :, 0].reshape(nb, MOBA_WIDTH)
    return _call(_odd_sample_out_body, f(nb, x.shape[1]),
                 x, att, pre["o"], pre["hg"], w["gn"], w["woa"], w["woh"], name="odd_sample_out")


def _row(a):
    return a.reshape(1, -1).astype(F32)


def _pad_lanes(a, n=LANES):
    return jnp.pad(a, [(0, 0)] * (a.ndim - 1) + [(0, n - a.shape[-1])])


def kernel(x_prompt, x_sample, state_ssd, state_ssd_conv, state_cf_conv, cache_k, cache_v, page_table, state_hg,
           ffn1_norm, ffn1_w_gu, ffn1_w_down, mix_norm, ffn2_norm, ffn2_w_gu, ffn2_w_down, final_norm,
           even_w_in, ssd_conv_w, ssd_conv_b, ssd_dt_bias, ssd_a_log, ssd_d, ssd_norm,
           cf_dw_w, cf_dw_b, cf_ln_g, cf_ln_b, even_w_out,
           odd_w_in, hg_lower_bound, hg_norm, odd_w_out):
    depth = ffn1_norm.shape[0]
    bp, lp, d = x_prompt.shape
    nb = x_sample.shape[0]
    assert x_sample.shape[1] == 1 and lp % TILE == 0 and lp // TILE <= LANES
    past_len = page_table.shape[1] * PAGE_SIZE

    ffn_bf16 = [a.astype(BF16) for a in (ffn1_w_gu, ffn1_w_down, ffn2_w_gu, ffn2_w_down)]

    def ffn_w(norm, w_gu, w_down, l):
        return (_row(norm[l]), w_gu, w_down, l)

    def even_w(e, l):
        wi = even_w_in[e]
        o1 = SSD_INNER
        o2 = o1 + SSD_CONV_DIM
        o3 = o2 + SSD_HEADS
        sel = (np.arange(LANES)[:, None] == np.arange(SSD_INNER)[None, :] // SSD_HEAD_DIM).astype(np.float32)
        return dict(
            g=_row(mix_norm[l]), wz=wi[:, :o1].astype(BF16), wxbc=wi[:, o1:o2].astype(BF16),
            wdt=_pad_lanes(wi[:, o2:o3]).astype(BF16), wga=wi[:, o3:o3 + CF_CH].astype(BF16),
            wgb=wi[:, o3 + CF_CH:].astype(BF16), cw=ssd_conv_w[e], cb=_row(ssd_conv_b[e]),
            dtb=_pad_lanes(_row(ssd_dt_bias[e])), alog=_pad_lanes(_row(ssd_a_log[e])),
            dfull=_row(jnp.repeat(ssd_d[e], SSD_HEAD_DIM)), sn=_row(ssd_norm[e]), sel=jnp.asarray(sel, BF16),
            cfw=cf_dw_w[e], cfb=_row(cf_dw_b[e]), cfg=_row(cf_ln_g[e]), cfbeta=_row(cf_ln_b[e]),
            woy=even_w_out[e, :SSD_INNER].astype(BF16), woc=even_w_out[e, SSD_INNER:].astype(BF16))

    def odd_w(o, l):
        return dict(g=_row(mix_norm[l]), win=odd_w_in[o].astype(BF16), lb=hg_lower_bound.astype(F32),
                    gn=_row(hg_norm[o]), woa=odd_w_out[o, :MOBA_WIDTH].astype(BF16),
                    woh=odd_w_out[o, MOBA_WIDTH:].astype(BF16))

    cache_kt = jnp.swapaxes(cache_k, -1, -2)
    cache_vt = jnp.swapaxes(cache_v, -1, -2)
    layer_w = []
    for l in range(depth):
        mix = even_w(l // 2, l) if l % 2 == 0 else odd_w(l // 2, l)
        layer_w.append((ffn_w(ffn1_norm, ffn_bf16[0], ffn_bf16[1], l), mix,
                        ffn_w(ffn2_norm, ffn_bf16[2], ffn_bf16[3], l), _row(final_norm) if l == depth - 1 else None))

    steps = bp * lp // TILE
    n_chunks = page_table.shape[1] // GATE_CHUNK
    host_layer = depth - 1 if depth % 2 == 0 else depth - 2
    if host_layer < 1 or steps % n_chunks or 2 * depth * (steps // n_chunks) != nb:
        host_layer = None

    def sample_group(hosted_gate):
        xs = x_sample.reshape(nb, d)
        ssd_s, sc_s, cf_s, k_s, v_s, hg_s = [], [], [], [], [], []
        for l, (w1, w, w2, fin) in enumerate(layer_w):
            xs = _ffn(xs, w1, tm=nb)
            if l % 2 == 0:
                e = l // 2
                xs, st, xt, ut = _even_sample(xs, w, state_ssd[e], state_ssd_conv[e], state_cf_conv[e])
                ssd_s.append(st)
                sc_s.append(xt)
                cf_s.append(ut)
            else:
                o = l // 2
                pre = _odd_sample_pre(xs, w, l, state_hg[o], page_table, past_len)
                if l == host_layer:
                    idx = hosted_gate(pre["qcol"], o)
                else:
                    idx = _sample_gate(page_table, cache_kt, pre["qcol"], o)
                xs = _odd_sample_post(xs, w, pre, idx, cache_kt, cache_vt, o, page_table)
                k_s.append(pre["kn"])
                v_s.append(pre["vn"])
                hg_s.append(pre["state"])
            xs = _ffn(xs, w2, tm=nb, final_g=fin)
        return xs.reshape(nb, 1, d), ssd_s, sc_s, cf_s, k_s, v_s, hg_s

    def prompt_group(hosted):
        xp = x_prompt.reshape(bp * lp, d)
        ssd_p, sc_p, cf_p, k_p, v_p, hg_p, ids = [], [], [], [], [], [], []

        def ffn(x, *args, **kw):
            if hosted is None:
                return _ffn(x, *args, **kw)
            qcol, cache_layer = hosted
            x, part = _ffn(x, *args, gate=(page_table, cache_kt, qcol, cache_layer, len(ids) * (steps // n_chunks)),
                           **kw)
            ids.append(part)
            return x

        for l, (w1, w, w2, fin) in enumerate(layer_w):
            xp = ffn(xp, w1, tm=TILE)
            if l % 2 == 0:
                xp3, st, xt, ut = _even_prompt(xp.reshape(bp, lp, d), w)
                xp = xp3.reshape(bp * lp, d)
                ssd_p.append(st)
                sc_p.append(xt)
                cf_p.append(ut)
                xp = ffn(xp, w2, tm=TILE, final_g=fin)
            else:
                q, kb, vt, kmean, ko, vo, ohg, hst = _odd_prompt_proj(xp.reshape(bp, lp, d), w, l)
                att = _moba_prompt(q, kb, vt, kmean[:, :, 0, :])
                k_p.append(jnp.swapaxes(ko, -1, -2))
                v_p.append(jnp.swapaxes(vo, -1, -2))
                hg_p.append(hst)
                xp = ffn(xp, w2, tm=TILE, final_g=fin, pre=(att.reshape(bp * lp, MOBA_WIDTH),
                                                            ohg.reshape(bp * lp, HG_WIDTH), w["woa"], w["woh"]))
        return (xp.reshape(bp, lp, d), ssd_p, sc_p, cf_p, k_p, v_p, hg_p), ids

    prompt_result = []

    def hosted_gate(qcol, cache_layer):
        out, ids = prompt_group((qcol, cache_layer))
        prompt_result.append(out)
        return jnp.concatenate(ids, axis=0)

    sample_out = sample_group(hosted_gate)
    prompt_out = prompt_result[0] if prompt_result else prompt_group(None)[0]
    y_p, ssd_p, sc_p, cf_p, k_p, v_p, hg_p = prompt_out
    y_s, ssd_s, sc_s, cf_s, k_s, v_s, hg_s = sample_out
    st = jnp.stack
    return (y_p, y_s, st(ssd_p), st(ssd_s), st(sc_p), st(sc_s), st(cf_p), st(cf_s),
            st(k_p), st(k_s), st(v_p), st(v_s), st(hg_p), st(hg_s))
```

```python
import functools
import math

import numpy as np
import jax
import jax.numpy as jnp
from jax import lax
from jax.experimental import pallas as pl
from jax.experimental.pallas import tpu as pltpu

F32 = jnp.float32
BF16 = jnp.bfloat16

SSD_HEADS = 16
SSD_HEAD_DIM = 64
SSD_INNER = SSD_HEADS * SSD_HEAD_DIM
SSD_GROUPS = 2
SSD_STATE = 64
SSD_CONV = 4
SSD_CONV_DIM = SSD_INNER + 2 * SSD_GROUPS * SSD_STATE
CF_CH = 512
CF_WIDTH = 31
MOBA_HEADS = 8
MOBA_HEAD_DIM = 64
MOBA_WIDTH = MOBA_HEADS * MOBA_HEAD_DIM
MOBA_BLOCK = 256
MOBA_TOPK = 3
ROPE_DIM = MOBA_HEAD_DIM // 4
ROPE_THETA = 500000.0
HG_HEADS = 4
HG_K = 128
HG_V = 128
HG_WIDTH = HG_HEADS * HG_K
PAGE_SIZE = 128
NORM_EPS = 1e-6
NEG = -1e30

LANES = 128
SSD_CHUNK = 128
TILE = 256
VT_ROWS = MOBA_HEAD_DIM + 16
VMEM_LIMIT = 56 * 1024 * 1024


def _sigmoid(x):
    return 1.0 / (1.0 + jnp.exp(-x))


def _silu(x):
    return x * _sigmoid(x)


def _softplus(x):
    return jnp.maximum(x, 0.0) + jnp.log1p(jnp.exp(-jnp.abs(x)))


def _rms(x, g):
    return x * lax.rsqrt(jnp.mean(x * x, axis=-1, keepdims=True) + NORM_EPS) * g


def _dot(a, b):
    return jnp.dot(a.astype(BF16), b.astype(BF16), preferred_element_type=F32)


def _dot_nt(a, b):
    return lax.dot_general(a.astype(BF16), b.astype(BF16), (((1,), (1,)), ((), ())),
                           preferred_element_type=F32)


def _split3(a):
    hi = a.astype(BF16)
    r = a - hi.astype(F32)
    mid = r.astype(BF16)
    lo = (r - mid.astype(F32)).astype(BF16)
    return hi, mid, lo


def _dot3_l(a, m):
    hi, mid, lo = _split3(a)
    return (jnp.dot(hi, m, preferred_element_type=F32) + jnp.dot(mid, m, preferred_element_type=F32)
            + jnp.dot(lo, m, preferred_element_type=F32))


def _dot3_r(m, parts):
    hi, mid, lo = parts
    return (jnp.dot(m, hi, preferred_element_type=F32) + jnp.dot(m, mid, preferred_element_type=F32)
            + jnp.dot(m, lo, preferred_element_type=F32))


def _const_spec(shape):
    n = len(shape)
    return pl.BlockSpec(shape, lambda *_: (0,) * n, pipeline_mode=pl.Buffered(1))


def _params(sem):
    return pltpu.CompilerParams(dimension_semantics=sem, vmem_limit_bytes=VMEM_LIMIT)


GATE_CHUNK = 16


def _gate_copies(pt_ref, ck_ref, kbuf, sem, layer, seq, chunk, slot):
    return [pltpu.make_async_copy(ck_ref.at[layer, pt_ref[seq, chunk * GATE_CHUNK + j]], kbuf.at[slot, j],
                                  sem.at[slot]) for j in range(GATE_CHUNK)]


def _gate_fetch(pt_ref, ck_ref, kbuf, sem, *, layer, seq0, n_chunks, n_steps):
    step = pl.program_id(0)
    slot = step % 2
    seq = seq0 + step // n_chunks
    c = step % n_chunks

    @pl.when(step == 0)
    def _():
        for cp in _gate_copies(pt_ref, ck_ref, kbuf, sem, layer, seq, c, slot):
            cp.start()

    nxt = step + 1

    @pl.when(nxt < n_steps)
    def _():
        for cp in _gate_copies(pt_ref, ck_ref, kbuf, sem, layer, seq0 + nxt // n_chunks, nxt % n_chunks, 1 - slot):
            cp.start()

    for cp in _gate_copies(pt_ref, ck_ref, kbuf, sem, layer, seq, c, slot):
        cp.wait()
    return slot, c


def _gate_accumulate(kbuf, q_ref, g_ref, slot, c):
    ppb = MOBA_BLOCK // PAGE_SIZE

    @pl.when(c == 0)
    def _():
        g_ref[...] = jnp.zeros(g_ref.shape, F32)

    lane = lax.broadcasted_iota(jnp.int32, (MOBA_HEADS, LANES), 1)
    sub = lax.broadcasted_iota(jnp.int32, (MOBA_HEADS, LANES), 0)
    g = g_ref[...]
    for jb in range(GATE_CHUNK // ppb):
        tile = jnp.zeros((MOBA_HEADS, LANES), F32)
        for h in range(MOBA_HEADS):
            ksum = kbuf[slot, jb * ppb, h]
            for r in range(1, ppb):
                ksum = ksum + kbuf[slot, jb * ppb + r, h]
            t = jnp.sum(ksum * q_ref[0, h], axis=0, keepdims=True)
            tile = jnp.where(sub == h, t, tile)
        col = jnp.sum(tile, axis=1, keepdims=True)
        g = jnp.where(lane == c * (GATE_CHUNK // ppb) + jb, g + col, g)
    g_ref[...] = g


def _gate_finish(g_ref, idx_ref, c, n_chunks):
    nblk = n_chunks * GATE_CHUNK // (MOBA_BLOCK // PAGE_SIZE)

    @pl.when(c == n_chunks - 1)
    def _():
        lane = lax.broadcasted_iota(jnp.int32, (MOBA_HEADS, LANES), 1)
        gcur = jnp.where(lane < nblk, g_ref[...] * (1.0 / MOBA_BLOCK), -jnp.inf)
        out = jnp.zeros((MOBA_HEADS, LANES), jnp.int32)
        for r in range(MOBA_TOPK):
            mx = jnp.max(gcur, axis=1, keepdims=True)
            first = jnp.min(jnp.where(gcur == mx, lane, LANES), axis=1, keepdims=True)
            out = jnp.where(lane == r, first, out)
            gcur = jnp.where(lane == first, -jnp.inf, gcur)
        idx_ref[0] = out


def _gate_scratch():
    return [pltpu.VMEM((2, GATE_CHUNK, MOBA_HEADS, MOBA_HEAD_DIM, PAGE_SIZE), F32), pltpu.SemaphoreType.DMA((2,)),
            pltpu.VMEM((MOBA_HEADS, LANES), F32)]


def _sample_gate_body(pt_ref, ck_ref, q_ref, idx_ref, kbuf, sem, g_ref, **job):
    slot, c = _gate_fetch(pt_ref, ck_ref, kbuf, sem, **job)
    _gate_accumulate(kbuf, q_ref, g_ref, slot, c)
    _gate_finish(g_ref, idx_ref, c, job["n_chunks"])


def _sample_gate(page_table, cache_kt, qcol, layer):
    nb, n_pages = page_table.shape
    n_chunks = n_pages // GATE_CHUNK
    job = dict(layer=layer, seq0=0, n_chunks=n_chunks, n_steps=nb * n_chunks)
    return pl.pallas_call(
        functools.partial(_sample_gate_body, **job),
        grid_spec=pltpu.PrefetchScalarGridSpec(
            num_scalar_prefetch=1, grid=(nb * n_chunks,),
            in_specs=[pl.BlockSpec(memory_space=pl.ANY),
                      pl.BlockSpec((1,) + qcol.shape[1:], lambda s, pt: (s // n_chunks, 0, 0, 0))],
            out_specs=pl.BlockSpec((1, MOBA_HEADS, LANES), lambda s, pt: (s // n_chunks, 0, 0)),
            scratch_shapes=_gate_scratch()),
        out_shape=jax.ShapeDtypeStruct((nb, MOBA_HEADS, LANES), jnp.int32),
        compiler_params=_params(("arbitrary",)), name="sample_gate")(page_table, cache_kt, qcol)


def _ffn_body(*refs, pre, final, job):
    it = iter(refs)
    if job is not None:
        pt_ref = next(it)
    x_ref = next(it)
    if pre:
        a_ref, b_ref, wa_ref, wb_ref = next(it), next(it), next(it), next(it)
    g_ref, wgu_ref, wd_ref = next(it), next(it), next(it)
    if final:
        fg_ref = next(it)
    if job is not None:
        ck_ref, q_ref = next(it), next(it)
    o_ref = next(it)
    if job is not None:
        idx_ref, kbuf, sem, gacc_ref = next(it), next(it), next(it), next(it)
        slot, c = _gate_fetch(pt_ref, ck_ref, kbuf, sem, **job)
        _gate_accumulate(kbuf, q_ref, gacc_ref, slot, c)
    d_ff = wd_ref.shape[1]
    x = x_ref[...]
    if pre:
        x = x + _dot(a_ref[...], wa_ref[...]) + _dot(b_ref[...], wb_ref[...])
    hb = _rms(x, g_ref[...]).astype(BF16)
    g = jnp.dot(hb, wgu_ref[0, :, :d_ff], preferred_element_type=F32)
    u = jnp.dot(hb, wgu_ref[0, :, d_ff:], preferred_element_type=F32)
    act = (_silu(g) * u).astype(BF16)
    y = x + 0.5 * jnp.dot(act, wd_ref[0], preferred_element_type=F32)
    if final:
        y = _rms(y, fg_ref[...])
    o_ref[...] = y
    if job is not None:
        _gate_finish(gacc_ref, idx_ref, c, job["n_chunks"])


def _ffn(x, w, *, tm, pre=None, final_g=None, gate=None):
    m, d = x.shape
    g, wgu, wd, layer = w
    steps = m // tm
    row = lambda c: pl.BlockSpec((tm, c), lambda i, *_: (i, 0))
    layer_spec = lambda a: pl.BlockSpec((1,) + a.shape[1:], lambda i, *_: (layer, 0, 0),
                                        pipeline_mode=pl.Buffered(1))
    args, specs = [x], [row(d)]
    if pre is not None:
        a, b, wa, wb = pre
        args += [a, b, wa, wb]
        specs += [row(a.shape[1]), row(b.shape[1]), _const_spec(wa.shape), _const_spec(wb.shape)]
    args += [g, wgu, wd]
    specs += [_const_spec(g.shape), layer_spec(wgu), layer_spec(wd)]
    if final_g is not None:
        args.append(final_g)
        specs.append(_const_spec(final_g.shape))
    out_shape = jax.ShapeDtypeStruct((m, d), F32)
    if gate is None:
        return pl.pallas_call(
            functools.partial(_ffn_body, pre=pre is not None, final=final_g is not None, job=None),
            grid=(steps,), in_specs=specs, out_specs=row(d), out_shape=out_shape,
            compiler_params=_params(("parallel",)), name="ffn")(*args)
    page_table, cache_kt, qcol, cache_layer, seq0 = gate
    n_chunks = page_table.shape[1] // GATE_CHUNK
    n_seq = steps // n_chunks
    assert n_seq * n_chunks == steps
    job = dict(layer=cache_layer, seq0=seq0, n_chunks=n_chunks, n_steps=steps)
    specs += [pl.BlockSpec(memory_space=pl.ANY),
              pl.BlockSpec((1,) + qcol.shape[1:], lambda i, pt: (seq0 + i // n_chunks, 0, 0, 0))]
    return pl.pallas_call(
        functools.partial(_ffn_body, pre=pre is not None, final=final_g is not None, job=job),
        grid_spec=pltpu.PrefetchScalarGridSpec(
            num_scalar_prefetch=1, grid=(steps,), in_specs=specs,
            out_specs=(row(d), pl.BlockSpec((1, MOBA_HEADS, LANES), lambda i, pt: (i // n_chunks, 0, 0))),
            scratch_shapes=_gate_scratch()),
        out_shape=(out_shape, jax.ShapeDtypeStruct((n_seq, MOBA_HEADS, LANES), jnp.int32)),
        compiler_params=_params(("arbitrary",)), name="ffn_gate")(page_table, *args, cache_kt, qcol)


def _even_prompt_body(x_ref, g_ref, wz_ref, wxbc_ref, wdt_ref, wga_ref, wgb_ref,
                      cw_ref, cb_ref, dtb_ref, alog_ref, dfull_ref, sn_ref, sel_ref,
                      cfw_ref, cfb_ref, cfg_ref, cfbeta_ref, woy_ref, woc_ref,
                      xo_ref, st_ref, xtail_ref, utail_ref,
                      xbuf, ubuf, s_ref, ybuf, *, nt):
    t = pl.program_id(1)
    T = TILE
    C = SSD_CHUNK

    @pl.when(t == 0)
    def _():
        xbuf[0:8, :] = jnp.zeros((8, SSD_CONV_DIM), F32)
        ubuf[0:32, :] = jnp.zeros((32, CF_CH), F32)
        s_ref[...] = jnp.zeros(s_ref.shape, F32)

    x = x_ref[0]
    hb = _rms(x, g_ref[...]).astype(BF16)
    z = jnp.dot(hb, wz_ref[...], preferred_element_type=F32)
    xbc = jnp.dot(hb, wxbc_ref[...], preferred_element_type=F32)
    dtr = jnp.dot(hb, wdt_ref[...], preferred_element_type=F32)
    ga = jnp.dot(hb, wga_ref[...], preferred_element_type=F32)
    gb = jnp.dot(hb, wgb_ref[...], preferred_element_type=F32)

    xbuf[8:8 + T, :] = xbc
    acc = cw_ref[0:1, :] * xbuf[5:5 + T, :]
    for k in range(1, SSD_CONV):
        acc = acc + cw_ref[k:k + 1, :] * xbuf[5 + k:5 + k + T, :]
    tail8 = xbuf[T:T + 8, :]
    xtail_ref[0] = tail8
    xbuf[0:8, :] = tail8
    xc = _silu(acc + cb_ref[...])
    xs = xc[:, 0:SSD_INNER]
    bm = xc[:, SSD_INNER:SSD_INNER + LANES]
    cm = xc[:, SSD_INNER + LANES:SSD_INNER + 2 * LANES]
    dt = _softplus(dtr + dtb_ref[...])
    dta = dt * (-jnp.exp(alog_ref[...]))
    sel = sel_ref[...]

    ri = lax.broadcasted_iota(jnp.int32, (C, C), 0)
    ci = lax.broadcasted_iota(jnp.int32, (C, C), 1)
    tri = ri >= ci
    trib = tri.astype(BF16)
    low = ci < SSD_HEAD_DIM

    for c in range(T // C):
        r0 = c * C
        xs_c, bm_c, cm_c = xs[r0:r0 + C], bm[r0:r0 + C], cm[r0:r0 + C]
        dt_c = dt[r0:r0 + C]
        cum = _dot3_r(trib, _split3(dta[r0:r0 + C]))
        cum_t = cum.T
        dt_t = dt_c.T
        cum_last = cum[C - 1:C, :]
        ecum_full = _dot3_l(jnp.exp(cum), sel)
        tail_full = _dot3_l(jnp.exp(cum_last - cum) * dt_c, sel)
        dlast_full = _dot3_l(jnp.broadcast_to(jnp.exp(cum_last), (8, LANES)), sel)[0:1]
        bm_t = bm_c.T
        cmb = cm_c.astype(BF16)
        for grp in range(SSD_GROUPS):
            bm_tg = jnp.where((ri // SSD_STATE) == grp, bm_t, 0.0).astype(BF16)
            gmat = jnp.dot(cmb, bm_tg, preferred_element_type=F32)
            pairs = SSD_HEADS // SSD_GROUPS // 2
            for pp in range(pairs):
                p = grp * pairs + pp
                lanes = slice(p * LANES, (p + 1) * LANES)
                xp = xs_c[:, lanes]
                xpb = xp.astype(BF16)
                ys = []
                for e in range(2):
                    h = 2 * p + e
                    seg = cum[:, h:h + 1] - cum_t[h:h + 1, :]
                    wm = gmat * jnp.exp(jnp.where(tri, seg, NEG)) * dt_t[h:h + 1, :]
                    ys.append(jnp.dot(wm.astype(BF16), xpb, preferred_element_type=F32))
                yp = jnp.where(low, ys[0], ys[1])
                sp = s_ref[p]
                yp = yp + jnp.dot(cmb, sp.astype(BF16), preferred_element_type=F32) * ecum_full[:, lanes]
                s_ref[p] = dlast_full[:, lanes] * sp + jnp.dot(
                    bm_tg, (xp * tail_full[:, lanes]).astype(BF16), preferred_element_type=F32)
                ybuf[r0:r0 + C, lanes] = yp

    @pl.when(t == nt - 1)
    def _():
        for p in range(SSD_HEADS // 2):
            st_ref[0, p] = s_ref[p].T

    y = (ybuf[...] + dfull_ref[...] * xs) * _silu(z)
    gw = SSD_INNER // SSD_GROUPS
    yn = []
    for grp in range(SSD_GROUPS):
        yg = y[:, grp * gw:(grp + 1) * gw]
        yn.append(_rms(yg, sn_ref[:, grp * gw:(grp + 1) * gw]).astype(BF16))

    u = ga * _sigmoid(gb)
    ubuf[32:32 + T, :] = u
    base = 32 - (CF_WIDTH - 1)
    cacc = None
    for r in range(8):
        offs = [o for o in range(base, base + CF_WIDTH) if o % 8 == r]
        if not offs:
            continue
        ur = ubuf[r:max(offs) + T, :]
        part = None
        for o in offs:
            term = cfw_ref[o - base:o - base + 1, :] * ur[o - r:o - r + T]
            part = term if part is None else part + term
        cacc = part if cacc is None else cacc + part
    tail32 = ubuf[T:T + 32, :]
    utail_ref[0] = tail32
    ubuf[0:32, :] = tail32
    c32 = cacc + cfb_ref[...]
    mu = jnp.mean(c32, axis=-1, keepdims=True)
    var = jnp.mean(jnp.square(c32 - mu), axis=-1, keepdims=True)
    c32 = _silu((c32 - mu) * lax.rsqrt(var + NORM_EPS) * cfg_ref[...] + cfbeta_ref[...])

    out = jnp.dot(c32.astype(BF16), woc_ref[...], preferred_element_type=F32)
    for grp in range(SSD_GROUPS):
        out = out + jnp.dot(yn[grp], woy_ref[grp * gw:(grp + 1) * gw, :], preferred_element_type=F32)
    xo_ref[0] = x + out


def _even_prompt(x, w):
    b, l, d = x.shape
    nt = l // TILE
    consts = [w[k] for k in ("g", "wz", "wxbc", "wdt", "wga", "wgb", "cw", "cb", "dtb", "alog", "dfull", "sn",
                             "sel", "cfw", "cfb", "cfg", "cfbeta", "woy", "woc")]
    out_shape = (jax.ShapeDtypeStruct((b, l, d), F32),
                 jax.ShapeDtypeStruct((b, SSD_HEADS // 2, LANES, LANES), F32),
                 jax.ShapeDtypeStruct((b, 8, SSD_CONV_DIM), F32),
                 jax.ShapeDtypeStruct((b, 32, CF_CH), F32))
    out_specs = (pl.BlockSpec((1, TILE, d), lambda i, t: (i, t, 0)),
                 pl.BlockSpec((1, SSD_HEADS // 2, LANES, LANES), lambda i, t: (i, 0, 0, 0)),
                 pl.BlockSpec((1, 8, SSD_CONV_DIM), lambda i, t: (i, 0, 0)),
                 pl.BlockSpec((1, 32, CF_CH), lambda i, t: (i, 0, 0)))
    xo, st, xtail, utail = pl.pallas_call(
        functools.partial(_even_prompt_body, nt=nt),
        grid=(b, nt),
        in_specs=[pl.BlockSpec((1, TILE, d), lambda i, t: (i, t, 0))] + [_const_spec(c.shape) for c in consts],
        out_specs=out_specs, out_shape=out_shape,
        scratch_shapes=[pltpu.VMEM((TILE + 8, SSD_CONV_DIM), F32), pltpu.VMEM((TILE + 32, CF_CH), F32),
                        pltpu.VMEM((SSD_HEADS // 2, LANES, LANES), F32), pltpu.VMEM((TILE, SSD_INNER), F32)],
        compiler_params=_params(("arbitrary", "arbitrary")), name="even_prompt")(x, *consts)
    half = SSD_HEADS // 2 // SSD_GROUPS
    st = st.reshape(b, SSD_HEADS // 2, 2, SSD_HEAD_DIM, SSD_GROUPS, SSD_STATE)
    state = jnp.concatenate([st[:, grp * half:(grp + 1) * half, :, :, grp, :] for grp in range(SSD_GROUPS)], axis=1)
    state = state.reshape(b, SSD_HEADS, SSD_HEAD_DIM, SSD_STATE)
    return xo, state, xtail[:, 8 - (SSD_CONV - 1):], utail[:, 32 - (CF_WIDTH - 1):]


def _hg_levels(T):
    t = np.arange(T)
    le = (t[None, :] <= t[:, None]).astype(np.float32)
    ds, ms = [], []
    m = 1
    while m < T:
        rb = (t // (2 * m)) * 2 * m + m - 1
        if m < 8:
            ds.append(le - (t[None, :] <= rb[:, None]).astype(np.float32))
        same = (t[:, None] // (2 * m)) == (t[None, :] // (2 * m))
        ms.append((same & ((t[:, None] % (2 * m)) >= m) & ((t[None, :] % (2 * m)) < m)).astype(np.float32))
        m *= 2
    return (jnp.asarray(le, BF16), jnp.asarray(np.stack(ds), BF16), jnp.asarray(np.stack(ms), BF16))


def _rope_tables(pos):
    half = ROPE_DIM // 2
    inv = ROPE_THETA ** (-np.arange(half, dtype=np.float64) / half)
    ang = pos.astype(np.float64)[:, None] * inv[None, :]
    cos, sin = np.cos(ang), np.sin(ang)
    n = pos.shape[0]
    one = np.ones((n, MOBA_HEAD_DIM - ROPE_DIM))
    zero = np.zeros((n, MOBA_HEAD_DIM - ROPE_DIM))
    zh = np.zeros((n, half))
    c = np.concatenate([cos, cos, one], axis=1)
    s1 = np.concatenate([-sin, zh, zero], axis=1)
    s2 = np.concatenate([zh, sin, zero], axis=1)
    return tuple(jnp.asarray(np.concatenate([a, a], axis=1), F32) for a in (c, s1, s2))


def _rope(x, c, s1, s2):
    outs = []
    for j in range(x.shape[1] // LANES):
        blk = x[:, j * LANES:(j + 1) * LANES]
        outs.append(blk * c + pltpu.roll(blk, LANES - ROPE_DIM // 2, 1) * s1 + pltpu.roll(blk, ROPE_DIM // 2, 1) * s2)
    return outs


def _hg_lower_bound(lb_ref, layer):
    a = lb_ref[...]
    mx = jnp.max(a, axis=0, keepdims=True)
    e = jnp.exp(a - mx)
    return jnp.sum(e[1:layer + 1], axis=0, keepdims=True) / jnp.sum(e, axis=0, keepdims=True)


def _odd_prompt_body(x_ref, g_ref, win_ref, c_ref, s1_ref, s2_ref, lb_ref, gn_ref, tril_ref, dm_ref, mm_ref,
                     q_ref, kb_ref, vt_ref, km_ref, ko_ref, vo_ref, o_ref, hs_ref,
                     st_ref, *, nt, layer):
    t = pl.program_id(1)
    T = TILE
    W = MOBA_WIDTH

    @pl.when(t == 0)
    def _():
        st_ref[...] = jnp.zeros(st_ref.shape, F32)

    x = x_ref[0]
    hb = _rms(x, g_ref[...]).astype(BF16)
    proj = jnp.dot(hb, win_ref[...], preferred_element_type=F32)

    c, s1, s2 = c_ref[...], s1_ref[...], s2_ref[...]
    qb = _rope(proj[:, 0:W], c, s1, s2)
    kb = _rope(proj[:, W:2 * W], c, s1, s2)
    v = proj[:, 2 * W:3 * W]
    for j in range(W // LANES):
        lanes = slice(j * LANES, (j + 1) * LANES)
        q_ref[0, :, lanes] = qb[j]
        kb_ref[0, :, lanes] = kb[j].astype(BF16)
        km_ref[0, 0, :, lanes] = jnp.mean(kb[j], axis=0, keepdims=True)
        kt = kb[j].T
        vt = v[:, lanes].T
        for e in range(2):
            ko_ref[0, 2 * j + e] = kt[e * 64:(e + 1) * 64, :]
            vo_ref[0, 2 * j + e] = vt[e * 64:(e + 1) * 64, :]
            r0 = (2 * j + e) * VT_ROWS
            vt_ref[0, r0:r0 + 64, :] = vt[e * 64:(e + 1) * 64, :].astype(BF16)
            vt_ref[0, r0 + 64:r0 + VT_ROWS, :] = (
                lax.broadcasted_iota(jnp.int32, (VT_ROWS - 64, T), 0) == 0).astype(BF16)

    lb = _hg_lower_bound(lb_ref, layer)
    hq = _silu(proj[:, 3 * W:3 * W + HG_WIDTH])
    f = lb + (1.0 - lb) * _sigmoid(proj[:, 3 * W + HG_WIDTH:3 * W + 2 * HG_WIDTH])
    hv = proj[:, 3 * W + 2 * HG_WIDTH:3 * W + 3 * HG_WIDTH]
    hgate = proj[:, 3 * W + 3 * HG_WIDTH:3 * W + 4 * HG_WIDTH]
    kk = 1.0 - f
    lf3 = _split3(jnp.log(f))
    cum = _dot3_r(tril_ref[...], lf3)
    nlev = mm_ref.shape[0]
    ri = lax.broadcasted_iota(jnp.int32, (T, T), 0)
    ci = lax.broadcasted_iota(jnp.int32, (T, T), 1)
    eye = ri == ci
    amat = []
    for h in range(HG_HEADS):
        lanes = slice(h * HG_K, (h + 1) * HG_K)
        diag = jnp.sum(hq[:, lanes] * kk[:, lanes], axis=1, keepdims=True)
        amat.append(jnp.where(eye, diag, 0.0))
    for lev in range(nlev):
        m = 1 << lev
        if lev < dm_ref.shape[0]:
            d = _dot3_r(dm_ref[lev], lf3)
        else:
            refs = [jnp.broadcast_to(cum[r0 + m - 1:r0 + m, :], (2 * m, cum.shape[1])) for r0 in range(0, T, 2 * m)]
            d = cum - (refs[0] if len(refs) == 1 else jnp.concatenate(refs, axis=0))
        e_all = jnp.exp(-jnp.abs(d))
        mk = mm_ref[lev].astype(F32)
        for h in range(HG_HEADS):
            lanes = slice(h * HG_K, (h + 1) * HG_K)
            a_m = _dot_nt(hq[:, lanes] * e_all[:, lanes], kk[:, lanes] * e_all[:, lanes])
            amat[h] = amat[h] + a_m * mk
    cum_last = cum[T - 1:T, :]
    ecum = jnp.exp(cum)
    ktail = kk * jnp.exp(cum_last - cum)
    elast = jnp.exp(cum_last)
    for h in range(HG_HEADS):
        lanes = slice(h * HG_K, (h + 1) * HG_K)
        vh = hv[:, lanes]
        vhb = vh.astype(BF16)
        st = st_ref[h]
        o = jnp.dot(amat[h].astype(BF16), vhb, preferred_element_type=F32)
        o = o + _dot_nt(hq[:, lanes] * ecum[:, lanes], st)
        st_new = elast[:, lanes] * st + jnp.dot(vh.T.astype(BF16), ktail[:, lanes].astype(BF16),
                                                preferred_element_type=F32)
        st_ref[h] = st_new
        o_ref[0, :, lanes] = _rms(o, gn_ref[:, lanes]) * _silu(hgate[:, lanes])

    @pl.when(t == nt - 1)
    def _():
        for h in range(HG_HEADS):
            hs_ref[0, h] = st_ref[h].T


def _odd_prompt_proj(x, w, layer):
    b, l, d = x.shape
    nt = l // TILE
    c, s1, s2 = _rope_tables(np.arange(l))
    tril, dm, mm = _hg_levels(TILE)
    consts_a = [w["g"], w["win"]]
    consts_b = [w["lb"], w["gn"], tril, dm, mm]
    tab = pl.BlockSpec((TILE, LANES), lambda i, t: (t, 0))
    W = MOBA_WIDTH
    out_shape = (jax.ShapeDtypeStruct((b, l, W), F32),
                 jax.ShapeDtypeStruct((b, l, W), BF16),
                 jax.ShapeDtypeStruct((b, MOBA_HEADS * VT_ROWS, l), BF16),
                 jax.ShapeDtypeStruct((b, nt, 1, W), F32),
                 jax.ShapeDtypeStruct((b, MOBA_HEADS, MOBA_HEAD_DIM, l), F32),
                 jax.ShapeDtypeStruct((b, MOBA_HEADS, MOBA_HEAD_DIM, l), F32),
                 jax.ShapeDtypeStruct((b, l, HG_WIDTH), F32),
                 jax.ShapeDtypeStruct((b, HG_HEADS, HG_K, HG_V), F32))
    out_specs = (pl.BlockSpec((1, TILE, W), lambda i, t: (i, t, 0)),
                 pl.BlockSpec((1, TILE, W), lambda i, t: (i, t, 0)),
                 pl.BlockSpec((1, MOBA_HEADS * VT_ROWS, TILE), lambda i, t: (i, 0, t)),
                 pl.BlockSpec((1, 1, 1, W), lambda i, t: (i, t, 0, 0)),
                 pl.BlockSpec((1, MOBA_HEADS, MOBA_HEAD_DIM, TILE), lambda i, t: (i, 0, 0, t)),
                 pl.BlockSpec((1, MOBA_HEADS, MOBA_HEAD_DIM, TILE), lambda i, t: (i, 0, 0, t)),
                 pl.BlockSpec((1, TILE, HG_WIDTH), lambda i, t: (i, t, 0)),
                 pl.BlockSpec((1, HG_HEADS, HG_K, HG_V), lambda i, t: (i, 0, 0, 0)))
    return pl.pallas_call(
        functools.partial(_odd_prompt_body, nt=nt, layer=layer),
        grid=(b, nt),
        in_specs=([pl.BlockSpec((1, TILE, d), lambda i, t: (i, t, 0))] + [_const_spec(a.shape) for a in consts_a]
                  + [tab, tab, tab] + [_const_spec(a.shape) for a in consts_b]),
        out_specs=out_specs, out_shape=out_shape,
        scratch_shapes=[pltpu.VMEM((HG_HEADS, HG_V, HG_K), F32)],
        compiler_params=_params(("arbitrary", "arbitrary")), name="odd_prompt_proj")(
            x, *consts_a, c, s1, s2, *consts_b)


def _moba_tile_body(q_ref, k_ref, vt_ref, km_ref, o_ref, sel_ref, s_ref, cm_ref, p_ref):
    j = pl.program_id(2)
    T = TILE
    Q = 2 * T
    hd = MOBA_HEAD_DIM
    q = q_ref[0]
    km = km_ref[0, 0]
    nbp = km.shape[0]
    lane = lax.broadcasted_iota(jnp.int32, (Q, LANES), 1)
    qes = [jnp.where((lane // hd) == e, q, 0.0) for e in range(2)]
    qs = [(qe * (hd ** -0.5 * math.log2(math.e))).astype(BF16) for qe in qes]
    last_blk = k_ref.shape[1] // T - 1

    def scores(b, slot, keep):
        off = pl.multiple_of(jnp.minimum(b, last_blk) * T, T)
        kk = k_ref[0, pl.ds(off, T), :]
        for e in range(2):
            s = jnp.where(keep(e), _dot_nt(kk, qs[e]), NEG)
            s_ref[slot, e] = s
            cm_ref[slot, e] = jnp.max(s, axis=0, keepdims=True)

    def selected(b):
        return lambda e: sel_ref[e, pl.ds(b, 1), :] > 0.5

    def pv(b, slot, e):
        off = pl.multiple_of(jnp.clip(b, 0, last_blk) * T, T)
        return jnp.dot(vt_ref[0, e * VT_ROWS:(e + 1) * VT_ROWS, pl.ds(off, T)], p_ref[slot, e],
                       preferred_element_type=F32)

    def softmax_step(slot, e, m, acc, prev):
        m_new = jnp.maximum(m, cm_ref[slot, e])
        p_ref[slot, e] = jnp.exp2((s_ref[slot, e] - m_new).astype(BF16))
        return m_new, jnp.exp2(m - m_new) * (acc + prev)

    p_ref[...] = jnp.zeros(p_ref.shape, BF16)

    blk = lax.broadcasted_iota(jnp.int32, (nbp, Q), 0)
    col = lax.broadcasted_iota(jnp.int32, (nbp, Q), 1)
    elig = blk < 2 * j + (col >= T).astype(jnp.int32)
    kh = km.astype(BF16)
    kl = (km - kh.astype(F32)).astype(BF16)
    for e in range(2):
        qh = qes[e].astype(BF16)
        ql = (qes[e] - qh.astype(F32)).astype(BF16)
        gate = _dot_nt(kh, qh) + _dot_nt(kh, ql) + _dot_nt(kl, qh)
        gcur = jnp.where(elig, gate, -jnp.inf)
        selm = jnp.zeros((nbp, Q), F32)
        for _ in range(MOBA_TOPK):
            mx = jnp.max(gcur, axis=0, keepdims=True)
            first = jnp.min(jnp.where(gcur == mx, blk, nbp), axis=0, keepdims=True)
            hit = blk == first
            selm = jnp.where(hit & elig, 1.0, selm)
            gcur = jnp.where(hit, -jnp.inf, gcur)
        sel_ref[e] = selm

    ri = lax.broadcasted_iota(jnp.int32, (T, Q), 0)
    ci = lax.broadcasted_iota(jnp.int32, (T, Q), 1)
    carry = []
    for e in range(2):
        carry += [jnp.full((1, Q), NEG, F32), jnp.zeros((VT_ROWS, Q), F32)]

    own_a = (ci < T) & (ri <= ci)
    scores(2 * j, 0, lambda e: own_a | (sel_ref[e, pl.ds(2 * j, 1), :] > 0.5))
    scores(2 * j + 1, 1, lambda e: (ci >= T) & (ri <= ci - T))
    for e in range(2):
        carry[2 * e], carry[2 * e + 1] = softmax_step(0, e, carry[2 * e], carry[2 * e + 1], 0.0)
    scores(0, 0, selected(0))
    for e in range(2):
        carry[2 * e], carry[2 * e + 1] = softmax_step(1, e, carry[2 * e], carry[2 * e + 1], pv(2 * j, 0, e))

    def body(u, carry):
        carry = list(carry)
        scores(2 * u + 1, 1, selected(2 * u + 1))
        for e in range(2):
            prev = pv(jnp.where(u == 0, 2 * j + 1, 2 * u - 1), 1, e)
            carry[2 * e], carry[2 * e + 1] = softmax_step(0, e, carry[2 * e], carry[2 * e + 1], prev)
        scores(2 * u + 2, 0, selected(2 * u + 2))
        for e in range(2):
            carry[2 * e], carry[2 * e + 1] = softmax_step(1, e, carry[2 * e], carry[2 * e + 1], pv(2 * u, 0, e))
        return tuple(carry)

    fin = lax.fori_loop(0, j, body, tuple(carry))
    outs = []
    for e in range(2):
        tot = fin[2 * e + 1] + pv(jnp.where(j == 0, 1, 2 * j - 1), 1, e)
        outs.append(tot[0:hd] / tot[hd:hd + 1])
    o_ref[0] = jnp.concatenate(outs, axis=0).T


def _moba_tiles(q, kb, vt, kmean):
    b, l, w = q.shape
    tq = 2 * TILE
    assert l % tq == 0
    npair = w // LANES
    nb = kmean.shape[1]
    km = kmean.reshape(b, nb, npair, LANES).transpose(0, 2, 1, 3)
    nbp = -(-nb // 8) * 8
    km = jnp.pad(km, ((0, 0), (0, 0), (0, nbp - nb), (0, 0)))
    return pl.pallas_call(
        _moba_tile_body,
        grid=(b, npair, l // tq),
        in_specs=[pl.BlockSpec((1, tq, LANES), lambda i, p, t: (i, t, p)),
                  pl.BlockSpec((1, l, LANES), lambda i, p, t: (i, 0, p)),
                  pl.BlockSpec((1, 2 * VT_ROWS, l), lambda i, p, t: (i, p, 0)),
                  pl.BlockSpec((1, 1, nbp, LANES), lambda i, p, t: (i, p, 0, 0))],
        out_specs=pl.BlockSpec((1, tq, LANES), lambda i, p, t: (i, t, p)),
        out_shape=jax.ShapeDtypeStruct((b, l, w), F32),
        scratch_shapes=[pltpu.VMEM((2, nbp, tq), F32), pltpu.VMEM((2, 2, TILE, tq), F32),
                        pltpu.VMEM((2, 2, 1, tq), F32), pltpu.VMEM((2, 2, TILE, tq), BF16)],
        compiler_params=_params(("arbitrary", "arbitrary", "arbitrary")), name="moba_prompt")(q, kb, vt, km)


def _even_sample_proj_body(x_ref, g_ref, wz_ref, wxbc_ref, wdt_ref, wga_ref, wgb_ref, cw_ref, cb_ref, dtb_ref,
                           xbuf_ref, cfw_ref, cfb_ref, cfg_ref, cfbeta_ref, ubuf_ref,
                           z_ref, xc_ref, dt_ref, xnew_ref, c_ref, unew_ref):
    x = x_ref[...]
    hb = _rms(x, g_ref[...]).astype(BF16)
    z_ref[...] = jnp.dot(hb, wz_ref[...], preferred_element_type=F32)
    xbc = jnp.dot(hb, wxbc_ref[...], preferred_element_type=F32)
    dtr = jnp.dot(hb, wdt_ref[...], preferred_element_type=F32)
    ga = jnp.dot(hb, wga_ref[...], preferred_element_type=F32)
    gb = jnp.dot(hb, wgb_ref[...], preferred_element_type=F32)
    k1 = SSD_CONV - 1
    acc = cw_ref[k1:k1 + 1, :] * xbc
    for k in range(k1):
        acc = acc + cw_ref[k:k + 1, :] * xbuf_ref[k]
        if k > 0:
            xnew_ref[k - 1] = xbuf_ref[k]
    xnew_ref[k1 - 1] = xbc
    xc_ref[...] = _silu(acc + cb_ref[...])
    dt_ref[...] = _softplus(dtr + dtb_ref[...])
    u = ga * _sigmoid(gb)
    k2 = CF_WIDTH - 1
    cacc = cfw_ref[k2:k2 + 1, :] * u
    for k in range(k2):
        cacc = cacc + cfw_ref[k:k + 1, :] * ubuf_ref[k]
        if k > 0:
            unew_ref[k - 1] = ubuf_ref[k]
    unew_ref[k2 - 1] = u
    c32 = cacc + cfb_ref[...]
    mu = jnp.mean(c32, axis=-1, keepdims=True)
    var = jnp.mean(jnp.square(c32 - mu), axis=-1, keepdims=True)
    c_ref[...] = _silu((c32 - mu) * lax.rsqrt(var + NORM_EPS) * cfg_ref[...] + cfbeta_ref[...])


STEP_ROWS = 128
HG_STEP_ROWS = 32


def _ssd_step_body(s_ref, xb_ref, dt_ref, alog_ref, b_ref, c_ref, so_ref, yt_ref):
    rows = s_ref.shape[0]
    dt = dt_ref[...]
    decay = jnp.exp(dt * (-jnp.exp(alog_ref[...])))
    bdt = b_ref[...] * dt
    cc = c_ref[...]
    for r in range(rows):
        s_new = decay[r:r + 1, :] * s_ref[r] + xb_ref[r] * bdt[r:r + 1, :]
        so_ref[r] = s_new
        yt_ref[:, r:r + 1] = jnp.sum(s_new * cc[r:r + 1, :], axis=1, keepdims=True)


def _even_sample_out_body(x_ref, y_ref, xs_ref, z_ref, dfull_ref, sn_ref, c_ref, woy_ref, woc_ref, o_ref):
    y = (y_ref[...] + dfull_ref[...] * xs_ref[...]) * _silu(z_ref[...])
    gw = SSD_INNER // SSD_GROUPS
    out = _dot(c_ref[...], woc_ref[...])
    for grp in range(SSD_GROUPS):
        lanes = slice(grp * gw, (grp + 1) * gw)
        out = out + _dot(_rms(y[:, lanes], sn_ref[:, lanes]), woy_ref[lanes, :])
    o_ref[...] = x_ref[...] + out


def _call(body, out_shape, *args, name):
    return pl.pallas_call(body, out_shape=out_shape, compiler_params=_params(None), name=name)(*args)


def _even_sample(x, w, s_ssd, buf_ssd, buf_cf):
    nb = x.shape[0]
    f = lambda *s: jax.ShapeDtypeStruct(s, F32)
    z, xc, dt, xnew, c, unew = _call(
        _even_sample_proj_body,
        (f(nb, SSD_INNER), f(nb, SSD_CONV_DIM), f(nb, LANES), f(SSD_CONV - 1, nb, SSD_CONV_DIM), f(nb, CF_CH),
         f(CF_WIDTH - 1, nb, CF_CH)),
        x, w["g"], w["wz"], w["wxbc"], w["wdt"], w["wga"], w["wgb"], w["cw"], w["cb"], w["dtb"],
        jnp.swapaxes(buf_ssd, 0, 1), w["cfw"], w["cfb"], w["cfg"], w["cfbeta"], jnp.swapaxes(buf_cf, 0, 1),
        name="even_sample_proj")
    xs = xc[:, :SSD_INNER]
    rows = nb * SSD_HEADS
    rep_heads = SSD_HEADS // SSD_GROUPS
    grp = lambda a: jnp.repeat(a.reshape(nb, SSD_GROUPS, SSD_STATE), rep_heads, axis=1).reshape(rows, SSD_STATE)
    bm = grp(xc[:, SSD_INNER:SSD_INNER + SSD_GROUPS * SSD_STATE])
    cm = grp(xc[:, SSD_INNER + SSD_GROUPS * SSD_STATE:])
    rb = STEP_ROWS
    assert rows % rb == 0
    rowspec = pl.BlockSpec((rb, SSD_STATE), lambda i: (i, 0))
    colspec = pl.BlockSpec((SSD_HEAD_DIM, rb), lambda i: (0, i))
    sspec = pl.BlockSpec((rb, SSD_HEAD_DIM, SSD_STATE), lambda i: (i, 0, 0))
    per_row = lambda a: jnp.broadcast_to(a.reshape(rows, 1), (rows, SSD_STATE))
    s_new, yt = pl.pallas_call(
        _ssd_step_body, grid=(rows // rb,),
        in_specs=[sspec, sspec, rowspec, rowspec, rowspec, rowspec],
        out_specs=(sspec, colspec),
        out_shape=(f(rows, SSD_HEAD_DIM, SSD_STATE), f(SSD_HEAD_DIM, rows)),
        compiler_params=_params(("parallel",)), name="ssd_step")(
            s_ssd.reshape(rows, SSD_HEAD_DIM, SSD_STATE),
            jnp.broadcast_to(xs.reshape(rows, SSD_HEAD_DIM, 1), (rows, SSD_HEAD_DIM, SSD_STATE)),
            per_row(dt[:, :SSD_HEADS]), per_row(jnp.tile(w["alog"][0, :SSD_HEADS], nb)), bm, cm)
    xo = _call(_even_sample_out_body, f(nb, x.shape[1]),
               x, yt.T.reshape(nb, SSD_INNER), xs, z, w["dfull"], w["sn"], c, w["woy"], w["woc"],
               name="even_sample_out")
    return (xo, s_new.reshape(s_ssd.shape), jnp.swapaxes(xnew, 0, 1), jnp.swapaxes(unew, 0, 1))


def _odd_sample_proj_body(x_ref, g_ref, win_ref, c_ref, s1_ref, s2_ref, lb_ref,
                          q_ref, k_ref, v_ref, hq_ref, f_ref, hv_ref, hg_ref, *, layer):
    W = MOBA_WIDTH
    hb = _rms(x_ref[...], g_ref[...]).astype(BF16)
    proj = jnp.dot(hb, win_ref[...], preferred_element_type=F32)
    c, s1, s2 = c_ref[...], s1_ref[...], s2_ref[...]
    qb = _rope(proj[:, 0:W], c, s1, s2)
    kb = _rope(proj[:, W:2 * W], c, s1, s2)
    for j in range(W // LANES):
        q_ref[:, j * LANES:(j + 1) * LANES] = qb[j]
        k_ref[:, j * LANES:(j + 1) * LANES] = kb[j]
    v_ref[...] = proj[:, 2 * W:3 * W]
    lb = _hg_lower_bound(lb_ref, layer)
    hq_ref[...] = _silu(proj[:, 3 * W:3 * W + HG_WIDTH])
    f_ref[...] = lb + (1.0 - lb) * _sigmoid(proj[:, 3 * W + HG_WIDTH:3 * W + 2 * HG_WIDTH])
    hv_ref[...] = proj[:, 3 * W + 2 * HG_WIDTH:3 * W + 3 * HG_WIDTH]
    hg_ref[...] = proj[:, 3 * W + 3 * HG_WIDTH:3 * W + 4 * HG_WIDTH]


def _hg_step_body(s_ref, qt_ref, ft_ref, v_ref, so_ref, o_ref):
    qt, ft, vv = qt_ref[0], ft_ref[0], v_ref[...]
    for r in range(s_ref.shape[0]):
        fc = ft[:, r:r + 1]
        s_new = fc * s_ref[r] + (1.0 - fc) * vv[r:r + 1, :]
        so_ref[r] = s_new
        o_ref[r:r + 1, :] = jnp.sum(s_new * qt[:, r:r + 1], axis=0, keepdims=True)


def _attn_copies(idx_ref, pt_ref, ck_ref, cv_ref, kbuf, vbuf, sem, layer, seq, slot):
    ppb = MOBA_BLOCK // PAGE_SIZE
    cps = []
    for h in range(MOBA_HEADS):
        for r in range(MOBA_TOPK):
            blk = idx_ref[seq, h * MOBA_TOPK + r]
            for pg in range(ppb):
                phys = pt_ref[seq, blk * ppb + pg]
                j = r * ppb + pg
                cps.append(pltpu.make_async_copy(ck_ref.at[layer, phys, h], kbuf.at[slot, h, j], sem.at[0, slot]))
                cps.append(pltpu.make_async_copy(cv_ref.at[layer, phys, h], vbuf.at[slot, h, j], sem.at[1, slot]))
    return cps


def _sample_attn_body(idx_ref, pt_ref, ck_ref, cv_ref, q_ref, kn_ref, vn_ref, o_ref, kbuf, vbuf, sem, *,
                      layer, n_seq):
    b = pl.program_id(0)
    slot = b % 2
    scale = MOBA_HEAD_DIM ** -0.5
    nsel = MOBA_TOPK * (MOBA_BLOCK // PAGE_SIZE)

    @pl.when(b == 0)
    def _():
        for cp in _attn_copies(idx_ref, pt_ref, ck_ref, cv_ref, kbuf, vbuf, sem, layer, b, slot):
            cp.start()

    @pl.when(b + 1 < n_seq)
    def _():
        for cp in _attn_copies(idx_ref, pt_ref, ck_ref, cv_ref, kbuf, vbuf, sem, layer, b + 1, 1 - slot):
            cp.start()

    for cp in _attn_copies(idx_ref, pt_ref, ck_ref, cv_ref, kbuf, vbuf, sem, layer, b, slot):
        cp.wait()

    for h in range(MOBA_HEADS):
        qc = q_ref[0, h]
        s_self = jnp.sum(qc * kn_ref[0, h], axis=0, keepdims=True) * scale
        ss = [jnp.sum(kbuf[slot, h, j] * qc, axis=0, keepdims=True) * scale for j in range(nsel)]
        m = s_self
        for s in ss:
            m = jnp.maximum(m, jnp.max(s, axis=1, keepdims=True))
        p_self = jnp.exp(s_self - m)
        l = p_self
        acc = jnp.zeros((MOBA_HEAD_DIM, LANES), F32)
        for j in range(nsel):
            p = jnp.exp(ss[j] - m)
            l = l + jnp.sum(p, axis=1, keepdims=True)
            acc = acc + vbuf[slot, h, j] * p
        o_ref[0, h] = (vn_ref[0, h] * p_self + jnp.sum(acc, axis=1, keepdims=True)) / l


def _odd_sample_out_body(x_ref, att_ref, o_ref, hg_ref, gn_ref, woa_ref, woh_ref, xo_ref):
    out = _dot(att_ref[...], woa_ref[...])
    o = o_ref[...]
    gated = []
    for h in range(HG_HEADS):
        lanes = slice(h * HG_V, (h + 1) * HG_V)
        gated.append(_rms(o[:, lanes], gn_ref[:, lanes]) * _silu(hg_ref[:, lanes]))
    out = out + _dot(jnp.concatenate(gated, axis=1), woh_ref[...])
    xo_ref[...] = x_ref[...] + out


def _odd_sample_pre(x, w, layer, s_hg, page_table, past_len):
    nb = x.shape[0]
    f = lambda *s: jax.ShapeDtypeStruct(s, F32)
    n_pages = page_table.shape[1]
    ppb = MOBA_BLOCK // PAGE_SIZE
    nblk = n_pages // ppb
    assert n_pages % ppb == 0 and nblk >= MOBA_TOPK and nblk < LANES
    assert n_pages % GATE_CHUNK == 0 and GATE_CHUNK % ppb == 0
    c, s1, s2 = _rope_tables(np.full((1,), past_len))
    q, k, v, hq, fg, hv, hg = _call(
        functools.partial(_odd_sample_proj_body, layer=layer),
        tuple(f(nb, MOBA_WIDTH) for _ in range(7)),
        x, w["g"], w["win"], c, s1, s2, w["lb"], name="odd_sample_proj")

    rows = nb * HG_HEADS
    rb = HG_STEP_ROWS
    assert rows % rb == 0
    cols = lambda a: _pad_lanes(jnp.swapaxes(a.reshape(rows // rb, rb, HG_K), 1, 2))
    sspec = pl.BlockSpec((rb, HG_K, HG_V), lambda i: (i, 0, 0))
    cspec = pl.BlockSpec((1, HG_K, LANES), lambda i: (i, 0, 0))
    rspec = pl.BlockSpec((rb, HG_V), lambda i: (i, 0))
    s_new, o = pl.pallas_call(
        _hg_step_body, grid=(rows // rb,),
        in_specs=[sspec, cspec, cspec, rspec], out_specs=(sspec, rspec),
        out_shape=(f(rows, HG_K, HG_V), f(rows, HG_V)),
        compiler_params=_params(("parallel",)), name="hg_step")(
            s_hg.reshape(rows, HG_K, HG_V), cols(hq), cols(fg), hv.reshape(rows, HG_V))

    hd = MOBA_HEAD_DIM
    col = lambda a: jnp.broadcast_to(a.reshape(nb, MOBA_HEADS, hd, 1), (nb, MOBA_HEADS, hd, LANES))
    return dict(qcol=col(q), kcol=col(k), vcol=col(v), kn=k.reshape(nb, MOBA_HEADS, 1, hd),
                vn=v.reshape(nb, MOBA_HEADS, 1, hd), o=o.reshape(nb, HG_WIDTH), hg=hg,
                state=s_new.reshape(s_hg.shape))


def _odd_sample_post(x, w, pre, idx, cache_kt, cache_vt, cache_layer, page_table):
    nb = x.shape[0]
    f = lambda *s: jax.ShapeDtypeStruct(s, F32)
    hd = MOBA_HEAD_DIM
    nsel = MOBA_TOPK * (MOBA_BLOCK // PAGE_SIZE)
    colspec = lambda nidx: pl.BlockSpec((1, MOBA_HEADS, hd, LANES), lambda i, *_: (i, 0, 0, 0))
    att = pl.pallas_call(
        functools.partial(_sample_attn_body, layer=cache_layer, n_seq=nb),
        grid_spec=pltpu.PrefetchScalarGridSpec(
            num_scalar_prefetch=2, grid=(nb,),
            in_specs=[pl.BlockSpec(memory_space=pl.ANY), pl.BlockSpec(memory_space=pl.ANY),
                      colspec(2), colspec(2), colspec(2)],
            out_specs=colspec(2),
            scratch_shapes=[pltpu.VMEM((2, MOBA_HEADS, nsel, hd, PAGE_SIZE), F32),
                            pltpu.VMEM((2, MOBA_HEADS, nsel, hd, PAGE_SIZE), F32),
                            pltpu.SemaphoreType.DMA((2, 2))]),
        out_shape=f(nb, MOBA_HEADS, hd, LANES),
        compiler_params=_params(("arbitrary",)), name="sample_attn")(
            idx[:, :, :MOBA_TOPK].reshape(nb, MOBA_HEADS * MOBA_TOPK), page_table, cache_kt, cache_vt,
            pre["qcol"], pre["kcol"], pre["vcol"])
    att = att[:, :, :, 0].reshape(nb, MOBA_WIDTH)
    return _call(_odd_sample_out_body, f(nb, x.shape[1]),
                 x, att, pre["o"], pre["hg"], w["gn"], w["woa"], w["woh"], name="odd_sample_out")


def _row(a):
    return a.reshape(1, -1).astype(F32)


def _pad_lanes(a, n=LANES):
    return jnp.pad(a, [(0, 0)] * (a.ndim - 1) + [(0, n - a.shape[-1])])


def kernel(x_prompt, x_sample, state_ssd, state_ssd_conv, state_cf_conv, cache_k, cache_v, page_table, state_hg,
           ffn1_norm, ffn1_w_gu, ffn1_w_down, mix_norm, ffn2_norm, ffn2_w_gu, ffn2_w_down, final_norm,
           even_w_in, ssd_conv_w, ssd_conv_b, ssd_dt_bias, ssd_a_log, ssd_d, ssd_norm,
           cf_dw_w, cf_dw_b, cf_ln_g, cf_ln_b, even_w_out,
           odd_w_in, hg_lower_bound, hg_norm, odd_w_out):
    depth = ffn1_norm.shape[0]
    bp, lp, d = x_prompt.shape
    nb = x_sample.shape[0]
    assert x_sample.shape[1] == 1 and lp % TILE == 0 and lp // TILE <= LANES
    past_len = page_table.shape[1] * PAGE_SIZE

    ffn_bf16 = [a.astype(BF16) for a in (ffn1_w_gu, ffn1_w_down, ffn2_w_gu, ffn2_w_down)]

    def ffn_w(norm, w_gu, w_down, l):
        return (_row(norm[l]), w_gu, w_down, l)

    def even_w(e, l):
        wi = even_w_in[e]
        o1 = SSD_INNER
        o2 = o1 + SSD_CONV_DIM
        o3 = o2 + SSD_HEADS
        sel = (np.arange(LANES)[:, None] == np.arange(SSD_INNER)[None, :] // SSD_HEAD_DIM).astype(np.float32)
        return dict(
            g=_row(mix_norm[l]), wz=wi[:, :o1].astype(BF16), wxbc=wi[:, o1:o2].astype(BF16),
            wdt=_pad_lanes(wi[:, o2:o3]).astype(BF16), wga=wi[:, o3:o3 + CF_CH].astype(BF16),
            wgb=wi[:, o3 + CF_CH:].astype(BF16), cw=ssd_conv_w[e], cb=_row(ssd_conv_b[e]),
            dtb=_pad_lanes(_row(ssd_dt_bias[e])), alog=_pad_lanes(_row(ssd_a_log[e])),
            dfull=_row(jnp.repeat(ssd_d[e], SSD_HEAD_DIM)), sn=_row(ssd_norm[e]), sel=jnp.asarray(sel, BF16),
            cfw=cf_dw_w[e], cfb=_row(cf_dw_b[e]), cfg=_row(cf_ln_g[e]), cfbeta=_row(cf_ln_b[e]),
            woy=even_w_out[e, :SSD_INNER].astype(BF16), woc=even_w_out[e, SSD_INNER:].astype(BF16))

    def odd_w(o, l):
        return dict(g=_row(mix_norm[l]), win=odd_w_in[o].astype(BF16), lb=hg_lower_bound.astype(F32),
                    gn=_row(hg_norm[o]), woa=odd_w_out[o, :MOBA_WIDTH].astype(BF16),
                    woh=odd_w_out[o, MOBA_WIDTH:].astype(BF16))

    cache_kt = jnp.swapaxes(cache_k, -1, -2)
    cache_vt = jnp.swapaxes(cache_v, -1, -2)
    layer_w = []
    for l in range(depth):
        mix = even_w(l // 2, l) if l % 2 == 0 else odd_w(l // 2, l)
        layer_w.append((ffn_w(ffn1_norm, ffn_bf16[0], ffn_bf16[1], l), mix,
                        ffn_w(ffn2_norm, ffn_bf16[2], ffn_bf16[3], l), _row(final_norm) if l == depth - 1 else None))

    steps = bp * lp // TILE
    n_chunks = page_table.shape[1] // GATE_CHUNK
    host_layer = depth - 1 if depth % 2 == 0 else depth - 2
    if host_layer < 1 or steps % n_chunks or 2 * depth * (steps // n_chunks) != nb:
        host_layer = None

    def sample_group(hosted_gate):
        xs = x_sample.reshape(nb, d)
        ssd_s, sc_s, cf_s, k_s, v_s, hg_s = [], [], [], [], [], []
        for l, (w1, w, w2, fin) in enumerate(layer_w):
            xs = _ffn(xs, w1, tm=nb)
            if l % 2 == 0:
                e = l // 2
                xs, st, xt, ut = _even_sample(xs, w, state_ssd[e], state_ssd_conv[e], state_cf_conv[e])
                ssd_s.append(st)
                sc_s.append(xt)
                cf_s.append(ut)
            else:
                o = l // 2
                pre = _odd_sample_pre(xs, w, l, state_hg[o], page_table, past_len)
                if l == host_layer:
                    idx = hosted_gate(pre["qcol"], o)
                else:
                    idx = _sample_gate(page_table, cache_kt, pre["qcol"], o)
                xs = _odd_sample_post(xs, w, pre, idx, cache_kt, cache_vt, o, page_table)
                k_s.append(pre["kn"])
                v_s.append(pre["vn"])
                hg_s.append(pre["state"])
            xs = _ffn(xs, w2, tm=nb, final_g=fin)
        return xs.reshape(nb, 1, d), ssd_s, sc_s, cf_s, k_s, v_s, hg_s

    def prompt_group(hosted):
        xp = x_prompt.reshape(bp * lp, d)
        ssd_p, sc_p, cf_p, k_p, v_p, hg_p, ids = [], [], [], [], [], [], []

        def ffn(x, *args, **kw):
            if hosted is None:
                return _ffn(x, *args, **kw)
            qcol, cache_layer = hosted
            x, part = _ffn(x, *args, gate=(page_table, cache_kt, qcol, cache_layer, len(ids) * (steps // n_chunks)),
                           **kw)
            ids.append(part)
            return x

        for l, (w1, w, w2, fin) in enumerate(layer_w):
            xp = ffn(xp, w1, tm=TILE)
            if l % 2 == 0:
                xp3, st, xt, ut = _even_prompt(xp.reshape(bp, lp, d), w)
                xp = xp3.reshape(bp * lp, d)
                ssd_p.append(st)
                sc_p.append(xt)
                cf_p.append(ut)
                xp = ffn(xp, w2, tm=TILE, final_g=fin)
            else:
                q, kb, vt, kmean, ko, vo, ohg, hst = _odd_prompt_proj(xp.reshape(bp, lp, d), w, l)
                att = _moba_tiles(q, kb, vt, kmean[:, :, 0, :])
                k_p.append(jnp.swapaxes(ko, -1, -2))
                v_p.append(jnp.swapaxes(vo, -1, -2))
                hg_p.append(hst)
                xp = ffn(xp, w2, tm=TILE, final_g=fin, pre=(att.reshape(bp * lp, MOBA_WIDTH),
                                                            ohg.reshape(bp * lp, HG_WIDTH), w["woa"], w["woh"]))
        return (xp.reshape(bp, lp, d), ssd_p, sc_p, cf_p, k_p, v_p, hg_p), ids

    prompt_result = []

    def hosted_gate(qcol, cache_layer):
        out, ids = prompt_group((qcol, cache_layer))
        prompt_result.append(out)
        return jnp.concatenate(ids, axis=0)

    sample_out = sample_group(hosted_gate)
    prompt_out = prompt_result[0] if prompt_result else prompt_group(None)[0]
    y_p, ssd_p, sc_p, cf_p, k_p, v_p, hg_p = prompt_out
    y_s, ssd_s, sc_s, cf_s, k_s, v_s, hg_s = sample_out
    st = jnp.stack
    return (y_p, y_s, st(ssd_p), st(ssd_s), st(sc_p), st(sc_s), st(cf_p), st(cf_s),
            st(k_p), st(k_s), st(v_p), st(v_s), st(hg_p), st(hg_s))
```

```python
import functools
import math

import numpy as np
import jax
import jax.numpy as jnp
from jax import lax
from jax.experimental import pallas as pl
from jax.experimental.pallas import tpu as pltpu

F32 = jnp.float32
BF16 = jnp.bfloat16

SSD_HEADS = 16
SSD_HEAD_DIM = 64
SSD_INNER = SSD_HEADS * SSD_HEAD_DIM
SSD_GROUPS = 2
SSD_STATE = 64
SSD_CONV = 4
SSD_CONV_DIM = SSD_INNER + 2 * SSD_GROUPS * SSD_STATE
CF_CH = 512
CF_WIDTH = 31
MOBA_HEADS = 8
MOBA_HEAD_DIM = 64
MOBA_WIDTH = MOBA_HEADS * MOBA_HEAD_DIM
MOBA_BLOCK = 256
MOBA_TOPK = 3
ROPE_DIM = MOBA_HEAD_DIM // 4
ROPE_THETA = 500000.0
HG_HEADS = 4
HG_K = 128
HG_V = 128
HG_WIDTH = HG_HEADS * HG_K
PAGE_SIZE = 128
NORM_EPS = 1e-6
NEG = -1e30

LANES = 128
SSD_CHUNK = 128
TILE = 256
VT_ROWS = MOBA_HEAD_DIM + 16
VMEM_LIMIT = 56 * 1024 * 1024


def _sigmoid(x):
    return 1.0 / (1.0 + jnp.exp(-x))


def _silu(x):
    return x * _sigmoid(x)


def _softplus(x):
    return jnp.maximum(x, 0.0) + jnp.log1p(jnp.exp(-jnp.abs(x)))


def _rms(x, g):
    return x * lax.rsqrt(jnp.mean(x * x, axis=-1, keepdims=True) + NORM_EPS) * g


def _dot(a, b):
    return jnp.dot(a.astype(BF16), b.astype(BF16), preferred_element_type=F32)


def _dot_nt(a, b):
    return lax.dot_general(a.astype(BF16), b.astype(BF16), (((1,), (1,)), ((), ())),
                           preferred_element_type=F32)


def _split3(a):
    hi = a.astype(BF16)
    r = a - hi.astype(F32)
    mid = r.astype(BF16)
    lo = (r - mid.astype(F32)).astype(BF16)
    return hi, mid, lo


def _dot3_l(a, m):
    hi, mid, lo = _split3(a)
    return (jnp.dot(hi, m, preferred_element_type=F32) + jnp.dot(mid, m, preferred_element_type=F32)
            + jnp.dot(lo, m, preferred_element_type=F32))


def _dot3_r(m, parts):
    hi, mid, lo = parts
    return (jnp.dot(m, hi, preferred_element_type=F32) + jnp.dot(m, mid, preferred_element_type=F32)
            + jnp.dot(m, lo, preferred_element_type=F32))


def _const_spec(shape):
    n = len(shape)
    return pl.BlockSpec(shape, lambda *_: (0,) * n, pipeline_mode=pl.Buffered(1))


def _params(sem):
    return pltpu.CompilerParams(dimension_semantics=sem, vmem_limit_bytes=VMEM_LIMIT)


GATE_CHUNK = 16


def _gate_copies(pt_ref, ck_ref, kbuf, sem, layer, seq, chunk, slot):
    return [pltpu.make_async_copy(ck_ref.at[layer, pt_ref[seq, chunk * GATE_CHUNK + j]], kbuf.at[slot, j],
                                  sem.at[slot]) for j in range(GATE_CHUNK)]


def _gate_fetch(pt_ref, ck_ref, kbuf, sem, *, layer, seq0, n_chunks, n_steps):
    step = pl.program_id(0)
    slot = step % 2
    seq = seq0 + step // n_chunks
    c = step % n_chunks

    @pl.when(step == 0)
    def _():
        for cp in _gate_copies(pt_ref, ck_ref, kbuf, sem, layer, seq, c, slot):
            cp.start()

    nxt = step + 1

    @pl.when(nxt < n_steps)
    def _():
        for cp in _gate_copies(pt_ref, ck_ref, kbuf, sem, layer, seq0 + nxt // n_chunks, nxt % n_chunks, 1 - slot):
            cp.start()

    for cp in _gate_copies(pt_ref, ck_ref, kbuf, sem, layer, seq, c, slot):
        cp.wait()
    return slot, c


def _gate_accumulate(kbuf, q_ref, g_ref, slot, c):
    ppb = MOBA_BLOCK // PAGE_SIZE

    @pl.when(c == 0)
    def _():
        g_ref[...] = jnp.zeros(g_ref.shape, F32)

    lane = lax.broadcasted_iota(jnp.int32, (MOBA_HEADS, LANES), 1)
    sub = lax.broadcasted_iota(jnp.int32, (MOBA_HEADS, LANES), 0)
    g = g_ref[...]
    for jb in range(GATE_CHUNK // ppb):
        tile = jnp.zeros((MOBA_HEADS, LANES), F32)
        for h in range(MOBA_HEADS):
            ksum = kbuf[slot, jb * ppb, h]
            for r in range(1, ppb):
                ksum = ksum + kbuf[slot, jb * ppb + r, h]
            t = jnp.sum(ksum * q_ref[0, h], axis=0, keepdims=True)
            tile = jnp.where(sub == h, t, tile)
        col = jnp.sum(tile, axis=1, keepdims=True)
        g = jnp.where(lane == c * (GATE_CHUNK // ppb) + jb, g + col, g)
    g_ref[...] = g


def _gate_finish(g_ref, idx_ref, c, n_chunks):
    nblk = n_chunks * GATE_CHUNK // (MOBA_BLOCK // PAGE_SIZE)

    @pl.when(c == n_chunks - 1)
    def _():
        lane = lax.broadcasted_iota(jnp.int32, (MOBA_HEADS, LANES), 1)
        gcur = jnp.where(lane < nblk, g_ref[...] * (1.0 / MOBA_BLOCK), -jnp.inf)
        out = jnp.zeros((MOBA_HEADS, LANES), jnp.int32)
        for r in range(MOBA_TOPK):
            mx = jnp.max(gcur, axis=1, keepdims=True)
            first = jnp.min(jnp.where(gcur == mx, lane, LANES), axis=1, keepdims=True)
            out = jnp.where(lane == r, first, out)
            gcur = jnp.where(lane == first, -jnp.inf, gcur)
        idx_ref[0] = out


def _gate_scratch():
    return [pltpu.VMEM((2, GATE_CHUNK, MOBA_HEADS, MOBA_HEAD_DIM, PAGE_SIZE), F32), pltpu.SemaphoreType.DMA((2,)),
            pltpu.VMEM((MOBA_HEADS, LANES), F32)]


def _sample_gate_body(pt_ref, ck_ref, q_ref, idx_ref, kbuf, sem, g_ref, **job):
    slot, c = _gate_fetch(pt_ref, ck_ref, kbuf, sem, **job)
    _gate_accumulate(kbuf, q_ref, g_ref, slot, c)
    _gate_finish(g_ref, idx_ref, c, job["n_chunks"])


def _sample_gate(page_table, cache_kt, qcol, layer):
    nb, n_pages = page_table.shape
    n_chunks = n_pages // GATE_CHUNK
    job = dict(layer=layer, seq0=0, n_chunks=n_chunks, n_steps=nb * n_chunks)
    return pl.pallas_call(
        functools.partial(_sample_gate_body, **job),
        grid_spec=pltpu.PrefetchScalarGridSpec(
            num_scalar_prefetch=1, grid=(nb * n_chunks,),
            in_specs=[pl.BlockSpec(memory_space=pl.ANY),
                      pl.BlockSpec((1,) + qcol.shape[1:], lambda s, pt: (s // n_chunks, 0, 0, 0))],
            out_specs=pl.BlockSpec((1, MOBA_HEADS, LANES), lambda s, pt: (s // n_chunks, 0, 0)),
            scratch_shapes=_gate_scratch()),
        out_shape=jax.ShapeDtypeStruct((nb, MOBA_HEADS, LANES), jnp.int32),
        compiler_params=_params(("arbitrary",)), name="sample_gate")(page_table, cache_kt, qcol)


def _ffn_body(*refs, pre, final, job):
    it = iter(refs)
    if job is not None:
        pt_ref = next(it)
    x_ref = next(it)
    if pre:
        a_ref, b_ref, wa_ref, wb_ref = next(it), next(it), next(it), next(it)
    g_ref, wgu_ref, wd_ref = next(it), next(it), next(it)
    if final:
        fg_ref = next(it)
    if job is not None:
        ck_ref, q_ref = next(it), next(it)
    o_ref = next(it)
    if job is not None:
        idx_ref, kbuf, sem, gacc_ref = next(it), next(it), next(it), next(it)
        slot, c = _gate_fetch(pt_ref, ck_ref, kbuf, sem, **job)
        _gate_accumulate(kbuf, q_ref, gacc_ref, slot, c)
    d_ff = wd_ref.shape[1]
    x = x_ref[...]
    if pre:
        x = x + _dot(a_ref[...], wa_ref[...]) + _dot(b_ref[...], wb_ref[...])
    hb = _rms(x, g_ref[...]).astype(BF16)
    g = jnp.dot(hb, wgu_ref[0, :, :d_ff], preferred_element_type=F32)
    u = jnp.dot(hb, wgu_ref[0, :, d_ff:], preferred_element_type=F32)
    act = (_silu(g) * u).astype(BF16)
    y = x + 0.5 * jnp.dot(act, wd_ref[0], preferred_element_type=F32)
    if final:
        y = _rms(y, fg_ref[...])
    o_ref[...] = y
    if job is not None:
        _gate_finish(gacc_ref, idx_ref, c, job["n_chunks"])


def _ffn(x, w, *, tm, pre=None, final_g=None, gate=None):
    m, d = x.shape
    g, wgu, wd, layer = w
    steps = m // tm
    row = lambda c: pl.BlockSpec((tm, c), lambda i, *_: (i, 0))
    layer_spec = lambda a: pl.BlockSpec((1,) + a.shape[1:], lambda i, *_: (layer, 0, 0),
                                        pipeline_mode=pl.Buffered(1))
    args, specs = [x], [row(d)]
    if pre is not None:
        a, b, wa, wb = pre
        args += [a, b, wa, wb]
        specs += [row(a.shape[1]), row(b.shape[1]), _const_spec(wa.shape), _const_spec(wb.shape)]
    args += [g, wgu, wd]
    specs += [_const_spec(g.shape), layer_spec(wgu), layer_spec(wd)]
    if final_g is not None:
        args.append(final_g)
        specs.append(_const_spec(final_g.shape))
    out_shape = jax.ShapeDtypeStruct((m, d), F32)
    if gate is None:
        return pl.pallas_call(
            functools.partial(_ffn_body, pre=pre is not None, final=final_g is not None, job=None),
            grid=(steps,), in_specs=specs, out_specs=row(d), out_shape=out_shape,
            compiler_params=_params(("parallel",)), name="ffn")(*args)
    page_table, cache_kt, qcol, cache_layer, seq0 = gate
    n_chunks = page_table.shape[1] // GATE_CHUNK
    n_seq = steps // n_chunks
    assert n_seq * n_chunks == steps
    job = dict(layer=cache_layer, seq0=seq0, n_chunks=n_chunks, n_steps=steps)
    specs += [pl.BlockSpec(memory_space=pl.ANY),
              pl.BlockSpec((1,) + qcol.shape[1:], lambda i, pt: (seq0 + i // n_chunks, 0, 0, 0))]
    return pl.pallas_call(
        functools.partial(_ffn_body, pre=pre is not None, final=final_g is not None, job=job),
        grid_spec=pltpu.PrefetchScalarGridSpec(
            num_scalar_prefetch=1, grid=(steps,), in_specs=specs,
            out_specs=(row(d), pl.BlockSpec((1, MOBA_HEADS, LANES), lambda i, pt: (i // n_chunks, 0, 0))),
            scratch_shapes=_gate_scratch()),
        out_shape=(out_shape, jax.ShapeDtypeStruct((n_seq, MOBA_HEADS, LANES), jnp.int32)),
        compiler_params=_params(("arbitrary",)), name="ffn_gate")(page_table, *args, cache_kt, qcol)


def _even_prompt_body(x_ref, g_ref, wz_ref, wxbc_ref, wdt_ref, wga_ref, wgb_ref,
                      cw_ref, cb_ref, dtb_ref, alog_ref, dfull_ref, sn_ref, sel_ref,
                      cfw_ref, cfb_ref, cfg_ref, cfbeta_ref, woy_ref, woc_ref,
                      xo_ref, st_ref, xtail_ref, utail_ref,
                      xbuf, ubuf, s_ref, ybuf, *, nt):
    t = pl.program_id(1)
    T = TILE
    C = SSD_CHUNK

    @pl.when(t == 0)
    def _():
        xbuf[0:8, :] = jnp.zeros((8, SSD_CONV_DIM), F32)
        ubuf[0:32, :] = jnp.zeros((32, CF_CH), F32)
        s_ref[...] = jnp.zeros(s_ref.shape, F32)

    x = x_ref[0]
    hb = _rms(x, g_ref[...]).astype(BF16)
    z = jnp.dot(hb, wz_ref[...], preferred_element_type=F32)
    xbc = jnp.dot(hb, wxbc_ref[...], preferred_element_type=F32)
    dtr = jnp.dot(hb, wdt_ref[...], preferred_element_type=F32)
    ga = jnp.dot(hb, wga_ref[...], preferred_element_type=F32)
    gb = jnp.dot(hb, wgb_ref[...], preferred_element_type=F32)

    xbuf[8:8 + T, :] = xbc
    k0 = SSD_CONV - 1
    acc = cw_ref[k0:k0 + 1, :] * xbuf[5 + k0:5 + k0 + T, :]
    for k in range(k0):
        acc = acc + cw_ref[k:k + 1, :] * xbuf[5 + k:5 + k + T, :]
    tail8 = xbuf[T:T + 8, :]
    xtail_ref[0] = tail8
    xbuf[0:8, :] = tail8
    xc = _silu(acc + cb_ref[...])
    xs = xc[:, 0:SSD_INNER]
    bm = xc[:, SSD_INNER:SSD_INNER + LANES]
    cm = xc[:, SSD_INNER + LANES:SSD_INNER + 2 * LANES]
    dt = _softplus(dtr + dtb_ref[...])
    dta = dt * (-jnp.exp(alog_ref[...]))
    sel = sel_ref[...]

    ri = lax.broadcasted_iota(jnp.int32, (C, C), 0)
    ci = lax.broadcasted_iota(jnp.int32, (C, C), 1)
    tri = ri >= ci
    trib = tri.astype(BF16)
    low = ci < SSD_HEAD_DIM

    for c in range(T // C):
        r0 = c * C
        xs_c, bm_c, cm_c = xs[r0:r0 + C], bm[r0:r0 + C], cm[r0:r0 + C]
        dt_c = dt[r0:r0 + C]
        cum = _dot3_r(trib, _split3(dta[r0:r0 + C]))
        cum_t = cum.T
        dt_t = dt_c.T
        cum_last = cum[C - 1:C, :]
        ecum_full = _dot3_l(jnp.exp(cum), sel)
        tail_full = _dot3_l(jnp.exp(cum_last - cum) * dt_c, sel)
        dlast_full = _dot3_l(jnp.broadcast_to(jnp.exp(cum_last), (8, LANES)), sel)[0:1]
        bm_t = bm_c.T
        cmb = cm_c.astype(BF16)
        for grp in range(SSD_GROUPS):
            bm_tg = jnp.where((ri // SSD_STATE) == grp, bm_t, 0.0).astype(BF16)
            gmat = jnp.dot(cmb, bm_tg, preferred_element_type=F32)
            pairs = SSD_HEADS // SSD_GROUPS // 2
            for pp in range(pairs):
                p = grp * pairs + pp
                lanes = slice(p * LANES, (p + 1) * LANES)
                xp = xs_c[:, lanes]
                xpb = xp.astype(BF16)
                ys = []
                for e in range(2):
                    h = 2 * p + e
                    seg = cum[:, h:h + 1] - cum_t[h:h + 1, :]
                    wm = gmat * jnp.exp(jnp.where(tri, seg, NEG)) * dt_t[h:h + 1, :]
                    ys.append(jnp.dot(wm.astype(BF16), xpb, preferred_element_type=F32))
                yp = jnp.where(low, ys[0], ys[1])
                sp = s_ref[p]
                yp = yp + jnp.dot(cmb, sp.astype(BF16), preferred_element_type=F32) * ecum_full[:, lanes]
                s_ref[p] = dlast_full[:, lanes] * sp + jnp.dot(
                    bm_tg, (xp * tail_full[:, lanes]).astype(BF16), preferred_element_type=F32)
                ybuf[r0:r0 + C, lanes] = yp

    @pl.when(t == nt - 1)
    def _():
        for p in range(SSD_HEADS // 2):
            st_ref[0, p] = s_ref[p].T

    y = (ybuf[...] + dfull_ref[...] * xs) * _silu(z)
    gw = SSD_INNER // SSD_GROUPS
    yn = []
    for grp in range(SSD_GROUPS):
        yg = y[:, grp * gw:(grp + 1) * gw]
        yn.append(_rms(yg, sn_ref[:, grp * gw:(grp + 1) * gw]).astype(BF16))

    u = ga * _sigmoid(gb)
    ubuf[32:32 + T, :] = u
    base = 32 - (CF_WIDTH - 1)
    cacc = None
    for r in range(8):
        offs = [o for o in range(base, base + CF_WIDTH) if o % 8 == r]
        if not offs:
            continue
        ur = ubuf[r:max(offs) + T, :]
        part = None
        for o in offs:
            term = cfw_ref[o - base:o - base + 1, :] * ur[o - r:o - r + T]
            part = term if part is None else part + term
        cacc = part if cacc is None else cacc + part
    tail32 = ubuf[T:T + 32, :]
    utail_ref[0] = tail32
    ubuf[0:32, :] = tail32
    c32 = cacc + cfb_ref[...]
    mu = jnp.mean(c32, axis=-1, keepdims=True)
    var = jnp.mean(jnp.square(c32 - mu), axis=-1, keepdims=True)
    c32 = _silu((c32 - mu) * lax.rsqrt(var + NORM_EPS) * cfg_ref[...] + cfbeta_ref[...])

    out = jnp.dot(c32.astype(BF16), woc_ref[...], preferred_element_type=F32)
    for grp in range(SSD_GROUPS):
        out = out + jnp.dot(yn[grp], woy_ref[grp * gw:(grp + 1) * gw, :], preferred_element_type=F32)
    xo_ref[0] = x + out


def _even_prompt(x, w):
    b, l, d = x.shape
    nt = l // TILE
    consts = [w[k] for k in ("g", "wz", "wxbc", "wdt", "wga", "wgb", "cw", "cb", "dtb", "alog", "dfull", "sn",
                             "sel", "cfw", "cfb", "cfg", "cfbeta", "woy", "woc")]
    out_shape = (jax.ShapeDtypeStruct((b, l, d), F32),
                 jax.ShapeDtypeStruct((b, SSD_HEADS // 2, LANES, LANES), F32),
                 jax.ShapeDtypeStruct((b, 8, SSD_CONV_DIM), F32),
                 jax.ShapeDtypeStruct((b, 32, CF_CH), F32))
    out_specs = (pl.BlockSpec((1, TILE, d), lambda i, t: (i, t, 0)),
                 pl.BlockSpec((1, SSD_HEADS // 2, LANES, LANES), lambda i, t: (i, 0, 0, 0)),
                 pl.BlockSpec((1, 8, SSD_CONV_DIM), lambda i, t: (i, 0, 0)),
                 pl.BlockSpec((1, 32, CF_CH), lambda i, t: (i, 0, 0)))
    xo, st, xtail, utail = pl.pallas_call(
        functools.partial(_even_prompt_body, nt=nt),
        grid=(b, nt),
        in_specs=[pl.BlockSpec((1, TILE, d), lambda i, t: (i, t, 0))] + [_const_spec(c.shape) for c in consts],
        out_specs=out_specs, out_shape=out_shape,
        scratch_shapes=[pltpu.VMEM((TILE + 8, SSD_CONV_DIM), F32), pltpu.VMEM((TILE + 32, CF_CH), F32),
                        pltpu.VMEM((SSD_HEADS // 2, LANES, LANES), F32), pltpu.VMEM((TILE, SSD_INNER), F32)],
        compiler_params=_params(("arbitrary", "arbitrary")), name="even_prompt")(x, *consts)
    half = SSD_HEADS // 2 // SSD_GROUPS
    st = st.reshape(b, SSD_HEADS // 2, 2, SSD_HEAD_DIM, SSD_GROUPS, SSD_STATE)
    state = jnp.concatenate([st[:, grp * half:(grp + 1) * half, :, :, grp, :] for grp in range(SSD_GROUPS)], axis=1)
    state = state.reshape(b, SSD_HEADS, SSD_HEAD_DIM, SSD_STATE)
    return xo, state, xtail[:, 8 - (SSD_CONV - 1):], utail[:, 32 - (CF_WIDTH - 1):]


def _hg_levels(T):
    t = np.arange(T)
    le = (t[None, :] <= t[:, None]).astype(np.float32)
    ds, ms = [], []
    m = 1
    while m < T:
        rb = (t // (2 * m)) * 2 * m + m - 1
        if m < 8:
            ds.append(le - (t[None, :] <= rb[:, None]).astype(np.float32))
        same = (t[:, None] // (2 * m)) == (t[None, :] // (2 * m))
        ms.append((same & ((t[:, None] % (2 * m)) >= m) & ((t[None, :] % (2 * m)) < m)).astype(np.float32))
        m *= 2
    return (jnp.asarray(le, BF16), jnp.asarray(np.stack(ds), BF16), jnp.asarray(np.stack(ms), BF16))


def _rope_tables(pos):
    half = ROPE_DIM // 2
    inv = ROPE_THETA ** (-np.arange(half, dtype=np.float64) / half)
    ang = pos.astype(np.float64)[:, None] * inv[None, :]
    cos, sin = np.cos(ang), np.sin(ang)
    n = pos.shape[0]
    one = np.ones((n, MOBA_HEAD_DIM - ROPE_DIM))
    zero = np.zeros((n, MOBA_HEAD_DIM - ROPE_DIM))
    zh = np.zeros((n, half))
    c = np.concatenate([cos, cos, one], axis=1)
    s1 = np.concatenate([-sin, zh, zero], axis=1)
    s2 = np.concatenate([zh, sin, zero], axis=1)
    return tuple(jnp.asarray(np.concatenate([a, a], axis=1), F32) for a in (c, s1, s2))


def _rope(x, c, s1, s2):
    outs = []
    for j in range(x.shape[1] // LANES):
        blk = x[:, j * LANES:(j + 1) * LANES]
        outs.append(blk * c + pltpu.roll(blk, LANES - ROPE_DIM // 2, 1) * s1 + pltpu.roll(blk, ROPE_DIM // 2, 1) * s2)
    return outs


def _hg_lower_bound(lb_ref, layer):
    a = lb_ref[...]
    mx = jnp.max(a, axis=0, keepdims=True)
    e = jnp.exp(a - mx)
    return jnp.sum(e[1:layer + 1], axis=0, keepdims=True) / jnp.sum(e, axis=0, keepdims=True)


def _odd_prompt_body(x_ref, g_ref, win_ref, c_ref, s1_ref, s2_ref, lb_ref, gn_ref, tril_ref, dm_ref, mm_ref,
                     q_ref, kb_ref, vt_ref, km_ref, ko_ref, vo_ref, o_ref, hs_ref,
                     st_ref, *, nt, layer):
    t = pl.program_id(1)
    T = TILE
    W = MOBA_WIDTH

    @pl.when(t == 0)
    def _():
        st_ref[...] = jnp.zeros(st_ref.shape, F32)

    x = x_ref[0]
    hb = _rms(x, g_ref[...]).astype(BF16)
    proj = jnp.dot(hb, win_ref[...], preferred_element_type=F32)

    c, s1, s2 = c_ref[...], s1_ref[...], s2_ref[...]
    qb = _rope(proj[:, 0:W], c, s1, s2)
    kb = _rope(proj[:, W:2 * W], c, s1, s2)
    v = proj[:, 2 * W:3 * W]
    for j in range(W // LANES):
        lanes = slice(j * LANES, (j + 1) * LANES)
        q_ref[0, :, lanes] = qb[j]
        kb_ref[0, :, lanes] = kb[j].astype(BF16)
        km_ref[0, 0, :, lanes] = jnp.mean(kb[j], axis=0, keepdims=True)
        kt = kb[j].T
        vt = v[:, lanes].T
        for e in range(2):
            ko_ref[0, 2 * j + e] = kt[e * 64:(e + 1) * 64, :]
            vo_ref[0, 2 * j + e] = vt[e * 64:(e + 1) * 64, :]
            r0 = (2 * j + e) * VT_ROWS
            vt_ref[0, r0:r0 + 64, :] = vt[e * 64:(e + 1) * 64, :].astype(BF16)
            vt_ref[0, r0 + 64:r0 + VT_ROWS, :] = (
                lax.broadcasted_iota(jnp.int32, (VT_ROWS - 64, T), 0) == 0).astype(BF16)

    lb = _hg_lower_bound(lb_ref, layer)
    hq = _silu(proj[:, 3 * W:3 * W + HG_WIDTH])
    f = lb + (1.0 - lb) * _sigmoid(proj[:, 3 * W + HG_WIDTH:3 * W + 2 * HG_WIDTH])
    hv = proj[:, 3 * W + 2 * HG_WIDTH:3 * W + 3 * HG_WIDTH]
    hgate = proj[:, 3 * W + 3 * HG_WIDTH:3 * W + 4 * HG_WIDTH]
    kk = 1.0 - f
    lf3 = _split3(jnp.log(f))
    cum = _dot3_r(tril_ref[...], lf3)
    nlev = mm_ref.shape[0]
    ri = lax.broadcasted_iota(jnp.int32, (T, T), 0)
    ci = lax.broadcasted_iota(jnp.int32, (T, T), 1)
    eye = ri == ci
    amat = []
    for h in range(HG_HEADS):
        lanes = slice(h * HG_K, (h + 1) * HG_K)
        diag = jnp.sum(hq[:, lanes] * kk[:, lanes], axis=1, keepdims=True)
        amat.append(jnp.where(eye, diag, 0.0))
    for lev in range(nlev):
        m = 1 << lev
        if lev < dm_ref.shape[0]:
            d = _dot3_r(dm_ref[lev], lf3)
        else:
            refs = [jnp.broadcast_to(cum[r0 + m - 1:r0 + m, :], (2 * m, cum.shape[1])) for r0 in range(0, T, 2 * m)]
            d = cum - (refs[0] if len(refs) == 1 else jnp.concatenate(refs, axis=0))
        e_all = jnp.exp(-jnp.abs(d))
        mk = mm_ref[lev].astype(F32)
        for h in range(HG_HEADS):
            lanes = slice(h * HG_K, (h + 1) * HG_K)
            a_m = _dot_nt(hq[:, lanes] * e_all[:, lanes], kk[:, lanes] * e_all[:, lanes])
            amat[h] = amat[h] + a_m * mk
    cum_last = cum[T - 1:T, :]
    ecum = jnp.exp(cum)
    ktail = kk * jnp.exp(cum_last - cum)
    elast = jnp.exp(cum_last)
    for h in range(HG_HEADS):
        lanes = slice(h * HG_K, (h + 1) * HG_K)
        vh = hv[:, lanes]
        vhb = vh.astype(BF16)
        st = st_ref[h]
        o = jnp.dot(amat[h].astype(BF16), vhb, preferred_element_type=F32)
        o = o + _dot_nt(hq[:, lanes] * ecum[:, lanes], st)
        st_new = elast[:, lanes] * st + jnp.dot(vh.T.astype(BF16), ktail[:, lanes].astype(BF16),
                                                preferred_element_type=F32)
        st_ref[h] = st_new
        o_ref[0, :, lanes] = _rms(o, gn_ref[:, lanes]) * _silu(hgate[:, lanes])

    @pl.when(t == nt - 1)
    def _():
        for h in range(HG_HEADS):
            hs_ref[0, h] = st_ref[h].T


def _odd_prompt_proj(x, w, layer):
    b, l, d = x.shape
    nt = l // TILE
    c, s1, s2 = _rope_tables(np.arange(l))
    tril, dm, mm = _hg_levels(TILE)
    consts_a = [w["g"], w["win"]]
    consts_b = [w["lb"], w["gn"], tril, dm, mm]
    tab = pl.BlockSpec((TILE, LANES), lambda i, t: (t, 0))
    W = MOBA_WIDTH
    out_shape = (jax.ShapeDtypeStruct((b, l, W), F32),
                 jax.ShapeDtypeStruct((b, l, W), BF16),
                 jax.ShapeDtypeStruct((b, MOBA_HEADS * VT_ROWS, l), BF16),
                 jax.ShapeDtypeStruct((b, nt, 1, W), F32),
                 jax.ShapeDtypeStruct((b, MOBA_HEADS, MOBA_HEAD_DIM, l), F32),
                 jax.ShapeDtypeStruct((b, MOBA_HEADS, MOBA_HEAD_DIM, l), F32),
                 jax.ShapeDtypeStruct((b, l, HG_WIDTH), F32),
                 jax.ShapeDtypeStruct((b, HG_HEADS, HG_K, HG_V), F32))
    out_specs = (pl.BlockSpec((1, TILE, W), lambda i, t: (i, t, 0)),
                 pl.BlockSpec((1, TILE, W), lambda i, t: (i, t, 0)),
                 pl.BlockSpec((1, MOBA_HEADS * VT_ROWS, TILE), lambda i, t: (i, 0, t)),
                 pl.BlockSpec((1, 1, 1, W), lambda i, t: (i, t, 0, 0)),
                 pl.BlockSpec((1, MOBA_HEADS, MOBA_HEAD_DIM, TILE), lambda i, t: (i, 0, 0, t)),
                 pl.BlockSpec((1, MOBA_HEADS, MOBA_HEAD_DIM, TILE), lambda i, t: (i, 0, 0, t)),
                 pl.BlockSpec((1, TILE, HG_WIDTH), lambda i, t: (i, t, 0)),
                 pl.BlockSpec((1, HG_HEADS, HG_K, HG_V), lambda i, t: (i, 0, 0, 0)))
    return pl.pallas_call(
        functools.partial(_odd_prompt_body, nt=nt, layer=layer),
        grid=(b, nt),
        in_specs=([pl.BlockSpec((1, TILE, d), lambda i, t: (i, t, 0))] + [_const_spec(a.shape) for a in consts_a]
                  + [tab, tab, tab] + [_const_spec(a.shape) for a in consts_b]),
        out_specs=out_specs, out_shape=out_shape,
        scratch_shapes=[pltpu.VMEM((HG_HEADS, HG_V, HG_K), F32)],
        compiler_params=_params(("arbitrary", "arbitrary")), name="odd_prompt_proj")(
            x, *consts_a, c, s1, s2, *consts_b)


def _moba_tile_body(q_ref, k_ref, vt_ref, km_ref, o_ref, sel_ref, s_ref, cm_ref, p_ref):
    j = pl.program_id(2)
    T = TILE
    Q = 2 * T
    hd = MOBA_HEAD_DIM
    q = q_ref[0]
    km = km_ref[0, 0]
    nbp = km.shape[0]
    lane = lax.broadcasted_iota(jnp.int32, (Q, LANES), 1)
    qes = [jnp.where((lane // hd) == e, q, 0.0) for e in range(2)]
    qs = [(qe * (hd ** -0.5 * math.log2(math.e))).astype(BF16) for qe in qes]
    last_blk = k_ref.shape[1] // T - 1

    def scores(b, slot, keep):
        off = pl.multiple_of(jnp.minimum(b, last_blk) * T, T)
        kk = k_ref[0, pl.ds(off, T), :]
        for e in range(2):
            s = jnp.where(keep(e), _dot_nt(kk, qs[e]), NEG)
            s_ref[slot, e] = s
            cm_ref[slot, e] = jnp.max(s, axis=0, keepdims=True)

    def selected(b):
        return lambda e: sel_ref[e, pl.ds(b, 1), :] > 0.5

    def pv(b, slot, e):
        off = pl.multiple_of(jnp.clip(b, 0, last_blk) * T, T)
        return jnp.dot(vt_ref[0, e * VT_ROWS:(e + 1) * VT_ROWS, pl.ds(off, T)], p_ref[slot, e],
                       preferred_element_type=F32)

    def softmax_step(slot, e, m, acc, prev):
        m_new = jnp.maximum(m, cm_ref[slot, e])
        p_ref[slot, e] = jnp.exp2((s_ref[slot, e] - m_new).astype(BF16))
        return m_new, jnp.exp2(m - m_new) * (acc + prev)

    p_ref[...] = jnp.zeros(p_ref.shape, BF16)

    blk = lax.broadcasted_iota(jnp.int32, (nbp, Q), 0)
    col = lax.broadcasted_iota(jnp.int32, (nbp, Q), 1)
    elig = blk < 2 * j + (col >= T).astype(jnp.int32)
    kh = km.astype(BF16)
    kl = (km - kh.astype(F32)).astype(BF16)
    for e in range(2):
        qh = qes[e].astype(BF16)
        ql = (qes[e] - qh.astype(F32)).astype(BF16)
        gate = _dot_nt(kh, qh) + _dot_nt(kh, ql) + _dot_nt(kl, qh)
        gcur = jnp.where(elig, gate, -jnp.inf)
        selm = jnp.zeros((nbp, Q), F32)
        for _ in range(MOBA_TOPK):
            mx = jnp.max(gcur, axis=0, keepdims=True)
            first = jnp.min(jnp.where(gcur == mx, blk, nbp), axis=0, keepdims=True)
            hit = blk == first
            selm = jnp.where(hit & elig, 1.0, selm)
            gcur = jnp.where(hit, -jnp.inf, gcur)
        sel_ref[e] = selm

    ri = lax.broadcasted_iota(jnp.int32, (T, Q), 0)
    ci = lax.broadcasted_iota(jnp.int32, (T, Q), 1)
    carry = []
    for e in range(2):
        carry += [jnp.full((1, Q), NEG, F32), jnp.zeros((VT_ROWS, Q), F32)]

    own_a = (ci < T) & (ri <= ci)
    scores(2 * j, 0, lambda e: own_a | (sel_ref[e, pl.ds(2 * j, 1), :] > 0.5))
    scores(2 * j + 1, 1, lambda e: (ci >= T) & (ri <= ci - T))
    for e in range(2):
        carry[2 * e], carry[2 * e + 1] = softmax_step(0, e, carry[2 * e], carry[2 * e + 1], 0.0)
    scores(0, 0, selected(0))
    for e in range(2):
        carry[2 * e], carry[2 * e + 1] = softmax_step(1, e, carry[2 * e], carry[2 * e + 1], pv(2 * j, 0, e))

    def body(u, carry):
        carry = list(carry)
        scores(2 * u + 1, 1, selected(2 * u + 1))
        for e in range(2):
            prev = pv(jnp.where(u == 0, 2 * j + 1, 2 * u - 1), 1, e)
            carry[2 * e], carry[2 * e + 1] = softmax_step(0, e, carry[2 * e], carry[2 * e + 1], prev)
        scores(2 * u + 2, 0, selected(2 * u + 2))
        for e in range(2):
            carry[2 * e], carry[2 * e + 1] = softmax_step(1, e, carry[2 * e], carry[2 * e + 1], pv(2 * u, 0, e))
        return tuple(carry)

    fin = lax.fori_loop(0, j, body, tuple(carry))
    outs = []
    for e in range(2):
        tot = fin[2 * e + 1] + pv(jnp.where(j == 0, 1, 2 * j - 1), 1, e)
        outs.append(tot[0:hd] / tot[hd:hd + 1])
    o_ref[0] = jnp.concatenate(outs, axis=0).T


def _moba_tiles(q, kb, vt, kmean):
    b, l, w = q.shape
    tq = 2 * TILE
    assert l % tq == 0
    npair = w // LANES
    nb = kmean.shape[1]
    km = kmean.reshape(b, nb, npair, LANES).transpose(0, 2, 1, 3)
    nbp = -(-nb // 8) * 8
    km = jnp.pad(km, ((0, 0), (0, 0), (0, nbp - nb), (0, 0)))
    return pl.pallas_call(
        _moba_tile_body,
        grid=(b, npair, l // tq),
        in_specs=[pl.BlockSpec((1, tq, LANES), lambda i, p, t: (i, t, p)),
                  pl.BlockSpec((1, l, LANES), lambda i, p, t: (i, 0, p)),
                  pl.BlockSpec((1, 2 * VT_ROWS, l), lambda i, p, t: (i, p, 0)),
                  pl.BlockSpec((1, 1, nbp, LANES), lambda i, p, t: (i, p, 0, 0))],
        out_specs=pl.BlockSpec((1, tq, LANES), lambda i, p, t: (i, t, p)),
        out_shape=jax.ShapeDtypeStruct((b, l, w), F32),
        scratch_shapes=[pltpu.VMEM((2, nbp, tq), F32), pltpu.VMEM((2, 2, TILE, tq), F32),
                        pltpu.VMEM((2, 2, 1, tq), F32), pltpu.VMEM((2, 2, TILE, tq), BF16)],
        compiler_params=_params(("arbitrary", "arbitrary", "arbitrary")), name="moba_prompt")(q, kb, vt, km)


def _even_sample_proj_body(x_ref, g_ref, wz_ref, wxbc_ref, wdt_ref, wga_ref, wgb_ref, cw_ref, cb_ref, dtb_ref,
                           xbuf_ref, cfw_ref, cfb_ref, cfg_ref, cfbeta_ref, ubuf_ref,
                           z_ref, xc_ref, dt_ref, xnew_ref, c_ref, unew_ref):
    x = x_ref[...]
    hb = _rms(x, g_ref[...]).astype(BF16)
    z_ref[...] = jnp.dot(hb, wz_ref[...], preferred_element_type=F32)
    xbc = jnp.dot(hb, wxbc_ref[...], preferred_element_type=F32)
    dtr = jnp.dot(hb, wdt_ref[...], preferred_element_type=F32)
    ga = jnp.dot(hb, wga_ref[...], preferred_element_type=F32)
    gb = jnp.dot(hb, wgb_ref[...], preferred_element_type=F32)
    k1 = SSD_CONV - 1
    acc = cw_ref[k1:k1 + 1, :] * xbc
    for k in range(k1):
        acc = acc + cw_ref[k:k + 1, :] * xbuf_ref[k]
        if k > 0:
            xnew_ref[k - 1] = xbuf_ref[k]
    xnew_ref[k1 - 1] = xbc
    xc_ref[...] = _silu(acc + cb_ref[...])
    dt_ref[...] = _softplus(dtr + dtb_ref[...])
    u = ga * _sigmoid(gb)
    k2 = CF_WIDTH - 1
    cacc = cfw_ref[k2:k2 + 1, :] * u
    for k in range(k2):
        cacc = cacc + cfw_ref[k:k + 1, :] * ubuf_ref[k]
        if k > 0:
            unew_ref[k - 1] = ubuf_ref[k]
    unew_ref[k2 - 1] = u
    c32 = cacc + cfb_ref[...]
    mu = jnp.mean(c32, axis=-1, keepdims=True)
    var = jnp.mean(jnp.square(c32 - mu), axis=-1, keepdims=True)
    c_ref[...] = _silu((c32 - mu) * lax.rsqrt(var + NORM_EPS) * cfg_ref[...] + cfbeta_ref[...])


STEP_ROWS = 128
HG_STEP_ROWS = 32


def _ssd_step_body(s_ref, xb_ref, dt_ref, alog_ref, b_ref, c_ref, so_ref, yt_ref):
    rows = s_ref.shape[0]
    dt = dt_ref[...]
    decay = jnp.exp(dt * (-jnp.exp(alog_ref[...])))
    bdt = b_ref[...] * dt
    cc = c_ref[...]
    for r in range(rows):
        s_new = decay[r:r + 1, :] * s_ref[r] + xb_ref[r] * bdt[r:r + 1, :]
        so_ref[r] = s_new
        yt_ref[:, r:r + 1] = jnp.sum(s_new * cc[r:r + 1, :], axis=1, keepdims=True)


def _even_sample_out_body(x_ref, y_ref, xs_ref, z_ref, dfull_ref, sn_ref, c_ref, woy_ref, woc_ref, o_ref):
    y = (y_ref[...] + dfull_ref[...] * xs_ref[...]) * _silu(z_ref[...])
    gw = SSD_INNER // SSD_GROUPS
    out = _dot(c_ref[...], woc_ref[...])
    for grp in range(SSD_GROUPS):
        lanes = slice(grp * gw, (grp + 1) * gw)
        out = out + _dot(_rms(y[:, lanes], sn_ref[:, lanes]), woy_ref[lanes, :])
    o_ref[...] = x_ref[...] + out


def _call(body, out_shape, *args, name):
    return pl.pallas_call(body, out_shape=out_shape, compiler_params=_params(None), name=name)(*args)


def _even_sample(x, w, s_ssd, buf_ssd, buf_cf):
    nb = x.shape[0]
    f = lambda *s: jax.ShapeDtypeStruct(s, F32)
    z, xc, dt, xnew, c, unew = _call(
        _even_sample_proj_body,
        (f(nb, SSD_INNER), f(nb, SSD_CONV_DIM), f(nb, LANES), f(SSD_CONV - 1, nb, SSD_CONV_DIM), f(nb, CF_CH),
         f(CF_WIDTH - 1, nb, CF_CH)),
        x, w["g"], w["wz"], w["wxbc"], w["wdt"], w["wga"], w["wgb"], w["cw"], w["cb"], w["dtb"],
        jnp.swapaxes(buf_ssd, 0, 1), w["cfw"], w["cfb"], w["cfg"], w["cfbeta"], jnp.swapaxes(buf_cf, 0, 1),
        name="even_sample_proj")
    xs = xc[:, :SSD_INNER]
    rows = nb * SSD_HEADS
    rep_heads = SSD_HEADS // SSD_GROUPS
    grp = lambda a: jnp.repeat(a.reshape(nb, SSD_GROUPS, SSD_STATE), rep_heads, axis=1).reshape(rows, SSD_STATE)
    bm = grp(xc[:, SSD_INNER:SSD_INNER + SSD_GROUPS * SSD_STATE])
    cm = grp(xc[:, SSD_INNER + SSD_GROUPS * SSD_STATE:])
    rb = STEP_ROWS
    assert rows % rb == 0
    rowspec = pl.BlockSpec((rb, SSD_STATE), lambda i: (i, 0))
    colspec = pl.BlockSpec((SSD_HEAD_DIM, rb), lambda i: (0, i))
    sspec = pl.BlockSpec((rb, SSD_HEAD_DIM, SSD_STATE), lambda i: (i, 0, 0))
    per_row = lambda a: jnp.broadcast_to(a.reshape(rows, 1), (rows, SSD_STATE))
    s_new, yt = pl.pallas_call(
        _ssd_step_body, grid=(rows // rb,),
        in_specs=[sspec, sspec, rowspec, rowspec, rowspec, rowspec],
        out_specs=(sspec, colspec),
        out_shape=(f(rows, SSD_HEAD_DIM, SSD_STATE), f(SSD_HEAD_DIM, rows)),
        compiler_params=_params(("parallel",)), name="ssd_step")(
            s_ssd.reshape(rows, SSD_HEAD_DIM, SSD_STATE),
            jnp.broadcast_to(xs.reshape(rows, SSD_HEAD_DIM, 1), (rows, SSD_HEAD_DIM, SSD_STATE)),
            per_row(dt[:, :SSD_HEADS]), per_row(jnp.tile(w["alog"][0, :SSD_HEADS], nb)), bm, cm)
    xo = _call(_even_sample_out_body, f(nb, x.shape[1]),
               x, yt.T.reshape(nb, SSD_INNER), xs, z, w["dfull"], w["sn"], c, w["woy"], w["woc"],
               name="even_sample_out")
    return (xo, s_new.reshape(s_ssd.shape), jnp.swapaxes(xnew, 0, 1), jnp.swapaxes(unew, 0, 1))


def _odd_sample_proj_body(x_ref, g_ref, win_ref, c_ref, s1_ref, s2_ref, lb_ref,
                          q_ref, k_ref, v_ref, hq_ref, f_ref, hv_ref, hg_ref, *, layer):
    W = MOBA_WIDTH
    hb = _rms(x_ref[...], g_ref[...]).astype(BF16)
    proj = jnp.dot(hb, win_ref[...], preferred_element_type=F32)
    c, s1, s2 = c_ref[...], s1_ref[...], s2_ref[...]
    qb = _rope(proj[:, 0:W], c, s1, s2)
    kb = _rope(proj[:, W:2 * W], c, s1, s2)
    for j in range(W // LANES):
        q_ref[:, j * LANES:(j + 1) * LANES] = qb[j]
        k_ref[:, j * LANES:(j + 1) * LANES] = kb[j]
    v_ref[...] = proj[:, 2 * W:3 * W]
    lb = _hg_lower_bound(lb_ref, layer)
    hq_ref[...] = _silu(proj[:, 3 * W:3 * W + HG_WIDTH])
    f_ref[...] = lb + (1.0 - lb) * _sigmoid(proj[:, 3 * W + HG_WIDTH:3 * W + 2 * HG_WIDTH])
    hv_ref[...] = proj[:, 3 * W + 2 * HG_WIDTH:3 * W + 3 * HG_WIDTH]
    hg_ref[...] = proj[:, 3 * W + 3 * HG_WIDTH:3 * W + 4 * HG_WIDTH]


def _hg_step_body(s_ref, qt_ref, ft_ref, v_ref, so_ref, o_ref):
    qt, ft, vv = qt_ref[0], ft_ref[0], v_ref[...]
    for r in range(s_ref.shape[0]):
        fc = ft[:, r:r + 1]
        s_new = fc * s_ref[r] + (1.0 - fc) * vv[r:r + 1, :]
        so_ref[r] = s_new
        o_ref[r:r + 1, :] = jnp.sum(s_new * qt[:, r:r + 1], axis=0, keepdims=True)


def _attn_copies(idx_ref, pt_ref, ck_ref, cv_ref, kbuf, vbuf, sem, layer, seq, slot):
    ppb = MOBA_BLOCK // PAGE_SIZE
    cps = []
    for h in range(MOBA_HEADS):
        for r in range(MOBA_TOPK):
            blk = idx_ref[seq, h * MOBA_TOPK + r]
            for pg in range(ppb):
                phys = pt_ref[seq, blk * ppb + pg]
                j = r * ppb + pg
                cps.append(pltpu.make_async_copy(ck_ref.at[layer, phys, h], kbuf.at[slot, h, j], sem.at[0, slot]))
                cps.append(pltpu.make_async_copy(cv_ref.at[layer, phys, h], vbuf.at[slot, h, j], sem.at[1, slot]))
    return cps


def _sample_attn_body(idx_ref, pt_ref, ck_ref, cv_ref, q_ref, kn_ref, vn_ref, o_ref, kbuf, vbuf, sem, *,
                      layer, n_seq):
    b = pl.program_id(0)
    slot = b % 2
    scale = MOBA_HEAD_DIM ** -0.5
    nsel = MOBA_TOPK * (MOBA_BLOCK // PAGE_SIZE)

    @pl.when(b == 0)
    def _():
        for cp in _attn_copies(idx_ref, pt_ref, ck_ref, cv_ref, kbuf, vbuf, sem, layer, b, slot):
            cp.start()

    @pl.when(b + 1 < n_seq)
    def _():
        for cp in _attn_copies(idx_ref, pt_ref, ck_ref, cv_ref, kbuf, vbuf, sem, layer, b + 1, 1 - slot):
            cp.start()

    for cp in _attn_copies(idx_ref, pt_ref, ck_ref, cv_ref, kbuf, vbuf, sem, layer, b, slot):
        cp.wait()

    for h in range(MOBA_HEADS):
        qc = q_ref[0, h]
        s_self = jnp.sum(qc * kn_ref[0, h], axis=0, keepdims=True) * scale
        ss = [jnp.sum(kbuf[slot, h, j] * qc, axis=0, keepdims=True) * scale for j in range(nsel)]
        m = s_self
        for s in ss:
            m = jnp.maximum(m, jnp.max(s, axis=1, keepdims=True))
        p_self = jnp.exp(s_self - m)
        l = p_self
        acc = jnp.zeros((MOBA_HEAD_DIM, LANES), F32)
        for j in range(nsel):
            p = jnp.exp(ss[j] - m)
            l = l + jnp.sum(p, axis=1, keepdims=True)
            acc = acc + vbuf[slot, h, j] * p
        o_ref[0, h] = (vn_ref[0, h] * p_self + jnp.sum(acc, axis=1, keepdims=True)) / l


def _odd_sample_out_body(x_ref, att_ref, o_ref, hg_ref, gn_ref, woa_ref, woh_ref, xo_ref):
    out = _dot(att_ref[...], woa_ref[...])
    o = o_ref[...]
    gated = []
    for h in range(HG_HEADS):
        lanes = slice(h * HG_V, (h + 1) * HG_V)
        gated.append(_rms(o[:, lanes], gn_ref[:, lanes]) * _silu(hg_ref[:, lanes]))
    out = out + _dot(jnp.concatenate(gated, axis=1), woh_ref[...])
    xo_ref[...] = x_ref[...] + out


def _odd_sample_pre(x, w, layer, s_hg, page_table, past_len):
    nb = x.shape[0]
    f = lambda *s: jax.ShapeDtypeStruct(s, F32)
    n_pages = page_table.shape[1]
    ppb = MOBA_BLOCK // PAGE_SIZE
    nblk = n_pages // ppb
    assert n_pages % ppb == 0 and nblk >= MOBA_TOPK and nblk < LANES
    assert n_pages % GATE_CHUNK == 0 and GATE_CHUNK % ppb == 0
    c, s1, s2 = _rope_tables(np.full((1,), past_len))
    q, k, v, hq, fg, hv, hg = _call(
        functools.partial(_odd_sample_proj_body, layer=layer),
        tuple(f(nb, MOBA_WIDTH) for _ in range(7)),
        x, w["g"], w["win"], c, s1, s2, w["lb"], name="odd_sample_proj")

    rows = nb * HG_HEADS
    rb = HG_STEP_ROWS
    assert rows % rb == 0
    cols = lambda a: _pad_lanes(jnp.swapaxes(a.reshape(rows // rb, rb, HG_K), 1, 2))
    sspec = pl.BlockSpec((rb, HG_K, HG_V), lambda i: (i, 0, 0))
    cspec = pl.BlockSpec((1, HG_K, LANES), lambda i: (i, 0, 0))
    rspec = pl.BlockSpec((rb, HG_V), lambda i: (i, 0))
    s_new, o = pl.pallas_call(
        _hg_step_body, grid=(rows // rb,),
        in_specs=[sspec, cspec, cspec, rspec], out_specs=(sspec, rspec),
        out_shape=(f(rows, HG_K, HG_V), f(rows, HG_V)),
        compiler_params=_params(("parallel",)), name="hg_step")(
            s_hg.reshape(rows, HG_K, HG_V), cols(hq), cols(fg), hv.reshape(rows, HG_V))

    hd = MOBA_HEAD_DIM
    col = lambda a: jnp.broadcast_to(a.reshape(nb, MOBA_HEADS, hd, 1), (nb, MOBA_HEADS, hd, LANES))
    return dict(qcol=col(q), kcol=col(k), vcol=col(v), kn=k.reshape(nb, MOBA_HEADS, 1, hd),
                vn=v.reshape(nb, MOBA_HEADS, 1, hd), o=o.reshape(nb, HG_WIDTH), hg=hg,
                state=s_new.reshape(s_hg.shape))


def _odd_sample_post(x, w, pre, idx, cache_kt, cache_vt, cache_layer, page_table):
    nb = x.shape[0]
    f = lambda *s: jax.ShapeDtypeStruct(s, F32)
    hd = MOBA_HEAD_DIM
    nsel = MOBA_TOPK * (MOBA_BLOCK // PAGE_SIZE)
    colspec = pl.BlockSpec((1, MOBA_HEADS, hd, LANES), lambda i, *_: (i, 0, 0, 0))
    att = pl.pallas_call(
        functools.partial(_sample_attn_body, layer=cache_layer, n_seq=nb),
        grid_spec=pltpu.PrefetchScalarGridSpec(
            num_scalar_prefetch=2, grid=(nb,),
            in_specs=[pl.BlockSpec(memory_space=pl.ANY), pl.BlockSpec(memory_space=pl.ANY),
                      colspec, colspec, colspec],
            out_specs=colspec,
            scratch_shapes=[pltpu.VMEM((2, MOBA_HEADS, nsel, hd, PAGE_SIZE), F32),
                            pltpu.VMEM((2, MOBA_HEADS, nsel, hd, PAGE_SIZE), F32),
                            pltpu.SemaphoreType.DMA((2, 2))]),
        out_shape=f(nb, MOBA_HEADS, hd, LANES),
        compiler_params=_params(("arbitrary",)), name="sample_attn")(
            idx[:, :, :MOBA_TOPK].reshape(nb, MOBA_HEADS * MOBA_TOPK), page_table, cache_kt, cache_vt,
            pre["qcol"], pre["kcol"], pre["vcol"])
    att = att[:, :, :, 0].reshape(nb, MOBA_WIDTH)
    return _call(_odd_sample_out_body, f(nb, x.shape[1]),
                 x, att, pre["o"], pre["hg"], w["gn"], w["woa"], w["woh"], name="odd_sample_out")


def _row(a):
    return a.reshape(1, -1).astype(F32)


def _pad_lanes(a, n=LANES):
    return jnp.pad(a, [(0, 0)] * (a.ndim - 1) + [(0, n - a.shape[-1])])


def kernel(x_prompt, x_sample, state_ssd, state_ssd_conv, state_cf_conv, cache_k, cache_v, page_table, state_hg,
           ffn1_norm, ffn1_w_gu, ffn1_w_down, mix_norm, ffn2_norm, ffn2_w_gu, ffn2_w_down, final_norm,
           even_w_in, ssd_conv_w, ssd_conv_b, ssd_dt_bias, ssd_a_log, ssd_d, ssd_norm,
           cf_dw_w, cf_dw_b, cf_ln_g, cf_ln_b, even_w_out,
           odd_w_in, hg_lower_bound, hg_norm, odd_w_out):
    depth = ffn1_norm.shape[0]
    bp, lp, d = x_prompt.shape
    nb = x_sample.shape[0]
    assert x_sample.shape[1] == 1 and lp % (2 * TILE) == 0
    past_len = page_table.shape[1] * PAGE_SIZE

    ffn_bf16 = [a.astype(BF16) for a in (ffn1_w_gu, ffn1_w_down, ffn2_w_gu, ffn2_w_down)]

    def ffn_w(norm, w_gu, w_down, l):
        return (_row(norm[l]), w_gu, w_down, l)

    def even_w(e, l):
        wi = even_w_in[e]
        o1 = SSD_INNER
        o2 = o1 + SSD_CONV_DIM
        o3 = o2 + SSD_HEADS
        sel = (np.arange(LANES)[:, None] == np.arange(SSD_INNER)[None, :] // SSD_HEAD_DIM).astype(np.float32)
        return dict(
            g=_row(mix_norm[l]), wz=wi[:, :o1].astype(BF16), wxbc=wi[:, o1:o2].astype(BF16),
            wdt=_pad_lanes(wi[:, o2:o3]).astype(BF16), wga=wi[:, o3:o3 + CF_CH].astype(BF16),
            wgb=wi[:, o3 + CF_CH:].astype(BF16), cw=ssd_conv_w[e], cb=_row(ssd_conv_b[e]),
            dtb=_pad_lanes(_row(ssd_dt_bias[e])), alog=_pad_lanes(_row(ssd_a_log[e])),
            dfull=_row(jnp.repeat(ssd_d[e], SSD_HEAD_DIM)), sn=_row(ssd_norm[e]), sel=jnp.asarray(sel, BF16),
            cfw=cf_dw_w[e], cfb=_row(cf_dw_b[e]), cfg=_row(cf_ln_g[e]), cfbeta=_row(cf_ln_b[e]),
            woy=even_w_out[e, :SSD_INNER].astype(BF16), woc=even_w_out[e, SSD_INNER:].astype(BF16))

    def odd_w(o, l):
        return dict(g=_row(mix_norm[l]), win=odd_w_in[o].astype(BF16), lb=hg_lower_bound.astype(F32),
                    gn=_row(hg_norm[o]), woa=odd_w_out[o, :MOBA_WIDTH].astype(BF16),
                    woh=odd_w_out[o, MOBA_WIDTH:].astype(BF16))

    cache_kt = jnp.swapaxes(cache_k, -1, -2)
    cache_vt = jnp.swapaxes(cache_v, -1, -2)
    layer_w = []
    for l in range(depth):
        mix = even_w(l // 2, l) if l % 2 == 0 else odd_w(l // 2, l)
        layer_w.append((ffn_w(ffn1_norm, ffn_bf16[0], ffn_bf16[1], l), mix,
                        ffn_w(ffn2_norm, ffn_bf16[2], ffn_bf16[3], l), _row(final_norm) if l == depth - 1 else None))

    steps = bp * lp // TILE
    n_chunks = page_table.shape[1] // GATE_CHUNK
    host_layer = depth - 1 if depth % 2 == 0 else depth - 2
    if host_layer < 1 or steps % n_chunks or 2 * depth * (steps // n_chunks) != nb:
        host_layer = None

    def sample_group(hosted_gate):
        xs = x_sample.reshape(nb, d)
        ssd_s, sc_s, cf_s, k_s, v_s, hg_s = [], [], [], [], [], []
        for l, (w1, w, w2, fin) in enumerate(layer_w):
            xs = _ffn(xs, w1, tm=nb)
            if l % 2 == 0:
                e = l // 2
                xs, st, xt, ut = _even_sample(xs, w, state_ssd[e], state_ssd_conv[e], state_cf_conv[e])
                ssd_s.append(st)
                sc_s.append(xt)
                cf_s.append(ut)
            else:
                o = l // 2
                pre = _odd_sample_pre(xs, w, l, state_hg[o], page_table, past_len)
                if l == host_layer:
                    idx = hosted_gate(pre["qcol"], o)
                else:
                    idx = _sample_gate(page_table, cache_kt, pre["qcol"], o)
                xs = _odd_sample_post(xs, w, pre, idx, cache_kt, cache_vt, o, page_table)
                k_s.append(pre["kn"])
                v_s.append(pre["vn"])
                hg_s.append(pre["state"])
            xs = _ffn(xs, w2, tm=nb, final_g=fin)
        return xs.reshape(nb, 1, d), ssd_s, sc_s, cf_s, k_s, v_s, hg_s

    def prompt_group(hosted):
        xp = x_prompt.reshape(bp * lp, d)
        ssd_p, sc_p, cf_p, k_p, v_p, hg_p, ids = [], [], [], [], [], [], []

        def ffn(x, *args, **kw):
            if hosted is None:
                return _ffn(x, *args, **kw)
            qcol, cache_layer = hosted
            x, part = _ffn(x, *args, gate=(page_table, cache_kt, qcol, cache_layer, len(ids) * (steps // n_chunks)),
                           **kw)
            ids.append(part)
            return x

        for l, (w1, w, w2, fin) in enumerate(layer_w):
            xp = ffn(xp, w1, tm=TILE)
            if l % 2 == 0:
                xp3, st, xt, ut = _even_prompt(xp.reshape(bp, lp, d), w)
                xp = xp3.reshape(bp * lp, d)
                ssd_p.append(st)
                sc_p.append(xt)
                cf_p.append(ut)
                xp = ffn(xp, w2, tm=TILE, final_g=fin)
            else:
                q, kb, vt, kmean, ko, vo, ohg, hst = _odd_prompt_proj(xp.reshape(bp, lp, d), w, l)
                att = _moba_tiles(q, kb, vt, kmean[:, :, 0, :])
                k_p.append(jnp.swapaxes(ko, -1, -2))
                v_p.append(jnp.swapaxes(vo, -1, -2))
                hg_p.append(hst)
                xp = ffn(xp, w2, tm=TILE, final_g=fin, pre=(att.reshape(bp * lp, MOBA_WIDTH),
                                                            ohg.reshape(bp * lp, HG_WIDTH), w["woa"], w["woh"]))
        return (xp.reshape(bp, lp, d), ssd_p, sc_p, cf_p, k_p, v_p, hg_p), ids

    prompt_result = []

    def hosted_gate(qcol, cache_layer):
        out, ids = prompt_group((qcol, cache_layer))
        prompt_result.append(out)
        return jnp.concatenate(ids, axis=0)

    sample_out = sample_group(hosted_gate)
    prompt_out = prompt_result[0] if prompt_result else prompt_group(None)[0]
    y_p, ssd_p, sc_p, cf_p, k_p, v_p, hg_p = prompt_out
    y_s, ssd_s, sc_s, cf_s, k_s, v_s, hg_s = sample_out
    st = jnp.stack
    return (y_p, y_s, st(ssd_p), st(ssd_s), st(sc_p), st(sc_s), st(cf_p), st(cf_s),
            st(k_p), st(k_s), st(v_p), st(v_s), st(hg_p), st(hg_s))
```

```python
import functools
import math

import numpy as np
import jax
import jax.numpy as jnp
from jax import lax
from jax.experimental import pallas as pl
from jax.experimental.pallas import tpu as pltpu

F32 = jnp.float32
BF16 = jnp.bfloat16

SSD_HEADS = 16
SSD_HEAD_DIM = 64
SSD_INNER = SSD_HEADS * SSD_HEAD_DIM
SSD_GROUPS = 2
SSD_STATE = 64
SSD_CONV = 4
SSD_CONV_DIM = SSD_INNER + 2 * SSD_GROUPS * SSD_STATE
CF_CH = 512
CF_WIDTH = 31
MOBA_HEADS = 8
MOBA_HEAD_DIM = 64
MOBA_WIDTH = MOBA_HEADS * MOBA_HEAD_DIM
MOBA_BLOCK = 256
MOBA_TOPK = 3
ROPE_DIM = MOBA_HEAD_DIM // 4
ROPE_THETA = 500000.0
HG_HEADS = 4
HG_K = 128
HG_V = 128
HG_WIDTH = HG_HEADS * HG_K
PAGE_SIZE = 128
NORM_EPS = 1e-6
NEG = -1e30

LANES = 128
SSD_CHUNK = 128
TILE = 256
VT_ROWS = MOBA_HEAD_DIM + 16
VMEM_LIMIT = 56 * 1024 * 1024


def _sigmoid(x):
    return 1.0 / (1.0 + jnp.exp(-x))


def _silu(x):
    return x * _sigmoid(x)


def _softplus(x):
    return jnp.maximum(x, 0.0) + jnp.log1p(jnp.exp(-jnp.abs(x)))


def _rms(x, g):
    return x * lax.rsqrt(jnp.mean(x * x, axis=-1, keepdims=True) + NORM_EPS) * g


def _dot(a, b):
    return jnp.dot(a.astype(BF16), b.astype(BF16), preferred_element_type=F32)


def _dot_nt(a, b):
    return lax.dot_general(a.astype(BF16), b.astype(BF16), (((1,), (1,)), ((), ())),
                           preferred_element_type=F32)


def _split3(a):
    hi = a.astype(BF16)
    r = a - hi.astype(F32)
    mid = r.astype(BF16)
    lo = (r - mid.astype(F32)).astype(BF16)
    return hi, mid, lo


def _dot3_l(a, m):
    hi, mid, lo = _split3(a)
    return (jnp.dot(hi, m, preferred_element_type=F32) + jnp.dot(mid, m, preferred_element_type=F32)
            + jnp.dot(lo, m, preferred_element_type=F32))


def _dot3_r(m, parts):
    hi, mid, lo = parts
    return (jnp.dot(m, hi, preferred_element_type=F32) + jnp.dot(m, mid, preferred_element_type=F32)
            + jnp.dot(m, lo, preferred_element_type=F32))


def _const_spec(shape):
    n = len(shape)
    return pl.BlockSpec(shape, lambda *_: (0,) * n, pipeline_mode=pl.Buffered(1))


def _params(sem):
    return pltpu.CompilerParams(dimension_semantics=sem, vmem_limit_bytes=VMEM_LIMIT)


GATE_CHUNK = 16


def _gate_copies(pt_ref, ck_ref, kbuf, sem, layer, seq, chunk, slot):
    return [pltpu.make_async_copy(ck_ref.at[layer, pt_ref[seq, chunk * GATE_CHUNK + j]], kbuf.at[slot, j],
                                  sem.at[slot]) for j in range(GATE_CHUNK)]


def _gate_fetch(pt_ref, ck_ref, kbuf, sem, *, layer, seq0, n_chunks, n_steps):
    step = pl.program_id(0)
    slot = step % 2
    seq = seq0 + step // n_chunks
    c = step % n_chunks

    @pl.when(step == 0)
    def _():
        for cp in _gate_copies(pt_ref, ck_ref, kbuf, sem, layer, seq, c, slot):
            cp.start()

    nxt = step + 1

    @pl.when(nxt < n_steps)
    def _():
        for cp in _gate_copies(pt_ref, ck_ref, kbuf, sem, layer, seq0 + nxt // n_chunks, nxt % n_chunks, 1 - slot):
            cp.start()

    for cp in _gate_copies(pt_ref, ck_ref, kbuf, sem, layer, seq, c, slot):
        cp.wait()
    return slot, c


GATE_CHUNK_BLOCKS = GATE_CHUNK // (MOBA_BLOCK // PAGE_SIZE)


def _gate_accumulate(kbuf, q_ref, g_ref, slot, c):
    ppb = MOBA_BLOCK // PAGE_SIZE
    nbc = GATE_CHUNK_BLOCKS
    lane = lax.broadcasted_iota(jnp.int32, (nbc, LANES), 1)
    sub = lax.broadcasted_iota(jnp.int32, (nbc, LANES), 0)
    rows = jnp.zeros((nbc, LANES), F32)
    for h in range(MOBA_HEADS):
        q = q_ref[0, h]
        tile = jnp.zeros((nbc, LANES), F32)
        for jb in range(nbc):
            ksum = kbuf[slot, jb * ppb, h]
            for r in range(1, ppb):
                ksum = ksum + kbuf[slot, jb * ppb + r, h]
            tile = jnp.where(sub == jb, jnp.sum(ksum * q, axis=0, keepdims=True), tile)
        rows = jnp.where(lane == h, jnp.sum(tile, axis=1, keepdims=True), rows)
    g_ref[pl.ds(pl.multiple_of(c * nbc, nbc), nbc), :] = rows


def _gate_finish(g_ref, idx_ref, c, n_chunks):
    @pl.when(c == n_chunks - 1)
    def _():
        nblk = g_ref.shape[0]
        row = lax.broadcasted_iota(jnp.int32, (nblk, LANES), 0)
        rank = lax.broadcasted_iota(jnp.int32, idx_ref.shape[1:], 0)
        gcur = g_ref[...] * (1.0 / MOBA_BLOCK)
        out = jnp.zeros(idx_ref.shape[1:], jnp.int32)
        for r in range(MOBA_TOPK):
            mx = jnp.max(gcur, axis=0, keepdims=True)
            first = jnp.min(jnp.where(gcur == mx, row, nblk), axis=0, keepdims=True)
            out = jnp.where(rank == r, first, out)
            gcur = jnp.where(row == first, -jnp.inf, gcur)
        idx_ref[0] = out


def _gate_scratch(n_chunks):
    return [pltpu.VMEM((2, GATE_CHUNK, MOBA_HEADS, MOBA_HEAD_DIM, PAGE_SIZE), F32), pltpu.SemaphoreType.DMA((2,)),
            pltpu.VMEM((n_chunks * GATE_CHUNK_BLOCKS, LANES), F32)]


def _sample_gate_body(pt_ref, ck_ref, q_ref, idx_ref, kbuf, sem, g_ref, **job):
    slot, c = _gate_fetch(pt_ref, ck_ref, kbuf, sem, **job)
    _gate_accumulate(kbuf, q_ref, g_ref, slot, c)
    _gate_finish(g_ref, idx_ref, c, job["n_chunks"])


def _sample_gate(page_table, cache_kt, qcol, layer):
    nb, n_pages = page_table.shape
    n_chunks = n_pages // GATE_CHUNK
    job = dict(layer=layer, seq0=0, n_chunks=n_chunks, n_steps=nb * n_chunks)
    return pl.pallas_call(
        functools.partial(_sample_gate_body, **job),
        grid_spec=pltpu.PrefetchScalarGridSpec(
            num_scalar_prefetch=1, grid=(nb * n_chunks,),
            in_specs=[pl.BlockSpec(memory_space=pl.ANY),
                      pl.BlockSpec((1,) + qcol.shape[1:], lambda s, pt: (s // n_chunks, 0, 0, 0))],
            out_specs=pl.BlockSpec((1, MOBA_HEADS, LANES), lambda s, pt: (s // n_chunks, 0, 0)),
            scratch_shapes=_gate_scratch(n_chunks)),
        out_shape=jax.ShapeDtypeStruct((nb, MOBA_HEADS, LANES), jnp.int32),
        compiler_params=_params(("arbitrary",)), name="sample_gate")(page_table, cache_kt, qcol)


def _ffn_body(*refs, pre, final, job):
    it = iter(refs)
    if job is not None:
        pt_ref = next(it)
    x_ref = next(it)
    if pre:
        a_ref, b_ref, wa_ref, wb_ref = next(it), next(it), next(it), next(it)
    g_ref, wgu_ref, wd_ref = next(it), next(it), next(it)
    if final:
        fg_ref = next(it)
    if job is not None:
        ck_ref, q_ref = next(it), next(it)
    o_ref = next(it)
    if job is not None:
        idx_ref, kbuf, sem, gacc_ref = next(it), next(it), next(it), next(it)
        slot, c = _gate_fetch(pt_ref, ck_ref, kbuf, sem, **job)
        _gate_accumulate(kbuf, q_ref, gacc_ref, slot, c)
    d_ff = wd_ref.shape[1]
    x = x_ref[...]
    if pre:
        x = x + _dot(a_ref[...], wa_ref[...]) + _dot(b_ref[...], wb_ref[...])
    hb = _rms(x, g_ref[...]).astype(BF16)
    g = jnp.dot(hb, wgu_ref[0, :, :d_ff], preferred_element_type=F32)
    u = jnp.dot(hb, wgu_ref[0, :, d_ff:], preferred_element_type=F32)
    act = (_silu(g) * u).astype(BF16)
    y = x + 0.5 * jnp.dot(act, wd_ref[0], preferred_element_type=F32)
    if final:
        y = _rms(y, fg_ref[...])
    o_ref[...] = y
    if job is not None:
        _gate_finish(gacc_ref, idx_ref, c, job["n_chunks"])


def _ffn(x, w, *, tm, pre=None, final_g=None, gate=None):
    m, d = x.shape
    g, wgu, wd, layer = w
    steps = m // tm
    row = lambda c: pl.BlockSpec((tm, c), lambda i, *_: (i, 0))
    layer_spec = lambda a: pl.BlockSpec((1,) + a.shape[1:], lambda i, *_: (layer, 0, 0),
                                        pipeline_mode=pl.Buffered(1))
    args, specs = [x], [row(d)]
    if pre is not None:
        a, b, wa, wb = pre
        args += [a, b, wa, wb]
        specs += [row(a.shape[1]), row(b.shape[1]), _const_spec(wa.shape), _const_spec(wb.shape)]
    args += [g, wgu, wd]
    specs += [_const_spec(g.shape), layer_spec(wgu), layer_spec(wd)]
    if final_g is not None:
        args.append(final_g)
        specs.append(_const_spec(final_g.shape))
    out_shape = jax.ShapeDtypeStruct((m, d), F32)
    if gate is None:
        return pl.pallas_call(
            functools.partial(_ffn_body, pre=pre is not None, final=final_g is not None, job=None),
            grid=(steps,), in_specs=specs, out_specs=row(d), out_shape=out_shape,
            compiler_params=_params(("parallel",)), name="ffn")(*args)
    page_table, cache_kt, qcol, cache_layer, seq0 = gate
    n_chunks = page_table.shape[1] // GATE_CHUNK
    n_seq = steps // n_chunks
    assert n_seq * n_chunks == steps
    job = dict(layer=cache_layer, seq0=seq0, n_chunks=n_chunks, n_steps=steps)
    specs += [pl.BlockSpec(memory_space=pl.ANY),
              pl.BlockSpec((1,) + qcol.shape[1:], lambda i, pt: (seq0 + i // n_chunks, 0, 0, 0))]
    return pl.pallas_call(
        functools.partial(_ffn_body, pre=pre is not None, final=final_g is not None, job=job),
        grid_spec=pltpu.PrefetchScalarGridSpec(
            num_scalar_prefetch=1, grid=(steps,), in_specs=specs,
            out_specs=(row(d), pl.BlockSpec((1, MOBA_HEADS, LANES), lambda i, pt: (i // n_chunks, 0, 0))),
            scratch_shapes=_gate_scratch(n_chunks)),
        out_shape=(out_shape, jax.ShapeDtypeStruct((n_seq, MOBA_HEADS, LANES), jnp.int32)),
        compiler_params=_params(("arbitrary",)), name="ffn_gate")(page_table, *args, cache_kt, qcol)


def _even_prompt_body(x_ref, g_ref, wz_ref, wxbc_ref, wdt_ref, wga_ref, wgb_ref,
                      cw_ref, cb_ref, dtb_ref, alog_ref, dfull_ref, sn_ref, sel_ref,
                      cfw_ref, cfb_ref, cfg_ref, cfbeta_ref, woy_ref, woc_ref,
                      xo_ref, st_ref, xtail_ref, utail_ref,
                      xbuf, ubuf, s_ref, ybuf, *, nt):
    t = pl.program_id(1)
    T = TILE
    C = SSD_CHUNK

    @pl.when(t == 0)
    def _():
        xbuf[0:8, :] = jnp.zeros((8, SSD_CONV_DIM), F32)
        ubuf[0:32, :] = jnp.zeros((32, CF_CH), F32)
        s_ref[...] = jnp.zeros(s_ref.shape, F32)

    x = x_ref[0]
    hb = _rms(x, g_ref[...]).astype(BF16)
    z = jnp.dot(hb, wz_ref[...], preferred_element_type=F32)
    xbc = jnp.dot(hb, wxbc_ref[...], preferred_element_type=F32)
    dtr = jnp.dot(hb, wdt_ref[...], preferred_element_type=F32)
    ga = jnp.dot(hb, wga_ref[...], preferred_element_type=F32)
    gb = jnp.dot(hb, wgb_ref[...], preferred_element_type=F32)

    xbuf[8:8 + T, :] = xbc
    k0 = SSD_CONV - 1
    acc = cw_ref[k0:k0 + 1, :] * xbuf[5 + k0:5 + k0 + T, :]
    for k in range(k0):
        acc = acc + cw_ref[k:k + 1, :] * xbuf[5 + k:5 + k + T, :]
    tail8 = xbuf[T:T + 8, :]
    xtail_ref[0] = tail8
    xbuf[0:8, :] = tail8
    xc = _silu(acc + cb_ref[...])
    xs = xc[:, 0:SSD_INNER]
    bm = xc[:, SSD_INNER:SSD_INNER + LANES]
    cm = xc[:, SSD_INNER + LANES:SSD_INNER + 2 * LANES]
    dt = _softplus(dtr + dtb_ref[...])
    dta = dt * (-jnp.exp(alog_ref[...]))
    sel = sel_ref[...]

    ri = lax.broadcasted_iota(jnp.int32, (C, C), 0)
    ci = lax.broadcasted_iota(jnp.int32, (C, C), 1)
    tri = ri >= ci
    trib = tri.astype(BF16)
    low = ci < SSD_HEAD_DIM

    for c in range(T // C):
        r0 = c * C
        xs_c, bm_c, cm_c = xs[r0:r0 + C], bm[r0:r0 + C], cm[r0:r0 + C]
        dt_c = dt[r0:r0 + C]
        cum = _dot3_r(trib, _split3(dta[r0:r0 + C]))
        cum_t = cum.T
        dt_t = dt_c.T
        cum_last = cum[C - 1:C, :]
        ecum_full = _dot3_l(jnp.exp(cum), sel)
        tail_full = _dot3_l(jnp.exp(cum_last - cum) * dt_c, sel)
        dlast_full = _dot3_l(jnp.broadcast_to(jnp.exp(cum_last), (8, LANES)), sel)[0:1]
        bm_t = bm_c.T
        cmb = cm_c.astype(BF16)
        for grp in range(SSD_GROUPS):
            bm_tg = jnp.where((ri // SSD_STATE) == grp, bm_t, 0.0).astype(BF16)
            gmat = jnp.dot(cmb, bm_tg, preferred_element_type=F32)
            pairs = SSD_HEADS // SSD_GROUPS // 2
            for pp in range(pairs):
                p = grp * pairs + pp
                lanes = slice(p * LANES, (p + 1) * LANES)
                xp = xs_c[:, lanes]
                xpb = xp.astype(BF16)
                ys = []
                for e in range(2):
                    h = 2 * p + e
                    seg = cum[:, h:h + 1] - cum_t[h:h + 1, :]
                    wm = gmat * jnp.exp(jnp.where(tri, seg, NEG)) * dt_t[h:h + 1, :]
                    ys.append(jnp.dot(wm.astype(BF16), xpb, preferred_element_type=F32))
                yp = jnp.where(low, ys[0], ys[1])
                sp = s_ref[p]
                yp = yp + jnp.dot(cmb, sp.astype(BF16), preferred_element_type=F32) * ecum_full[:, lanes]
                s_ref[p] = dlast_full[:, lanes] * sp + jnp.dot(
                    bm_tg, (xp * tail_full[:, lanes]).astype(BF16), preferred_element_type=F32)
                ybuf[r0:r0 + C, lanes] = yp

    @pl.when(t == nt - 1)
    def _():
        for p in range(SSD_HEADS // 2):
            st_ref[0, p] = s_ref[p].T

    y = (ybuf[...] + dfull_ref[...] * xs) * _silu(z)
    gw = SSD_INNER // SSD_GROUPS
    yn = []
    for grp in range(SSD_GROUPS):
        yg = y[:, grp * gw:(grp + 1) * gw]
        yn.append(_rms(yg, sn_ref[:, grp * gw:(grp + 1) * gw]).astype(BF16))

    u = ga * _sigmoid(gb)
    ubuf[32:32 + T, :] = u
    base = 32 - (CF_WIDTH - 1)
    cacc = None
    for r in range(8):
        offs = [o for o in range(base, base + CF_WIDTH) if o % 8 == r]
        if not offs:
            continue
        ur = ubuf[r:max(offs) + T, :]
        part = None
        for o in offs:
            term = cfw_ref[o - base:o - base + 1, :] * ur[o - r:o - r + T]
            part = term if part is None else part + term
        cacc = part if cacc is None else cacc + part
    tail32 = ubuf[T:T + 32, :]
    utail_ref[0] = tail32
    ubuf[0:32, :] = tail32
    c32 = cacc + cfb_ref[...]
    mu = jnp.mean(c32, axis=-1, keepdims=True)
    var = jnp.mean(jnp.square(c32 - mu), axis=-1, keepdims=True)
    c32 = _silu((c32 - mu) * lax.rsqrt(var + NORM_EPS) * cfg_ref[...] + cfbeta_ref[...])

    out = jnp.dot(c32.astype(BF16), woc_ref[...], preferred_element_type=F32)
    for grp in range(SSD_GROUPS):
        out = out + jnp.dot(yn[grp], woy_ref[grp * gw:(grp + 1) * gw, :], preferred_element_type=F32)
    xo_ref[0] = x + out


def _even_prompt(x, w):
    b, l, d = x.shape
    nt = l // TILE
    consts = [w[k] for k in ("g", "wz", "wxbc", "wdt", "wga", "wgb", "cw", "cb", "dtb", "alog", "dfull", "sn",
                             "sel", "cfw", "cfb", "cfg", "cfbeta", "woy", "woc")]
    out_shape = (jax.ShapeDtypeStruct((b, l, d), F32),
                 jax.ShapeDtypeStruct((b, SSD_HEADS // 2, LANES, LANES), F32),
                 jax.ShapeDtypeStruct((b, 8, SSD_CONV_DIM), F32),
                 jax.ShapeDtypeStruct((b, 32, CF_CH), F32))
    out_specs = (pl.BlockSpec((1, TILE, d), lambda i, t: (i, t, 0)),
                 pl.BlockSpec((1, SSD_HEADS // 2, LANES, LANES), lambda i, t: (i, 0, 0, 0)),
                 pl.BlockSpec((1, 8, SSD_CONV_DIM), lambda i, t: (i, 0, 0)),
                 pl.BlockSpec((1, 32, CF_CH), lambda i, t: (i, 0, 0)))
    xo, st, xtail, utail = pl.pallas_call(
        functools.partial(_even_prompt_body, nt=nt),
        grid=(b, nt),
        in_specs=[pl.BlockSpec((1, TILE, d), lambda i, t: (i, t, 0))] + [_const_spec(c.shape) for c in consts],
        out_specs=out_specs, out_shape=out_shape,
        scratch_shapes=[pltpu.VMEM((TILE + 8, SSD_CONV_DIM), F32), pltpu.VMEM((TILE + 32, CF_CH), F32),
                        pltpu.VMEM((SSD_HEADS // 2, LANES, LANES), F32), pltpu.VMEM((TILE, SSD_INNER), F32)],
        compiler_params=_params(("arbitrary", "arbitrary")), name="even_prompt")(x, *consts)
    half = SSD_HEADS // 2 // SSD_GROUPS
    st = st.reshape(b, SSD_HEADS // 2, 2, SSD_HEAD_DIM, SSD_GROUPS, SSD_STATE)
    state = jnp.concatenate([st[:, grp * half:(grp + 1) * half, :, :, grp, :] for grp in range(SSD_GROUPS)], axis=1)
    state = state.reshape(b, SSD_HEADS, SSD_HEAD_DIM, SSD_STATE)
    return xo, state, xtail[:, 8 - (SSD_CONV - 1):], utail[:, 32 - (CF_WIDTH - 1):]


def _hg_levels(T):
    t = np.arange(T)
    le = (t[None, :] <= t[:, None]).astype(np.float32)
    ds, ms = [], []
    m = 1
    while m < T:
        rb = (t // (2 * m)) * 2 * m + m - 1
        if m < 8:
            ds.append(le - (t[None, :] <= rb[:, None]).astype(np.float32))
        same = (t[:, None] // (2 * m)) == (t[None, :] // (2 * m))
        ms.append((same & ((t[:, None] % (2 * m)) >= m) & ((t[None, :] % (2 * m)) < m)).astype(np.float32))
        m *= 2
    return (jnp.asarray(le, BF16), jnp.asarray(np.stack(ds), BF16), jnp.asarray(np.stack(ms), BF16))


def _rope_tables(pos):
    half = ROPE_DIM // 2
    inv = ROPE_THETA ** (-np.arange(half, dtype=np.float64) / half)
    ang = pos.astype(np.float64)[:, None] * inv[None, :]
    cos, sin = np.cos(ang), np.sin(ang)
    n = pos.shape[0]
    one = np.ones((n, MOBA_HEAD_DIM - ROPE_DIM))
    zero = np.zeros((n, MOBA_HEAD_DIM - ROPE_DIM))
    zh = np.zeros((n, half))
    c = np.concatenate([cos, cos, one], axis=1)
    s1 = np.concatenate([-sin, zh, zero], axis=1)
    s2 = np.concatenate([zh, sin, zero], axis=1)
    return tuple(jnp.asarray(np.concatenate([a, a], axis=1), F32) for a in (c, s1, s2))


def _rope(x, c, s1, s2):
    outs = []
    for j in range(x.shape[1] // LANES):
        blk = x[:, j * LANES:(j + 1) * LANES]
        outs.append(blk * c + pltpu.roll(blk, LANES - ROPE_DIM // 2, 1) * s1 + pltpu.roll(blk, ROPE_DIM // 2, 1) * s2)
    return outs


def _hg_lower_bound(lb_ref, layer):
    a = lb_ref[...]
    mx = jnp.max(a, axis=0, keepdims=True)
    e = jnp.exp(a - mx)
    return jnp.sum(e[1:layer + 1], axis=0, keepdims=True) / jnp.sum(e, axis=0, keepdims=True)


def _odd_prompt_body(x_ref, g_ref, win_ref, c_ref, s1_ref, s2_ref, lb_ref, gn_ref, tril_ref, dm_ref, mm_ref,
                     q_ref, kb_ref, vt_ref, km_ref, ko_ref, vo_ref, o_ref, hs_ref,
                     st_ref, *, nt, layer):
    t = pl.program_id(1)
    T = TILE
    W = MOBA_WIDTH

    @pl.when(t == 0)
    def _():
        st_ref[...] = jnp.zeros(st_ref.shape, F32)

    x = x_ref[0]
    hb = _rms(x, g_ref[...]).astype(BF16)
    proj = jnp.dot(hb, win_ref[...], preferred_element_type=F32)

    c, s1, s2 = c_ref[...], s1_ref[...], s2_ref[...]
    qb = _rope(proj[:, 0:W], c, s1, s2)
    kb = _rope(proj[:, W:2 * W], c, s1, s2)
    v = proj[:, 2 * W:3 * W]
    for j in range(W // LANES):
        lanes = slice(j * LANES, (j + 1) * LANES)
        q_ref[0, :, lanes] = qb[j]
        kb_ref[0, :, lanes] = kb[j].astype(BF16)
        km_ref[0, 0, :, lanes] = jnp.mean(kb[j], axis=0, keepdims=True)
        kt = kb[j].T
        vt = v[:, lanes].T
        for e in range(2):
            ko_ref[0, 2 * j + e] = kt[e * 64:(e + 1) * 64, :]
            vo_ref[0, 2 * j + e] = vt[e * 64:(e + 1) * 64, :]
            r0 = (2 * j + e) * VT_ROWS
            vt_ref[0, r0:r0 + 64, :] = vt[e * 64:(e + 1) * 64, :].astype(BF16)
            vt_ref[0, r0 + 64:r0 + VT_ROWS, :] = (
                lax.broadcasted_iota(jnp.int32, (VT_ROWS - 64, T), 0) == 0).astype(BF16)

    lb = _hg_lower_bound(lb_ref, layer)
    hq = _silu(proj[:, 3 * W:3 * W + HG_WIDTH])
    f = lb + (1.0 - lb) * _sigmoid(proj[:, 3 * W + HG_WIDTH:3 * W + 2 * HG_WIDTH])
    hv = proj[:, 3 * W + 2 * HG_WIDTH:3 * W + 3 * HG_WIDTH]
    hgate = proj[:, 3 * W + 3 * HG_WIDTH:3 * W + 4 * HG_WIDTH]
    kk = 1.0 - f
    lf3 = _split3(jnp.log(f))
    cum = _dot3_r(tril_ref[...], lf3)
    nlev = mm_ref.shape[0]
    ri = lax.broadcasted_iota(jnp.int32, (T, T), 0)
    ci = lax.broadcasted_iota(jnp.int32, (T, T), 1)
    eye = ri == ci
    amat = []
    for h in range(HG_HEADS):
        lanes = slice(h * HG_K, (h + 1) * HG_K)
        diag = jnp.sum(hq[:, lanes] * kk[:, lanes], axis=1, keepdims=True)
        amat.append(jnp.where(eye, diag, 0.0))
    for lev in range(nlev):
        m = 1 << lev
        if lev < dm_ref.shape[0]:
            d = _dot3_r(dm_ref[lev], lf3)
        else:
            refs = [jnp.broadcast_to(cum[r0 + m - 1:r0 + m, :], (2 * m, cum.shape[1])) for r0 in range(0, T, 2 * m)]
            d = cum - (refs[0] if len(refs) == 1 else jnp.concatenate(refs, axis=0))
        e_all = jnp.exp(-jnp.abs(d))
        mk = mm_ref[lev].astype(F32)
        for h in range(HG_HEADS):
            lanes = slice(h * HG_K, (h + 1) * HG_K)
            a_m = _dot_nt(hq[:, lanes] * e_all[:, lanes], kk[:, lanes] * e_all[:, lanes])
            amat[h] = amat[h] + a_m * mk
    cum_last = cum[T - 1:T, :]
    ecum = jnp.exp(cum)
    ktail = kk * jnp.exp(cum_last - cum)
    elast = jnp.exp(cum_last)
    for h in range(HG_HEADS):
        lanes = slice(h * HG_K, (h + 1) * HG_K)
        vh = hv[:, lanes]
        vhb = vh.astype(BF16)
        st = st_ref[h]
        o = jnp.dot(amat[h].astype(BF16), vhb, preferred_element_type=F32)
        o = o + _dot_nt(hq[:, lanes] * ecum[:, lanes], st)
        st_new = elast[:, lanes] * st + jnp.dot(vh.T.astype(BF16), ktail[:, lanes].astype(BF16),
                                                preferred_element_type=F32)
        st_ref[h] = st_new
        o_ref[0, :, lanes] = _rms(o, gn_ref[:, lanes]) * _silu(hgate[:, lanes])

    @pl.when(t == nt - 1)
    def _():
        for h in range(HG_HEADS):
            hs_ref[0, h] = st_ref[h].T


def _odd_prompt_proj(x, w, layer):
    b, l, d = x.shape
    nt = l // TILE
    c, s1, s2 = _rope_tables(np.arange(l))
    tril, dm, mm = _hg_levels(TILE)
    consts_a = [w["g"], w["win"]]
    consts_b = [w["lb"], w["gn"], tril, dm, mm]
    tab = pl.BlockSpec((TILE, LANES), lambda i, t: (t, 0))
    W = MOBA_WIDTH
    out_shape = (jax.ShapeDtypeStruct((b, l, W), F32),
                 jax.ShapeDtypeStruct((b, l, W), BF16),
                 jax.ShapeDtypeStruct((b, MOBA_HEADS * VT_ROWS, l), BF16),
                 jax.ShapeDtypeStruct((b, nt, 1, W), F32),
                 jax.ShapeDtypeStruct((b, MOBA_HEADS, MOBA_HEAD_DIM, l), F32),
                 jax.ShapeDtypeStruct((b, MOBA_HEADS, MOBA_HEAD_DIM, l), F32),
                 jax.ShapeDtypeStruct((b, l, HG_WIDTH), F32),
                 jax.ShapeDtypeStruct((b, HG_HEADS, HG_K, HG_V), F32))
    out_specs = (pl.BlockSpec((1, TILE, W), lambda i, t: (i, t, 0)),
                 pl.BlockSpec((1, TILE, W), lambda i, t: (i, t, 0)),
                 pl.BlockSpec((1, MOBA_HEADS * VT_ROWS, TILE), lambda i, t: (i, 0, t)),
                 pl.BlockSpec((1, 1, 1, W), lambda i, t: (i, t, 0, 0)),
                 pl.BlockSpec((1, MOBA_HEADS, MOBA_HEAD_DIM, TILE), lambda i, t: (i, 0, 0, t)),
                 pl.BlockSpec((1, MOBA_HEADS, MOBA_HEAD_DIM, TILE), lambda i, t: (i, 0, 0, t)),
                 pl.BlockSpec((1, TILE, HG_WIDTH), lambda i, t: (i, t, 0)),
                 pl.BlockSpec((1, HG_HEADS, HG_K, HG_V), lambda i, t: (i, 0, 0, 0)))
    return pl.pallas_call(
        functools.partial(_odd_prompt_body, nt=nt, layer=layer),
        grid=(b, nt),
        in_specs=([pl.BlockSpec((1, TILE, d), lambda i, t: (i, t, 0))] + [_const_spec(a.shape) for a in consts_a]
                  + [tab, tab, tab] + [_const_spec(a.shape) for a in consts_b]),
        out_specs=out_specs, out_shape=out_shape,
        scratch_shapes=[pltpu.VMEM((HG_HEADS, HG_V, HG_K), F32)],
        compiler_params=_params(("arbitrary", "arbitrary")), name="odd_prompt_proj")(
            x, *consts_a, c, s1, s2, *consts_b)


def _moba_tile_body(q_ref, k_ref, vt_ref, km_ref, o_ref, sel_ref, s_ref, cm_ref, p_ref):
    j = pl.program_id(2)
    T = TILE
    Q = 2 * T
    hd = MOBA_HEAD_DIM
    q = q_ref[0]
    km = km_ref[0, 0]
    nbp = km.shape[0]
    lane = lax.broadcasted_iota(jnp.int32, (Q, LANES), 1)
    qes = [jnp.where((lane // hd) == e, q, 0.0) for e in range(2)]
    qs = [(qe * (hd ** -0.5 * math.log2(math.e))).astype(BF16) for qe in qes]
    last_blk = k_ref.shape[1] // T - 1

    def scores(b, slot, keep):
        off = pl.multiple_of(jnp.minimum(b, last_blk) * T, T)
        kk = k_ref[0, pl.ds(off, T), :]
        for e in range(2):
            s = jnp.where(keep(e), _dot_nt(kk, qs[e]), NEG)
            s_ref[slot, e] = s
            cm_ref[slot, e] = jnp.max(s, axis=0, keepdims=True)

    def selected(b):
        return lambda e: sel_ref[e, pl.ds(b, 1), :] > 0.5

    def pv(b, slot, e):
        off = pl.multiple_of(jnp.clip(b, 0, last_blk) * T, T)
        return jnp.dot(vt_ref[0, e * VT_ROWS:(e + 1) * VT_ROWS, pl.ds(off, T)], p_ref[slot, e],
                       preferred_element_type=F32)

    def softmax_step(slot, e, m, acc, prev):
        m_new = jnp.maximum(m, cm_ref[slot, e])
        p_ref[slot, e] = jnp.exp2((s_ref[slot, e] - m_new).astype(BF16))
        return m_new, jnp.exp2(m - m_new) * (acc + prev)

    p_ref[...] = jnp.zeros(p_ref.shape, BF16)

    blk = lax.broadcasted_iota(jnp.int32, (nbp, Q), 0)
    col = lax.broadcasted_iota(jnp.int32, (nbp, Q), 1)
    elig = blk < 2 * j + (col >= T).astype(jnp.int32)
    kh = km.astype(BF16)
    kl = (km - kh.astype(F32)).astype(BF16)
    for e in range(2):
        qh = qes[e].astype(BF16)
        ql = (qes[e] - qh.astype(F32)).astype(BF16)
        gate = _dot_nt(kh, qh) + _dot_nt(kh, ql) + _dot_nt(kl, qh)
        gcur = jnp.where(elig, gate, -jnp.inf)
        selm = jnp.zeros((nbp, Q), F32)
        for _ in range(MOBA_TOPK):
            mx = jnp.max(gcur, axis=0, keepdims=True)
            first = jnp.min(jnp.where(gcur == mx, blk, nbp), axis=0, keepdims=True)
            hit = blk == first
            selm = jnp.where(hit & elig, 1.0, selm)
            gcur = jnp.where(hit, -jnp.inf, gcur)
        sel_ref[e] = selm

    ri = lax.broadcasted_iota(jnp.int32, (T, Q), 0)
    ci = lax.broadcasted_iota(jnp.int32, (T, Q), 1)
    carry = []
    for e in range(2):
        carry += [jnp.full((1, Q), NEG, F32), jnp.zeros((VT_ROWS, Q), F32)]

    own_a = (ci < T) & (ri <= ci)
    scores(2 * j, 0, lambda e: own_a | (sel_ref[e, pl.ds(2 * j, 1), :] > 0.5))
    scores(2 * j + 1, 1, lambda e: (ci >= T) & (ri <= ci - T))
    for e in range(2):
        carry[2 * e], carry[2 * e + 1] = softmax_step(0, e, carry[2 * e], carry[2 * e + 1], 0.0)
    scores(0, 0, selected(0))
    for e in range(2):
        carry[2 * e], carry[2 * e + 1] = softmax_step(1, e, carry[2 * e], carry[2 * e + 1], pv(2 * j, 0, e))

    def body(u, carry):
        carry = list(carry)
        scores(2 * u + 1, 1, selected(2 * u + 1))
        for e in range(2):
            prev = pv(jnp.where(u == 0, 2 * j + 1, 2 * u - 1), 1, e)
            carry[2 * e], carry[2 * e + 1] = softmax_step(0, e, carry[2 * e], carry[2 * e + 1], prev)
        scores(2 * u + 2, 0, selected(2 * u + 2))
        for e in range(2):
            carry[2 * e], carry[2 * e + 1] = softmax_step(1, e, carry[2 * e], carry[2 * e + 1], pv(2 * u, 0, e))
        return tuple(carry)

    fin = lax.fori_loop(0, j, body, tuple(carry))
    outs = []
    for e in range(2):
        tot = fin[2 * e + 1] + pv(jnp.where(j == 0, 1, 2 * j - 1), 1, e)
        outs.append(tot[0:hd] / tot[hd:hd + 1])
    o_ref[0] = jnp.concatenate(outs, axis=0).T


def _moba_tiles(q, kb, vt, kmean):
    b, l, w = q.shape
    tq = 2 * TILE
    assert l % tq == 0
    npair = w // LANES
    nb = kmean.shape[1]
    km = kmean.reshape(b, nb, npair, LANES).transpose(0, 2, 1, 3)
    nbp = -(-nb // 8) * 8
    km = jnp.pad(km, ((0, 0), (0, 0), (0, nbp - nb), (0, 0)))
    return pl.pallas_call(
        _moba_tile_body,
        grid=(b, npair, l // tq),
        in_specs=[pl.BlockSpec((1, tq, LANES), lambda i, p, t: (i, t, p)),
                  pl.BlockSpec((1, l, LANES), lambda i, p, t: (i, 0, p)),
                  pl.BlockSpec((1, 2 * VT_ROWS, l), lambda i, p, t: (i, p, 0)),
                  pl.BlockSpec((1, 1, nbp, LANES), lambda i, p, t: (i, p, 0, 0))],
        out_specs=pl.BlockSpec((1, tq, LANES), lambda i, p, t: (i, t, p)),
        out_shape=jax.ShapeDtypeStruct((b, l, w), F32),
        scratch_shapes=[pltpu.VMEM((2, nbp, tq), F32), pltpu.VMEM((2, 2, TILE, tq), F32),
                        pltpu.VMEM((2, 2, 1, tq), F32), pltpu.VMEM((2, 2, TILE, tq), BF16)],
        compiler_params=_params(("arbitrary", "arbitrary", "arbitrary")), name="moba_prompt")(q, kb, vt, km)


def _even_sample_proj_body(x_ref, g_ref, wz_ref, wxbc_ref, wdt_ref, wga_ref, wgb_ref, cw_ref, cb_ref, dtb_ref,
                           xbuf_ref, cfw_ref, cfb_ref, cfg_ref, cfbeta_ref, ubuf_ref,
                           z_ref, xc_ref, dt_ref, xnew_ref, c_ref, unew_ref):
    x = x_ref[...]
    hb = _rms(x, g_ref[...]).astype(BF16)
    z_ref[...] = jnp.dot(hb, wz_ref[...], preferred_element_type=F32)
    xbc = jnp.dot(hb, wxbc_ref[...], preferred_element_type=F32)
    dtr = jnp.dot(hb, wdt_ref[...], preferred_element_type=F32)
    ga = jnp.dot(hb, wga_ref[...], preferred_element_type=F32)
    gb = jnp.dot(hb, wgb_ref[...], preferred_element_type=F32)
    k1 = SSD_CONV - 1
    acc = cw_ref[k1:k1 + 1, :] * xbc
    for k in range(k1):
        acc = acc + cw_ref[k:k + 1, :] * xbuf_ref[k]
        if k > 0:
            xnew_ref[k - 1] = xbuf_ref[k]
    xnew_ref[k1 - 1] = xbc
    xc_ref[...] = _silu(acc + cb_ref[...])
    dt_ref[...] = _softplus(dtr + dtb_ref[...])
    u = ga * _sigmoid(gb)
    k2 = CF_WIDTH - 1
    cacc = cfw_ref[k2:k2 + 1, :] * u
    for k in range(k2):
        cacc = cacc + cfw_ref[k:k + 1, :] * ubuf_ref[k]
        if k > 0:
            unew_ref[k - 1] = ubuf_ref[k]
    unew_ref[k2 - 1] = u
    c32 = cacc + cfb_ref[...]
    mu = jnp.mean(c32, axis=-1, keepdims=True)
    var = jnp.mean(jnp.square(c32 - mu), axis=-1, keepdims=True)
    c_ref[...] = _silu((c32 - mu) * lax.rsqrt(var + NORM_EPS) * cfg_ref[...] + cfbeta_ref[...])


STEP_ROWS = 128
HG_STEP_ROWS = 32


def _ssd_step_body(s_ref, xb_ref, dt_ref, alog_ref, b_ref, c_ref, so_ref, yt_ref):
    rows = s_ref.shape[0]
    dt = dt_ref[...]
    decay = jnp.exp(dt * (-jnp.exp(alog_ref[...])))
    bdt = b_ref[...] * dt
    cc = c_ref[...]
    for r in range(rows):
        s_new = decay[r:r + 1, :] * s_ref[r] + xb_ref[r] * bdt[r:r + 1, :]
        so_ref[r] = s_new
        yt_ref[:, r:r + 1] = jnp.sum(s_new * cc[r:r + 1, :], axis=1, keepdims=True)


def _even_sample_out_body(x_ref, y_ref, xs_ref, z_ref, dfull_ref, sn_ref, c_ref, woy_ref, woc_ref, o_ref):
    y = (y_ref[...] + dfull_ref[...] * xs_ref[...]) * _silu(z_ref[...])
    gw = SSD_INNER // SSD_GROUPS
    out = _dot(c_ref[...], woc_ref[...])
    for grp in range(SSD_GROUPS):
        lanes = slice(grp * gw, (grp + 1) * gw)
        out = out + _dot(_rms(y[:, lanes], sn_ref[:, lanes]), woy_ref[lanes, :])
    o_ref[...] = x_ref[...] + out


def _call(body, out_shape, *args, name):
    return pl.pallas_call(body, out_shape=out_shape, compiler_params=_params(None), name=name)(*args)


def _even_sample(x, w, s_ssd, buf_ssd, buf_cf):
    nb = x.shape[0]
    f = lambda *s: jax.ShapeDtypeStruct(s, F32)
    z, xc, dt, xnew, c, unew = _call(
        _even_sample_proj_body,
        (f(nb, SSD_INNER), f(nb, SSD_CONV_DIM), f(nb, LANES), f(SSD_CONV - 1, nb, SSD_CONV_DIM), f(nb, CF_CH),
         f(CF_WIDTH - 1, nb, CF_CH)),
        x, w["g"], w["wz"], w["wxbc"], w["wdt"], w["wga"], w["wgb"], w["cw"], w["cb"], w["dtb"],
        jnp.swapaxes(buf_ssd, 0, 1), w["cfw"], w["cfb"], w["cfg"], w["cfbeta"], jnp.swapaxes(buf_cf, 0, 1),
        name="even_sample_proj")
    xs = xc[:, :SSD_INNER]
    rows = nb * SSD_HEADS
    rep_heads = SSD_HEADS // SSD_GROUPS
    grp = lambda a: jnp.repeat(a.reshape(nb, SSD_GROUPS, SSD_STATE), rep_heads, axis=1).reshape(rows, SSD_STATE)
    bm = grp(xc[:, SSD_INNER:SSD_INNER + SSD_GROUPS * SSD_STATE])
    cm = grp(xc[:, SSD_INNER + SSD_GROUPS * SSD_STATE:])
    rb = STEP_ROWS
    assert rows % rb == 0
    rowspec = pl.BlockSpec((rb, SSD_STATE), lambda i: (i, 0))
    colspec = pl.BlockSpec((SSD_HEAD_DIM, rb), lambda i: (0, i))
    sspec = pl.BlockSpec((rb, SSD_HEAD_DIM, SSD_STATE), lambda i: (i, 0, 0))
    per_row = lambda a: jnp.broadcast_to(a.reshape(rows, 1), (rows, SSD_STATE))
    s_new, yt = pl.pallas_call(
        _ssd_step_body, grid=(rows // rb,),
        in_specs=[sspec, sspec, rowspec, rowspec, rowspec, rowspec],
        out_specs=(sspec, colspec),
        out_shape=(f(rows, SSD_HEAD_DIM, SSD_STATE), f(SSD_HEAD_DIM, rows)),
        compiler_params=_params(("parallel",)), name="ssd_step")(
            s_ssd.reshape(rows, SSD_HEAD_DIM, SSD_STATE),
            jnp.broadcast_to(xs.reshape(rows, SSD_HEAD_DIM, 1), (rows, SSD_HEAD_DIM, SSD_STATE)),
            per_row(dt[:, :SSD_HEADS]), per_row(jnp.tile(w["alog"][0, :SSD_HEADS], nb)), bm, cm)
    xo = _call(_even_sample_out_body, f(nb, x.shape[1]),
               x, yt.T.reshape(nb, SSD_INNER), xs, z, w["dfull"], w["sn"], c, w["woy"], w["woc"],
               name="even_sample_out")
    return (xo, s_new.reshape(s_ssd.shape), jnp.swapaxes(xnew, 0, 1), jnp.swapaxes(unew, 0, 1))


def _odd_sample_proj_body(x_ref, g_ref, win_ref, c_ref, s1_ref, s2_ref, lb_ref,
                          q_ref, k_ref, v_ref, hq_ref, f_ref, hv_ref, hg_ref, *, layer):
    W = MOBA_WIDTH
    hb = _rms(x_ref[...], g_ref[...]).astype(BF16)
    proj = jnp.dot(hb, win_ref[...], preferred_element_type=F32)
    c, s1, s2 = c_ref[...], s1_ref[...], s2_ref[...]
    qb = _rope(proj[:, 0:W], c, s1, s2)
    kb = _rope(proj[:, W:2 * W], c, s1, s2)
    for j in range(W // LANES):
        q_ref[:, j * LANES:(j + 1) * LANES] = qb[j]
        k_ref[:, j * LANES:(j + 1) * LANES] = kb[j]
    v_ref[...] = proj[:, 2 * W:3 * W]
    lb = _hg_lower_bound(lb_ref, layer)
    hq_ref[...] = _silu(proj[:, 3 * W:3 * W + HG_WIDTH])
    f_ref[...] = lb + (1.0 - lb) * _sigmoid(proj[:, 3 * W + HG_WIDTH:3 * W + 2 * HG_WIDTH])
    hv_ref[...] = proj[:, 3 * W + 2 * HG_WIDTH:3 * W + 3 * HG_WIDTH]
    hg_ref[...] = proj[:, 3 * W + 3 * HG_WIDTH:3 * W + 4 * HG_WIDTH]


def _hg_step_body(s_ref, qt_ref, ft_ref, v_ref, so_ref, o_ref):
    qt, ft, vv = qt_ref[0], ft_ref[0], v_ref[...]
    for r in range(s_ref.shape[0]):
        fc = ft[:, r:r + 1]
        s_new = fc * s_ref[r] + (1.0 - fc) * vv[r:r + 1, :]
        so_ref[r] = s_new
        o_ref[r:r + 1, :] = jnp.sum(s_new * qt[:, r:r + 1], axis=0, keepdims=True)


def _attn_copies(idx_ref, pt_ref, ck_ref, cv_ref, kbuf, vbuf, sem, layer, seq, slot):
    ppb = MOBA_BLOCK // PAGE_SIZE
    cps = []
    for h in range(MOBA_HEADS):
        for r in range(MOBA_TOPK):
            blk = idx_ref[seq, h * MOBA_TOPK + r]
            for pg in range(ppb):
                phys = pt_ref[seq, blk * ppb + pg]
                j = r * ppb + pg
                cps.append(pltpu.make_async_copy(ck_ref.at[layer, phys, h], kbuf.at[slot, h, j], sem.at[0, slot]))
                cps.append(pltpu.make_async_copy(cv_ref.at[layer, phys, h], vbuf.at[slot, h, j], sem.at[1, slot]))
    return cps


def _sample_attn_body(idx_ref, pt_ref, ck_ref, cv_ref, q_ref, kn_ref, vn_ref, o_ref, kbuf, vbuf, sem, *,
                      layer, n_seq):
    b = pl.program_id(0)
    slot = b % 2
    scale = MOBA_HEAD_DIM ** -0.5
    nsel = MOBA_TOPK * (MOBA_BLOCK // PAGE_SIZE)

    @pl.when(b == 0)
    def _():
        for cp in _attn_copies(idx_ref, pt_ref, ck_ref, cv_ref, kbuf, vbuf, sem, layer, b, slot):
            cp.start()

    @pl.when(b + 1 < n_seq)
    def _():
        for cp in _attn_copies(idx_ref, pt_ref, ck_ref, cv_ref, kbuf, vbuf, sem, layer, b + 1, 1 - slot):
            cp.start()

    for cp in _attn_copies(idx_ref, pt_ref, ck_ref, cv_ref, kbuf, vbuf, sem, layer, b, slot):
        cp.wait()

    for h in range(MOBA_HEADS):
        qc = q_ref[0, h]
        s_self = jnp.sum(qc * kn_ref[0, h], axis=0, keepdims=True) * scale
        ss = [jnp.sum(kbuf[slot, h, j] * qc, axis=0, keepdims=True) * scale for j in range(nsel)]
        m = s_self
        for s in ss:
            m = jnp.maximum(m, jnp.max(s, axis=1, keepdims=True))
        p_self = jnp.exp(s_self - m)
        l = p_self
        acc = jnp.zeros((MOBA_HEAD_DIM, LANES), F32)
        for j in range(nsel):
            p = jnp.exp(ss[j] - m)
            l = l + jnp.sum(p, axis=1, keepdims=True)
            acc = acc + vbuf[slot, h, j] * p
        o_ref[0, h] = (vn_ref[0, h] * p_self + jnp.sum(acc, axis=1, keepdims=True)) / l


def _odd_sample_out_body(x_ref, att_ref, o_ref, hg_ref, gn_ref, woa_ref, woh_ref, xo_ref):
    out = _dot(att_ref[...], woa_ref[...])
    o = o_ref[...]
    gated = []
    for h in range(HG_HEADS):
        lanes = slice(h * HG_V, (h + 1) * HG_V)
        gated.append(_rms(o[:, lanes], gn_ref[:, lanes]) * _silu(hg_ref[:, lanes]))
    out = out + _dot(jnp.concatenate(gated, axis=1), woh_ref[...])
    xo_ref[...] = x_ref[...] + out


def _odd_sample_pre(x, w, layer, s_hg, page_table, past_len):
    nb = x.shape[0]
    f = lambda *s: jax.ShapeDtypeStruct(s, F32)
    n_pages = page_table.shape[1]
    ppb = MOBA_BLOCK // PAGE_SIZE
    nblk = n_pages // ppb
    assert n_pages % ppb == 0 and nblk >= MOBA_TOPK and nblk < LANES
    assert n_pages % GATE_CHUNK == 0 and GATE_CHUNK % ppb == 0
    c, s1, s2 = _rope_tables(np.full((1,), past_len))
    q, k, v, hq, fg, hv, hg = _call(
        functools.partial(_odd_sample_proj_body, layer=layer),
        tuple(f(nb, MOBA_WIDTH) for _ in range(7)),
        x, w["g"], w["win"], c, s1, s2, w["lb"], name="odd_sample_proj")

    rows = nb * HG_HEADS
    rb = HG_STEP_ROWS
    assert rows % rb == 0
    cols = lambda a: _pad_lanes(jnp.swapaxes(a.reshape(rows // rb, rb, HG_K), 1, 2))
    sspec = pl.BlockSpec((rb, HG_K, HG_V), lambda i: (i, 0, 0))
    cspec = pl.BlockSpec((1, HG_K, LANES), lambda i: (i, 0, 0))
    rspec = pl.BlockSpec((rb, HG_V), lambda i: (i, 0))
    s_new, o = pl.pallas_call(
        _hg_step_body, grid=(rows // rb,),
        in_specs=[sspec, cspec, cspec, rspec], out_specs=(sspec, rspec),
        out_shape=(f(rows, HG_K, HG_V), f(rows, HG_V)),
        compiler_params=_params(("parallel",)), name="hg_step")(
            s_hg.reshape(rows, HG_K, HG_V), cols(hq), cols(fg), hv.reshape(rows, HG_V))

    hd = MOBA_HEAD_DIM
    col = lambda a: jnp.broadcast_to(a.reshape(nb, MOBA_HEADS, hd, 1), (nb, MOBA_HEADS, hd, LANES))
    return dict(qcol=col(q), kcol=col(k), vcol=col(v), kn=k.reshape(nb, MOBA_HEADS, 1, hd),
                vn=v.reshape(nb, MOBA_HEADS, 1, hd), o=o.reshape(nb, HG_WIDTH), hg=hg,
                state=s_new.reshape(s_hg.shape))


def _odd_sample_post(x, w, pre, idx, cache_kt, cache_vt, cache_layer, page_table):
    nb = x.shape[0]
    f = lambda *s: jax.ShapeDtypeStruct(s, F32)
    hd = MOBA_HEAD_DIM
    nsel = MOBA_TOPK * (MOBA_BLOCK // PAGE_SIZE)
    colspec = pl.BlockSpec((1, MOBA_HEADS, hd, LANES), lambda i, *_: (i, 0, 0, 0))
    att = pl.pallas_call(
        functools.partial(_sample_attn_body, layer=cache_layer, n_seq=nb),
        grid_spec=pltpu.PrefetchScalarGridSpec(
            num_scalar_prefetch=2, grid=(nb,),
            in_specs=[pl.BlockSpec(memory_space=pl.ANY), pl.BlockSpec(memory_space=pl.ANY),
                      colspec, colspec, colspec],
            out_specs=colspec,
            scratch_shapes=[pltpu.VMEM((2, MOBA_HEADS, nsel, hd, PAGE_SIZE), F32),
                            pltpu.VMEM((2, MOBA_HEADS, nsel, hd, PAGE_SIZE), F32),
                            pltpu.SemaphoreType.DMA((2, 2))]),
        out_shape=f(nb, MOBA_HEADS, hd, LANES),
        compiler_params=_params(("arbitrary",)), name="sample_attn")(
            jnp.swapaxes(idx[:, :MOBA_TOPK, :MOBA_HEADS], 1, 2).reshape(nb, MOBA_HEADS * MOBA_TOPK),
            page_table, cache_kt, cache_vt,
            pre["qcol"], pre["kcol"], pre["vcol"])
    att = att[:, :, :, 0].reshape(nb, MOBA_WIDTH)
    return _call(_odd_sample_out_body, f(nb, x.shape[1]),
                 x, att, pre["o"], pre["hg"], w["gn"], w["woa"], w["woh"], name="odd_sample_out")


def _row(a):
    return a.reshape(1, -1).astype(F32)


def _pad_lanes(a, n=LANES):
    return jnp.pad(a, [(0, 0)] * (a.ndim - 1) + [(0, n - a.shape[-1])])


def kernel(x_prompt, x_sample, state_ssd, state_ssd_conv, state_cf_conv, cache_k, cache_v, page_table, state_hg,
           ffn1_norm, ffn1_w_gu, ffn1_w_down, mix_norm, ffn2_norm, ffn2_w_gu, ffn2_w_down, final_norm,
           even_w_in, ssd_conv_w, ssd_conv_b, ssd_dt_bias, ssd_a_log, ssd_d, ssd_norm,
           cf_dw_w, cf_dw_b, cf_ln_g, cf_ln_b, even_w_out,
           odd_w_in, hg_lower_bound, hg_norm, odd_w_out):
    depth = ffn1_norm.shape[0]
    bp, lp, d = x_prompt.shape
    nb = x_sample.shape[0]
    assert x_sample.shape[1] == 1 and lp % (2 * TILE) == 0
    past_len = page_table.shape[1] * PAGE_SIZE

    ffn_bf16 = [a.astype(BF16) for a in (ffn1_w_gu, ffn1_w_down, ffn2_w_gu, ffn2_w_down)]

    def ffn_w(norm, w_gu, w_down, l):
        return (_row(norm[l]), w_gu, w_down, l)

    def even_w(e, l):
        wi = even_w_in[e]
        o1 = SSD_INNER
        o2 = o1 + SSD_CONV_DIM
        o3 = o2 + SSD_HEADS
        sel = (np.arange(LANES)[:, None] == np.arange(SSD_INNER)[None, :] // SSD_HEAD_DIM).astype(np.float32)
        return dict(
            g=_row(mix_norm[l]), wz=wi[:, :o1].astype(BF16), wxbc=wi[:, o1:o2].astype(BF16),
            wdt=_pad_lanes(wi[:, o2:o3]).astype(BF16), wga=wi[:, o3:o3 + CF_CH].astype(BF16),
            wgb=wi[:, o3 + CF_CH:].astype(BF16), cw=ssd_conv_w[e], cb=_row(ssd_conv_b[e]),
            dtb=_pad_lanes(_row(ssd_dt_bias[e])), alog=_pad_lanes(_row(ssd_a_log[e])),
            dfull=_row(jnp.repeat(ssd_d[e], SSD_HEAD_DIM)), sn=_row(ssd_norm[e]), sel=jnp.asarray(sel, BF16),
            cfw=cf_dw_w[e], cfb=_row(cf_dw_b[e]), cfg=_row(cf_ln_g[e]), cfbeta=_row(cf_ln_b[e]),
            woy=even_w_out[e, :SSD_INNER].astype(BF16), woc=even_w_out[e, SSD_INNER:].astype(BF16))

    def odd_w(o, l):
        return dict(g=_row(mix_norm[l]), win=odd_w_in[o].astype(BF16), lb=hg_lower_bound.astype(F32),
                    gn=_row(hg_norm[o]), woa=odd_w_out[o, :MOBA_WIDTH].astype(BF16),
                    woh=odd_w_out[o, MOBA_WIDTH:].astype(BF16))

    cache_kt = jnp.swapaxes(cache_k, -1, -2)
    cache_vt = jnp.swapaxes(cache_v, -1, -2)
    layer_w = []
    for l in range(depth):
        mix = even_w(l // 2, l) if l % 2 == 0 else odd_w(l // 2, l)
        layer_w.append((ffn_w(ffn1_norm, ffn_bf16[0], ffn_bf16[1], l), mix,
                        ffn_w(ffn2_norm, ffn_bf16[2], ffn_bf16[3], l), _row(final_norm) if l == depth - 1 else None))

    steps = bp * lp // TILE
    n_chunks = page_table.shape[1] // GATE_CHUNK
    host_layer = depth - 1 if depth % 2 == 0 else depth - 2
    if host_layer < 1 or steps % n_chunks or 2 * depth * (steps // n_chunks) != nb:
        host_layer = None

    def sample_group(hosted_gate):
        xs = x_sample.reshape(nb, d)
        ssd_s, sc_s, cf_s, k_s, v_s, hg_s = [], [], [], [], [], []
        for l, (w1, w, w2, fin) in enumerate(layer_w):
            xs = _ffn(xs, w1, tm=nb)
            if l % 2 == 0:
                e = l // 2
                xs, st, xt, ut = _even_sample(xs, w, state_ssd[e], state_ssd_conv[e], state_cf_conv[e])
                ssd_s.append(st)
                sc_s.append(xt)
                cf_s.append(ut)
            else:
                o = l // 2
                pre = _odd_sample_pre(xs, w, l, state_hg[o], page_table, past_len)
                if l == host_layer:
                    idx = hosted_gate(pre["qcol"], o)
                else:
                    idx = _sample_gate(page_table, cache_kt, pre["qcol"], o)
                xs = _odd_sample_post(xs, w, pre, idx, cache_kt, cache_vt, o, page_table)
                k_s.append(pre["kn"])
                v_s.append(pre["vn"])
                hg_s.append(pre["state"])
            xs = _ffn(xs, w2, tm=nb, final_g=fin)
        return xs.reshape(nb, 1, d), ssd_s, sc_s, cf_s, k_s, v_s, hg_s

    def prompt_group(hosted):
        xp = x_prompt.reshape(bp * lp, d)
        ssd_p, sc_p, cf_p, k_p, v_p, hg_p, ids = [], [], [], [], [], [], []

        def ffn(x, *args, **kw):
            if hosted is None:
                return _ffn(x, *args, **kw)
            qcol, cache_layer = hosted
            x, part = _ffn(x, *args, gate=(page_table, cache_kt, qcol, cache_layer, len(ids) * (steps // n_chunks)),
                           **kw)
            ids.append(part)
            return x

        for l, (w1, w, w2, fin) in enumerate(layer_w):
            xp = ffn(xp, w1, tm=TILE)
            if l % 2 == 0:
                xp3, st, xt, ut = _even_prompt(xp.reshape(bp, lp, d), w)
                xp = xp3.reshape(bp * lp, d)
                ssd_p.append(st)
                sc_p.append(xt)
                cf_p.append(ut)
                xp = ffn(xp, w2, tm=TILE, final_g=fin)
            else:
                q, kb, vt, kmean, ko, vo, ohg, hst = _odd_prompt_proj(xp.reshape(bp, lp, d), w, l)
                att = _moba_tiles(q, kb, vt, kmean[:, :, 0, :])
                k_p.append(jnp.swapaxes(ko, -1, -2))
                v_p.append(jnp.swapaxes(vo, -1, -2))
                hg_p.append(hst)
                xp = ffn(xp, w2, tm=TILE, final_g=fin, pre=(att.reshape(bp * lp, MOBA_WIDTH),
                                                            ohg.reshape(bp * lp, HG_WIDTH), w["woa"], w["woh"]))
        return (xp.reshape(bp, lp, d), ssd_p, sc_p, cf_p, k_p, v_p, hg_p), ids

    prompt_result = []

    def hosted_gate(qcol, cache_layer):
        out, ids = prompt_group((qcol, cache_layer))
        prompt_result.append(out)
        return jnp.concatenate(ids, axis=0)

    sample_out = sample_group(hosted_gate)
    prompt_out = prompt_result[0] if prompt_result else prompt_group(None)[0]
    y_p, ssd_p, sc_p, cf_p, k_p, v_p, hg_p = prompt_out
    y_s, ssd_s, sc_s, cf_s, k_s, v_s, hg_s = sample_out
    st = jnp.stack
    return (y_p, y_s, st(ssd_p), st(ssd_s), st(sc_p), st(sc_s), st(cf_p), st(cf_s),
            st(k_p), st(k_s), st(v_p), st(v_s), st(hg_p), st(hg_s))
```

```python
import functools
import math

import numpy as np
import jax
import jax.numpy as jnp
from jax import lax
from jax.experimental import pallas as pl
from jax.experimental.pallas import tpu as pltpu

F32 = jnp.float32
BF16 = jnp.bfloat16

SSD_HEADS = 16
SSD_HEAD_DIM = 64
SSD_INNER = SSD_HEADS * SSD_HEAD_DIM
SSD_GROUPS = 2
SSD_STATE = 64
SSD_CONV = 4
SSD_CONV_DIM = SSD_INNER + 2 * SSD_GROUPS * SSD_STATE
CF_CH = 512
CF_WIDTH = 31
MOBA_HEADS = 8
MOBA_HEAD_DIM = 64
MOBA_WIDTH = MOBA_HEADS * MOBA_HEAD_DIM
MOBA_BLOCK = 256
MOBA_TOPK = 3
ROPE_DIM = MOBA_HEAD_DIM // 4
ROPE_THETA = 500000.0
HG_HEADS = 4
HG_K = 128
HG_V = 128
HG_WIDTH = HG_HEADS * HG_K
PAGE_SIZE = 128
NORM_EPS = 1e-6
NEG = -1e30

LANES = 128
SSD_CHUNK = 128
TILE = 256
VT_ROWS = MOBA_HEAD_DIM + 16
VMEM_LIMIT = 56 * 1024 * 1024


def _sigmoid(x):
    return 1.0 / (1.0 + jnp.exp(-x))


def _silu(x):
    return x * _sigmoid(x)


def _softplus(x):
    return jnp.maximum(x, 0.0) + jnp.log1p(jnp.exp(-jnp.abs(x)))


def _rms(x, g):
    return x * lax.rsqrt(jnp.mean(x * x, axis=-1, keepdims=True) + NORM_EPS) * g


def _dot(a, b):
    return jnp.dot(a.astype(BF16), b.astype(BF16), preferred_element_type=F32)


def _dot_nt(a, b):
    return lax.dot_general(a.astype(BF16), b.astype(BF16), (((1,), (1,)), ((), ())),
                           preferred_element_type=F32)


def _split3(a):
    hi = a.astype(BF16)
    r = a - hi.astype(F32)
    mid = r.astype(BF16)
    lo = (r - mid.astype(F32)).astype(BF16)
    return hi, mid, lo


def _dot3_l(a, m):
    hi, mid, lo = _split3(a)
    return (jnp.dot(hi, m, preferred_element_type=F32) + jnp.dot(mid, m, preferred_element_type=F32)
            + jnp.dot(lo, m, preferred_element_type=F32))


def _dot3_r(m, parts):
    hi, mid, lo = parts
    return (jnp.dot(m, hi, preferred_element_type=F32) + jnp.dot(m, mid, preferred_element_type=F32)
            + jnp.dot(m, lo, preferred_element_type=F32))


def _const_spec(shape):
    n = len(shape)
    return pl.BlockSpec(shape, lambda *_: (0,) * n, pipeline_mode=pl.Buffered(1))


def _params(sem):
    return pltpu.CompilerParams(dimension_semantics=sem, vmem_limit_bytes=VMEM_LIMIT)


GATE_CHUNK = 16


def _gate_copies(pt_ref, ck_ref, kbuf, sem, layer, seq, chunk, slot):
    return [pltpu.make_async_copy(ck_ref.at[layer, pt_ref[seq, chunk * GATE_CHUNK + j]], kbuf.at[slot, j],
                                  sem.at[slot]) for j in range(GATE_CHUNK)]


def _gate_fetch(pt_ref, ck_ref, kbuf, sem, *, layer, seq0, n_chunks, n_steps):
    step = pl.program_id(0)
    slot = step % 2
    seq = seq0 + step // n_chunks
    c = step % n_chunks

    @pl.when(step == 0)
    def _():
        for cp in _gate_copies(pt_ref, ck_ref, kbuf, sem, layer, seq, c, slot):
            cp.start()

    nxt = step + 1

    @pl.when(nxt < n_steps)
    def _():
        for cp in _gate_copies(pt_ref, ck_ref, kbuf, sem, layer, seq0 + nxt // n_chunks, nxt % n_chunks, 1 - slot):
            cp.start()

    for cp in _gate_copies(pt_ref, ck_ref, kbuf, sem, layer, seq, c, slot):
        cp.wait()
    return slot, c


GATE_CHUNK_BLOCKS = GATE_CHUNK // (MOBA_BLOCK // PAGE_SIZE)


def _gate_accumulate(kbuf, q_ref, g_ref, slot, c):
    ppb = MOBA_BLOCK // PAGE_SIZE
    nbc = GATE_CHUNK_BLOCKS
    lane = lax.broadcasted_iota(jnp.int32, (nbc, LANES), 1)
    sub = lax.broadcasted_iota(jnp.int32, (nbc, LANES), 0)
    rows = jnp.zeros((nbc, LANES), F32)
    for h in range(MOBA_HEADS):
        q = q_ref[0, h]
        tile = jnp.zeros((nbc, LANES), F32)
        for jb in range(nbc):
            ksum = kbuf[slot, jb * ppb, h]
            for r in range(1, ppb):
                ksum = ksum + kbuf[slot, jb * ppb + r, h]
            tile = jnp.where(sub == jb, jnp.sum(ksum * q, axis=0, keepdims=True), tile)
        rows = jnp.where(lane == h, jnp.sum(tile, axis=1, keepdims=True), rows)
    g_ref[pl.ds(pl.multiple_of(c * nbc, nbc), nbc), :] = rows


def _gate_finish(g_ref, idx_ref, c, n_chunks):
    @pl.when(c == n_chunks - 1)
    def _():
        nblk = g_ref.shape[0]
        row = lax.broadcasted_iota(jnp.int32, (nblk, LANES), 0)
        rank = lax.broadcasted_iota(jnp.int32, idx_ref.shape[1:], 0)
        gcur = g_ref[...] * (1.0 / MOBA_BLOCK)
        out = jnp.zeros(idx_ref.shape[1:], jnp.int32)
        for r in range(MOBA_TOPK):
            mx = jnp.max(gcur, axis=0, keepdims=True)
            first = jnp.min(jnp.where(gcur == mx, row, nblk), axis=0, keepdims=True)
            out = jnp.where(rank == r, first, out)
            gcur = jnp.where(row == first, -jnp.inf, gcur)
        idx_ref[0] = out


def _gate_scratch(n_chunks):
    return [pltpu.VMEM((2, GATE_CHUNK, MOBA_HEADS, MOBA_HEAD_DIM, PAGE_SIZE), F32), pltpu.SemaphoreType.DMA((2,)),
            pltpu.VMEM((n_chunks * GATE_CHUNK_BLOCKS, LANES), F32)]


def _sample_gate_body(pt_ref, ck_ref, q_ref, idx_ref, kbuf, sem, g_ref, **job):
    slot, c = _gate_fetch(pt_ref, ck_ref, kbuf, sem, **job)
    _gate_accumulate(kbuf, q_ref, g_ref, slot, c)
    _gate_finish(g_ref, idx_ref, c, job["n_chunks"])


def _sample_gate(page_table, cache_kt, qcol, layer):
    nb, n_pages = page_table.shape
    n_chunks = n_pages // GATE_CHUNK
    job = dict(layer=layer, seq0=0, n_chunks=n_chunks, n_steps=nb * n_chunks)
    return pl.pallas_call(
        functools.partial(_sample_gate_body, **job),
        grid_spec=pltpu.PrefetchScalarGridSpec(
            num_scalar_prefetch=1, grid=(nb * n_chunks,),
            in_specs=[pl.BlockSpec(memory_space=pl.ANY),
                      pl.BlockSpec((1,) + qcol.shape[1:], lambda s, pt: (s // n_chunks, 0, 0, 0))],
            out_specs=pl.BlockSpec((1, MOBA_HEADS, LANES), lambda s, pt: (s // n_chunks, 0, 0)),
            scratch_shapes=_gate_scratch(n_chunks)),
        out_shape=jax.ShapeDtypeStruct((nb, MOBA_HEADS, LANES), jnp.int32),
        compiler_params=_params(("arbitrary",)), name="sample_gate")(page_table, cache_kt, qcol)


def _ffn_body(*refs, pre, final, job):
    it = iter(refs)
    if job is not None:
        pt_ref = next(it)
    x_ref = next(it)
    if pre:
        a_ref, b_ref, wa_ref, wb_ref = next(it), next(it), next(it), next(it)
    g_ref, wgu_ref, wd_ref = next(it), next(it), next(it)
    if final:
        fg_ref = next(it)
    if job is not None:
        ck_ref, q_ref = next(it), next(it)
    o_ref = next(it)
    if job is not None:
        idx_ref, kbuf, sem, gacc_ref = next(it), next(it), next(it), next(it)
        slot, c = _gate_fetch(pt_ref, ck_ref, kbuf, sem, **job)
        _gate_accumulate(kbuf, q_ref, gacc_ref, slot, c)
    d_ff = wd_ref.shape[1]
    x = x_ref[...]
    if pre:
        x = x + _dot(a_ref[...], wa_ref[...]) + _dot(b_ref[...], wb_ref[...])
    hb = _rms(x, g_ref[...]).astype(BF16)
    g = jnp.dot(hb, wgu_ref[0, :, :d_ff], preferred_element_type=F32)
    u = jnp.dot(hb, wgu_ref[0, :, d_ff:], preferred_element_type=F32)
    act = (_silu(g) * u).astype(BF16)
    y = x + 0.5 * jnp.dot(act, wd_ref[0], preferred_element_type=F32)
    if final:
        y = _rms(y, fg_ref[...])
    o_ref[...] = y
    if job is not None:
        _gate_finish(gacc_ref, idx_ref, c, job["n_chunks"])


def _ffn(x, w, *, tm, pre=None, final_g=None, gate=None):
    m, d = x.shape
    g, wgu, wd, layer = w
    steps = m // tm
    row = lambda c: pl.BlockSpec((tm, c), lambda i, *_: (i, 0))
    layer_spec = lambda a: pl.BlockSpec((1,) + a.shape[1:], lambda i, *_: (layer, 0, 0),
                                        pipeline_mode=pl.Buffered(1))
    args, specs = [x], [row(d)]
    if pre is not None:
        a, b, wa, wb = pre
        args += [a, b, wa, wb]
        specs += [row(a.shape[1]), row(b.shape[1]), _const_spec(wa.shape), _const_spec(wb.shape)]
    args += [g, wgu, wd]
    specs += [_const_spec(g.shape), layer_spec(wgu), layer_spec(wd)]
    if final_g is not None:
        args.append(final_g)
        specs.append(_const_spec(final_g.shape))
    out_shape = jax.ShapeDtypeStruct((m, d), F32)
    if gate is None:
        return pl.pallas_call(
            functools.partial(_ffn_body, pre=pre is not None, final=final_g is not None, job=None),
            grid=(steps,), in_specs=specs, out_specs=row(d), out_shape=out_shape,
            compiler_params=_params(("parallel",)), name="ffn")(*args)
    page_table, cache_kt, qcol, cache_layer, seq0 = gate
    n_chunks = page_table.shape[1] // GATE_CHUNK
    n_seq = steps // n_chunks
    assert n_seq * n_chunks == steps
    job = dict(layer=cache_layer, seq0=seq0, n_chunks=n_chunks, n_steps=steps)
    specs += [pl.BlockSpec(memory_space=pl.ANY),
              pl.BlockSpec((1,) + qcol.shape[1:], lambda i, pt: (seq0 + i // n_chunks, 0, 0, 0))]
    return pl.pallas_call(
        functools.partial(_ffn_body, pre=pre is not None, final=final_g is not None, job=job),
        grid_spec=pltpu.PrefetchScalarGridSpec(
            num_scalar_prefetch=1, grid=(steps,), in_specs=specs,
            out_specs=(row(d), pl.BlockSpec((1, MOBA_HEADS, LANES), lambda i, pt: (i // n_chunks, 0, 0))),
            scratch_shapes=_gate_scratch(n_chunks)),
        out_shape=(out_shape, jax.ShapeDtypeStruct((n_seq, MOBA_HEADS, LANES), jnp.int32)),
        compiler_params=_params(("arbitrary",)), name="ffn_gate")(page_table, *args, cache_kt, qcol)


def _even_prompt_body(x_ref, g_ref, wz_ref, wxbc_ref, wdt_ref, wga_ref, wgb_ref,
                      cw_ref, cb_ref, dtb_ref, alog_ref, dfull_ref, sn_ref, sel_ref,
                      cfw_ref, cfb_ref, cfg_ref, cfbeta_ref, woy_ref, woc_ref,
                      xo_ref, st_ref, xtail_ref, utail_ref,
                      xbuf, ubuf, s_ref, ybuf, *, nt):
    t = pl.program_id(1)
    T = TILE
    C = SSD_CHUNK

    @pl.when(t == 0)
    def _():
        xbuf[0:8, :] = jnp.zeros((8, SSD_CONV_DIM), F32)
        ubuf[0:32, :] = jnp.zeros((32, CF_CH), F32)
        s_ref[...] = jnp.zeros(s_ref.shape, F32)

    x = x_ref[0]
    hb = _rms(x, g_ref[...]).astype(BF16)
    z = jnp.dot(hb, wz_ref[...], preferred_element_type=F32)
    xbc = jnp.dot(hb, wxbc_ref[...], preferred_element_type=F32)
    dtr = jnp.dot(hb, wdt_ref[...], preferred_element_type=F32)
    ga = jnp.dot(hb, wga_ref[...], preferred_element_type=F32)
    gb = jnp.dot(hb, wgb_ref[...], preferred_element_type=F32)

    xbuf[8:8 + T, :] = xbc
    k0 = SSD_CONV - 1
    acc = cw_ref[k0:k0 + 1, :] * xbuf[5 + k0:5 + k0 + T, :]
    for k in range(k0):
        acc = acc + cw_ref[k:k + 1, :] * xbuf[5 + k:5 + k + T, :]
    tail8 = xbuf[T:T + 8, :]
    xtail_ref[0] = tail8
    xbuf[0:8, :] = tail8
    xc = _silu(acc + cb_ref[...])
    xs = xc[:, 0:SSD_INNER]
    bm = xc[:, SSD_INNER:SSD_INNER + LANES]
    cm = xc[:, SSD_INNER + LANES:SSD_INNER + 2 * LANES]
    dt = _softplus(dtr + dtb_ref[...])
    dta = dt * (-jnp.exp(alog_ref[...]))
    sel = sel_ref[...]

    ri = lax.broadcasted_iota(jnp.int32, (C, C), 0)
    ci = lax.broadcasted_iota(jnp.int32, (C, C), 1)
    tri = ri >= ci
    trib = tri.astype(BF16)
    low = ci < SSD_HEAD_DIM

    for c in range(T // C):
        r0 = c * C
        xs_c, bm_c, cm_c = xs[r0:r0 + C], bm[r0:r0 + C], cm[r0:r0 + C]
        dt_c = dt[r0:r0 + C]
        cum = _dot3_r(trib, _split3(dta[r0:r0 + C]))
        cum_t = cum.T
        dt_t = dt_c.T
        cum_last = cum[C - 1:C, :]
        ecum_full = _dot3_l(jnp.exp(cum), sel)
        tail_full = _dot3_l(jnp.exp(cum_last - cum) * dt_c, sel)
        dlast_full = _dot3_l(jnp.broadcast_to(jnp.exp(cum_last), (8, LANES)), sel)[0:1]
        bm_t = bm_c.T
        cmb = cm_c.astype(BF16)
        for grp in range(SSD_GROUPS):
            bm_tg = jnp.where((ri // SSD_STATE) == grp, bm_t, 0.0).astype(BF16)
            gmat = jnp.dot(cmb, bm_tg, preferred_element_type=F32)
            pairs = SSD_HEADS // SSD_GROUPS // 2
            for pp in range(pairs):
                p = grp * pairs + pp
                lanes = slice(p * LANES, (p + 1) * LANES)
                xp = xs_c[:, lanes]
                xpb = xp.astype(BF16)
                ys = []
                for e in range(2):
                    h = 2 * p + e
                    seg = cum[:, h:h + 1] - cum_t[h:h + 1, :]
                    wm = gmat * jnp.exp(jnp.where(tri, seg, NEG)) * dt_t[h:h + 1, :]
                    ys.append(jnp.dot(wm.astype(BF16), xpb, preferred_element_type=F32))
                yp = jnp.where(low, ys[0], ys[1])
                sp = s_ref[p]
                yp = yp + jnp.dot(cmb, sp.astype(BF16), preferred_element_type=F32) * ecum_full[:, lanes]
                s_ref[p] = dlast_full[:, lanes] * sp + jnp.dot(
                    bm_tg, (xp * tail_full[:, lanes]).astype(BF16), preferred_element_type=F32)
                ybuf[r0:r0 + C, lanes] = yp

    @pl.when(t == nt - 1)
    def _():
        for p in range(SSD_HEADS // 2):
            st_ref[0, p] = s_ref[p].T

    y = (ybuf[...] + dfull_ref[...] * xs) * _silu(z)
    gw = SSD_INNER // SSD_GROUPS
    yn = []
    for grp in range(SSD_GROUPS):
        yg = y[:, grp * gw:(grp + 1) * gw]
        yn.append(_rms(yg, sn_ref[:, grp * gw:(grp + 1) * gw]).astype(BF16))

    u = ga * _sigmoid(gb)
    ubuf[32:32 + T, :] = u
    base = 32 - (CF_WIDTH - 1)
    cacc = None
    for r in range(8):
        offs = [o for o in range(base, base + CF_WIDTH) if o % 8 == r]
        if not offs:
            continue
        ur = ubuf[r:max(offs) + T, :]
        part = None
        for o in offs:
            term = cfw_ref[o - base:o - base + 1, :] * ur[o - r:o - r + T]
            part = term if part is None else part + term
        cacc = part if cacc is None else cacc + part
    tail32 = ubuf[T:T + 32, :]
    utail_ref[0] = tail32
    ubuf[0:32, :] = tail32
    c32 = cacc + cfb_ref[...]
    mu = jnp.mean(c32, axis=-1, keepdims=True)
    var = jnp.mean(jnp.square(c32 - mu), axis=-1, keepdims=True)
    c32 = _silu((c32 - mu) * lax.rsqrt(var + NORM_EPS) * cfg_ref[...] + cfbeta_ref[...])

    out = jnp.dot(c32.astype(BF16), woc_ref[...], preferred_element_type=F32)
    for grp in range(SSD_GROUPS):
        out = out + jnp.dot(yn[grp], woy_ref[grp * gw:(grp + 1) * gw, :], preferred_element_type=F32)
    xo_ref[0] = x + out


def _even_prompt(x, w):
    b, l, d = x.shape
    nt = l // TILE
    consts = [w[k] for k in ("g", "wz", "wxbc", "wdt", "wga", "wgb", "cw", "cb", "dtb", "alog", "dfull", "sn",
                             "sel", "cfw", "cfb", "cfg", "cfbeta", "woy", "woc")]
    out_shape = (jax.ShapeDtypeStruct((b, l, d), F32),
                 jax.ShapeDtypeStruct((b, SSD_HEADS // 2, LANES, LANES), F32),
                 jax.ShapeDtypeStruct((b, 8, SSD_CONV_DIM), F32),
                 jax.ShapeDtypeStruct((b, 32, CF_CH), F32))
    out_specs = (pl.BlockSpec((1, TILE, d), lambda i, t: (i, t, 0)),
                 pl.BlockSpec((1, SSD_HEADS // 2, LANES, LANES), lambda i, t: (i, 0, 0, 0)),
                 pl.BlockSpec((1, 8, SSD_CONV_DIM), lambda i, t: (i, 0, 0)),
                 pl.BlockSpec((1, 32, CF_CH), lambda i, t: (i, 0, 0)))
    xo, st, xtail, utail = pl.pallas_call(
        functools.partial(_even_prompt_body, nt=nt),
        grid=(b, nt),
        in_specs=[pl.BlockSpec((1, TILE, d), lambda i, t: (i, t, 0))] + [_const_spec(c.shape) for c in consts],
        out_specs=out_specs, out_shape=out_shape,
        scratch_shapes=[pltpu.VMEM((TILE + 8, SSD_CONV_DIM), F32), pltpu.VMEM((TILE + 32, CF_CH), F32),
                        pltpu.VMEM((SSD_HEADS // 2, LANES, LANES), F32), pltpu.VMEM((TILE, SSD_INNER), F32)],
        compiler_params=_params(("arbitrary", "arbitrary")), name="even_prompt")(x, *consts)
    half = SSD_HEADS // 2 // SSD_GROUPS
    st = st.reshape(b, SSD_HEADS // 2, 2, SSD_HEAD_DIM, SSD_GROUPS, SSD_STATE)
    state = jnp.concatenate([st[:, grp * half:(grp + 1) * half, :, :, grp, :] for grp in range(SSD_GROUPS)], axis=1)
    state = state.reshape(b, SSD_HEADS, SSD_HEAD_DIM, SSD_STATE)
    return xo, state, xtail[:, 8 - (SSD_CONV - 1):], utail[:, 32 - (CF_WIDTH - 1):]


def _hg_levels(T):
    t = np.arange(T)
    le = (t[None, :] <= t[:, None]).astype(np.float32)
    ds, ms = [], []
    m = 1
    while m < T:
        rb = (t // (2 * m)) * 2 * m + m - 1
        if m < 8:
            ds.append(le - (t[None, :] <= rb[:, None]).astype(np.float32))
        same = (t[:, None] // (2 * m)) == (t[None, :] // (2 * m))
        ms.append((same & ((t[:, None] % (2 * m)) >= m) & ((t[None, :] % (2 * m)) < m)).astype(np.float32))
        m *= 2
    return (jnp.asarray(le, BF16), jnp.asarray(np.stack(ds), BF16), jnp.asarray(np.stack(ms), BF16))


def _rope_tables(pos):
    half = ROPE_DIM // 2
    inv = ROPE_THETA ** (-np.arange(half, dtype=np.float64) / half)
    ang = pos.astype(np.float64)[:, None] * inv[None, :]
    cos, sin = np.cos(ang), np.sin(ang)
    n = pos.shape[0]
    one = np.ones((n, MOBA_HEAD_DIM - ROPE_DIM))
    zero = np.zeros((n, MOBA_HEAD_DIM - ROPE_DIM))
    zh = np.zeros((n, half))
    c = np.concatenate([cos, cos, one], axis=1)
    s1 = np.concatenate([-sin, zh, zero], axis=1)
    s2 = np.concatenate([zh, sin, zero], axis=1)
    return tuple(jnp.asarray(np.concatenate([a, a], axis=1), F32) for a in (c, s1, s2))


def _rope(x, c, s1, s2):
    outs = []
    for j in range(x.shape[1] // LANES):
        blk = x[:, j * LANES:(j + 1) * LANES]
        outs.append(blk * c + pltpu.roll(blk, LANES - ROPE_DIM // 2, 1) * s1 + pltpu.roll(blk, ROPE_DIM // 2, 1) * s2)
    return outs


def _hg_lower_bound(lb_ref, layer):
    a = lb_ref[...]
    mx = jnp.max(a, axis=0, keepdims=True)
    e = jnp.exp(a - mx)
    return jnp.sum(e[1:layer + 1], axis=0, keepdims=True) / jnp.sum(e, axis=0, keepdims=True)


def _odd_prompt_body(x_ref, g_ref, win_ref, c_ref, s1_ref, s2_ref, lb_ref, gn_ref, tril_ref, dm_ref, mm_ref,
                     q_ref, kb_ref, vt_ref, km_ref, ko_ref, vo_ref, o_ref, hs_ref,
                     st_ref, *, nt, layer):
    t = pl.program_id(1)
    T = TILE
    W = MOBA_WIDTH

    @pl.when(t == 0)
    def _():
        st_ref[...] = jnp.zeros(st_ref.shape, F32)

    x = x_ref[0]
    hb = _rms(x, g_ref[...]).astype(BF16)
    proj = jnp.dot(hb, win_ref[...], preferred_element_type=F32)

    c, s1, s2 = c_ref[...], s1_ref[...], s2_ref[...]
    qb = _rope(proj[:, 0:W], c, s1, s2)
    kb = _rope(proj[:, W:2 * W], c, s1, s2)
    v = proj[:, 2 * W:3 * W]
    for j in range(W // LANES):
        lanes = slice(j * LANES, (j + 1) * LANES)
        q_ref[0, :, lanes] = qb[j]
        kb_ref[0, :, lanes] = kb[j].astype(BF16)
        km_ref[0, 0, :, lanes] = jnp.mean(kb[j], axis=0, keepdims=True)
        kt = kb[j].T
        vt = v[:, lanes].T
        for e in range(2):
            ko_ref[0, 2 * j + e] = kt[e * 64:(e + 1) * 64, :]
            vo_ref[0, 2 * j + e] = vt[e * 64:(e + 1) * 64, :]
            r0 = (2 * j + e) * VT_ROWS
            vt_ref[0, r0:r0 + 64, :] = vt[e * 64:(e + 1) * 64, :].astype(BF16)
            vt_ref[0, r0 + 64:r0 + VT_ROWS, :] = (
                lax.broadcasted_iota(jnp.int32, (VT_ROWS - 64, T), 0) == 0).astype(BF16)

    lb = _hg_lower_bound(lb_ref, layer)
    hq = _silu(proj[:, 3 * W:3 * W + HG_WIDTH])
    f = lb + (1.0 - lb) * _sigmoid(proj[:, 3 * W + HG_WIDTH:3 * W + 2 * HG_WIDTH])
    hv = proj[:, 3 * W + 2 * HG_WIDTH:3 * W + 3 * HG_WIDTH]
    hgate = proj[:, 3 * W + 3 * HG_WIDTH:3 * W + 4 * HG_WIDTH]
    kk = 1.0 - f
    lf3 = _split3(jnp.log(f))
    cum = _dot3_r(tril_ref[...], lf3)
    nlev = mm_ref.shape[0]
    ri = lax.broadcasted_iota(jnp.int32, (T, T), 0)
    ci = lax.broadcasted_iota(jnp.int32, (T, T), 1)
    eye = ri == ci
    amat = []
    for h in range(HG_HEADS):
        lanes = slice(h * HG_K, (h + 1) * HG_K)
        diag = jnp.sum(hq[:, lanes] * kk[:, lanes], axis=1, keepdims=True)
        amat.append(jnp.where(eye, diag, 0.0))
    for lev in range(nlev):
        m = 1 << lev
        if lev < dm_ref.shape[0]:
            d = _dot3_r(dm_ref[lev], lf3)
        else:
            refs = [jnp.broadcast_to(cum[r0 + m - 1:r0 + m, :], (2 * m, cum.shape[1])) for r0 in range(0, T, 2 * m)]
            d = cum - (refs[0] if len(refs) == 1 else jnp.concatenate(refs, axis=0))
        e_all = jnp.exp(-jnp.abs(d))
        mk = mm_ref[lev].astype(F32)
        for h in range(HG_HEADS):
            lanes = slice(h * HG_K, (h + 1) * HG_K)
            a_m = _dot_nt(hq[:, lanes] * e_all[:, lanes], kk[:, lanes] * e_all[:, lanes])
            amat[h] = amat[h] + a_m * mk
    cum_last = cum[T - 1:T, :]
    ecum = jnp.exp(cum)
    ktail = kk * jnp.exp(cum_last - cum)
    elast = jnp.exp(cum_last)
    for h in range(HG_HEADS):
        lanes = slice(h * HG_K, (h + 1) * HG_K)
        vh = hv[:, lanes]
        vhb = vh.astype(BF16)
        st = st_ref[h]
        o = jnp.dot(amat[h].astype(BF16), vhb, preferred_element_type=F32)
        o = o + _dot_nt(hq[:, lanes] * ecum[:, lanes], st)
        st_new = elast[:, lanes] * st + jnp.dot(vh.T.astype(BF16), ktail[:, lanes].astype(BF16),
                                                preferred_element_type=F32)
        st_ref[h] = st_new
        o_ref[0, :, lanes] = _rms(o, gn_ref[:, lanes]) * _silu(hgate[:, lanes])

    @pl.when(t == nt - 1)
    def _():
        for h in range(HG_HEADS):
            hs_ref[0, h] = st_ref[h].T


def _odd_prompt_proj(x, w, layer):
    b, l, d = x.shape
    nt = l // TILE
    c, s1, s2 = _rope_tables(np.arange(l))
    tril, dm, mm = _hg_levels(TILE)
    consts_a = [w["g"], w["win"]]
    consts_b = [w["lb"], w["gn"], tril, dm, mm]
    tab = pl.BlockSpec((TILE, LANES), lambda i, t: (t, 0))
    W = MOBA_WIDTH
    out_shape = (jax.ShapeDtypeStruct((b, l, W), F32),
                 jax.ShapeDtypeStruct((b, l, W), BF16),
                 jax.ShapeDtypeStruct((b, MOBA_HEADS * VT_ROWS, l), BF16),
                 jax.ShapeDtypeStruct((b, nt, 1, W), F32),
                 jax.ShapeDtypeStruct((b, MOBA_HEADS, MOBA_HEAD_DIM, l), F32),
                 jax.ShapeDtypeStruct((b, MOBA_HEADS, MOBA_HEAD_DIM, l), F32),
                 jax.ShapeDtypeStruct((b, l, HG_WIDTH), F32),
                 jax.ShapeDtypeStruct((b, HG_HEADS, HG_K, HG_V), F32))
    out_specs = (pl.BlockSpec((1, TILE, W), lambda i, t: (i, t, 0)),
                 pl.BlockSpec((1, TILE, W), lambda i, t: (i, t, 0)),
                 pl.BlockSpec((1, MOBA_HEADS * VT_ROWS, TILE), lambda i, t: (i, 0, t)),
                 pl.BlockSpec((1, 1, 1, W), lambda i, t: (i, t, 0, 0)),
                 pl.BlockSpec((1, MOBA_HEADS, MOBA_HEAD_DIM, TILE), lambda i, t: (i, 0, 0, t)),
                 pl.BlockSpec((1, MOBA_HEADS, MOBA_HEAD_DIM, TILE), lambda i, t: (i, 0, 0, t)),
                 pl.BlockSpec((1, TILE, HG_WIDTH), lambda i, t: (i, t, 0)),
                 pl.BlockSpec((1, HG_HEADS, HG_K, HG_V), lambda i, t: (i, 0, 0, 0)))
    return pl.pallas_call(
        functools.partial(_odd_prompt_body, nt=nt, layer=layer),
        grid=(b, nt),
        in_specs=([pl.BlockSpec((1, TILE, d), lambda i, t: (i, t, 0))] + [_const_spec(a.shape) for a in consts_a]
                  + [tab, tab, tab] + [_const_spec(a.shape) for a in consts_b]),
        out_specs=out_specs, out_shape=out_shape,
        scratch_shapes=[pltpu.VMEM((HG_HEADS, HG_V, HG_K), F32)],
        compiler_params=_params(("arbitrary", "arbitrary")), name="odd_prompt_proj")(
            x, *consts_a, c, s1, s2, *consts_b)


def _moba_tile_body(q_ref, k_ref, vt_ref, km_ref, o_ref, sel_ref, s_ref, cm_ref, p_ref):
    j = pl.program_id(2)
    T = TILE
    Q = 2 * T
    hd = MOBA_HEAD_DIM
    q = q_ref[0]
    km = km_ref[0, 0]
    nbp = km.shape[0]
    lane = lax.broadcasted_iota(jnp.int32, (Q, LANES), 1)
    qes = [jnp.where((lane // hd) == e, q, 0.0) for e in range(2)]
    qs = [(qe * (hd ** -0.5 * math.log2(math.e))).astype(BF16) for qe in qes]
    last_blk = k_ref.shape[1] // T - 1

    def scores(b, slot, keep):
        off = pl.multiple_of(jnp.minimum(b, last_blk) * T, T)
        kk = k_ref[0, pl.ds(off, T), :]
        for e in range(2):
            s = jnp.where(keep(e), _dot_nt(kk, qs[e]), NEG)
            s_ref[slot, e] = s
            cm_ref[slot, e] = jnp.max(s, axis=0, keepdims=True)

    def selected(b):
        return lambda e: sel_ref[e, pl.ds(b, 1), :] > 0.5

    def pv(b, slot, e):
        off = pl.multiple_of(jnp.clip(b, 0, last_blk) * T, T)
        return jnp.dot(vt_ref[0, e * VT_ROWS:(e + 1) * VT_ROWS, pl.ds(off, T)], p_ref[slot, e],
                       preferred_element_type=F32)

    def softmax_step(slot, e, m, acc, prev):
        m_new = jnp.maximum(m, cm_ref[slot, e])
        p_ref[slot, e] = jnp.exp2((s_ref[slot, e] - m_new).astype(BF16))
        return m_new, jnp.exp2(m - m_new) * (acc + prev)

    blk = lax.broadcasted_iota(jnp.int32, (nbp, Q), 0)
    col = lax.broadcasted_iota(jnp.int32, (nbp, Q), 1)
    elig = blk < 2 * j + (col >= T).astype(jnp.int32)
    kh = km.astype(BF16)
    kl = (km - kh.astype(F32)).astype(BF16)
    for e in range(2):
        qh = qes[e].astype(BF16)
        ql = (qes[e] - qh.astype(F32)).astype(BF16)
        gate = _dot_nt(kh, qh) + _dot_nt(kh, ql) + _dot_nt(kl, qh)
        gcur = jnp.where(elig, gate, -jnp.inf)
        selm = jnp.zeros((nbp, Q), F32)
        for _ in range(MOBA_TOPK):
            mx = jnp.max(gcur, axis=0, keepdims=True)
            first = jnp.min(jnp.where(gcur == mx, blk, nbp), axis=0, keepdims=True)
            hit = blk == first
            selm = jnp.where(hit & elig, 1.0, selm)
            gcur = jnp.where(hit, -jnp.inf, gcur)
        sel_ref[e] = selm

    ri = lax.broadcasted_iota(jnp.int32, (T, Q), 0)
    ci = lax.broadcasted_iota(jnp.int32, (T, Q), 1)
    carry = []
    for e in range(2):
        carry += [jnp.full((1, Q), NEG, F32), jnp.zeros((VT_ROWS, Q), F32)]

    own_a = (ci < T) & (ri <= ci)
    scores(2 * j, 0, lambda e: own_a | (sel_ref[e, pl.ds(2 * j, 1), :] > 0.5))
    scores(2 * j + 1, 1, lambda e: (ci >= T) & (ri <= ci - T))
    for e in range(2):
        carry[2 * e], carry[2 * e + 1] = softmax_step(0, e, carry[2 * e], carry[2 * e + 1], 0.0)
    scores(0, 0, selected(0))
    for e in range(2):
        carry[2 * e], carry[2 * e + 1] = softmax_step(1, e, carry[2 * e], carry[2 * e + 1], pv(2 * j, 0, e))

    def body(u, carry):
        carry = list(carry)
        scores(2 * u + 1, 1, selected(2 * u + 1))
        for e in range(2):
            prev = pv(jnp.where(u == 0, 2 * j + 1, 2 * u - 1), 1, e)
            carry[2 * e], carry[2 * e + 1] = softmax_step(0, e, carry[2 * e], carry[2 * e + 1], prev)
        scores(2 * u + 2, 0, selected(2 * u + 2))
        for e in range(2):
            carry[2 * e], carry[2 * e + 1] = softmax_step(1, e, carry[2 * e], carry[2 * e + 1], pv(2 * u, 0, e))
        return tuple(carry)

    fin = lax.fori_loop(0, j, body, tuple(carry))
    outs = []
    for e in range(2):
        tot = fin[2 * e + 1] + pv(jnp.where(j == 0, 1, 2 * j - 1), 1, e)
        outs.append(tot[0:hd] / tot[hd:hd + 1])
    o_ref[0] = jnp.concatenate(outs, axis=0).T


def _moba_tiles(q, kb, vt, kmean):
    b, l, w = q.shape
    tq = 2 * TILE
    assert l % tq == 0
    npair = w // LANES
    nb = kmean.shape[1]
    km = kmean.reshape(b, nb, npair, LANES).transpose(0, 2, 1, 3)
    nbp = -(-nb // 8) * 8
    km = jnp.pad(km, ((0, 0), (0, 0), (0, nbp - nb), (0, 0)))
    return pl.pallas_call(
        _moba_tile_body,
        grid=(b, npair, l // tq),
        in_specs=[pl.BlockSpec((1, tq, LANES), lambda i, p, t: (i, t, p)),
                  pl.BlockSpec((1, l, LANES), lambda i, p, t: (i, 0, p)),
                  pl.BlockSpec((1, 2 * VT_ROWS, l), lambda i, p, t: (i, p, 0)),
                  pl.BlockSpec((1, 1, nbp, LANES), lambda i, p, t: (i, p, 0, 0))],
        out_specs=pl.BlockSpec((1, tq, LANES), lambda i, p, t: (i, t, p)),
        out_shape=jax.ShapeDtypeStruct((b, l, w), F32),
        scratch_shapes=[pltpu.VMEM((2, nbp, tq), F32), pltpu.VMEM((2, 2, TILE, tq), F32),
                        pltpu.VMEM((2, 2, 1, tq), F32), pltpu.VMEM((2, 2, TILE, tq), BF16)],
        compiler_params=_params(("arbitrary", "arbitrary", "arbitrary")), name="moba_prompt")(q, kb, vt, km)


def _even_sample_proj_body(x_ref, g_ref, wz_ref, wxbc_ref, wdt_ref, wga_ref, wgb_ref, cw_ref, cb_ref, dtb_ref,
                           xbuf_ref, cfw_ref, cfb_ref, cfg_ref, cfbeta_ref, ubuf_ref,
                           z_ref, xc_ref, dt_ref, xnew_ref, c_ref, unew_ref):
    x = x_ref[...]
    hb = _rms(x, g_ref[...]).astype(BF16)
    z_ref[...] = jnp.dot(hb, wz_ref[...], preferred_element_type=F32)
    xbc = jnp.dot(hb, wxbc_ref[...], preferred_element_type=F32)
    dtr = jnp.dot(hb, wdt_ref[...], preferred_element_type=F32)
    ga = jnp.dot(hb, wga_ref[...], preferred_element_type=F32)
    gb = jnp.dot(hb, wgb_ref[...], preferred_element_type=F32)
    k1 = SSD_CONV - 1
    acc = cw_ref[k1:k1 + 1, :] * xbc
    for k in range(k1):
        acc = acc + cw_ref[k:k + 1, :] * xbuf_ref[k]
        if k > 0:
            xnew_ref[k - 1] = xbuf_ref[k]
    xnew_ref[k1 - 1] = xbc
    xc_ref[...] = _silu(acc + cb_ref[...])
    dt_ref[...] = _softplus(dtr + dtb_ref[...])
    u = ga * _sigmoid(gb)
    k2 = CF_WIDTH - 1
    cacc = cfw_ref[k2:k2 + 1, :] * u
    for k in range(k2):
        cacc = cacc + cfw_ref[k:k + 1, :] * ubuf_ref[k]
        if k > 0:
            unew_ref[k - 1] = ubuf_ref[k]
    unew_ref[k2 - 1] = u
    c32 = cacc + cfb_ref[...]
    mu = jnp.mean(c32, axis=-1, keepdims=True)
    var = jnp.mean(jnp.square(c32 - mu), axis=-1, keepdims=True)
    c_ref[...] = _silu((c32 - mu) * lax.rsqrt(var + NORM_EPS) * cfg_ref[...] + cfbeta_ref[...])


STEP_ROWS = 128
HG_STEP_ROWS = 32


def _ssd_step_body(s_ref, xb_ref, dt_ref, alog_ref, b_ref, c_ref, so_ref, yt_ref):
    rows = s_ref.shape[0]
    dt = dt_ref[...]
    decay = jnp.exp(dt * (-jnp.exp(alog_ref[...])))
    bdt = b_ref[...] * dt
    cc = c_ref[...]
    for r in range(rows):
        s_new = decay[r:r + 1, :] * s_ref[r] + xb_ref[r] * bdt[r:r + 1, :]
        so_ref[r] = s_new
        yt_ref[:, r:r + 1] = jnp.sum(s_new * cc[r:r + 1, :], axis=1, keepdims=True)


def _even_sample_out_body(x_ref, y_ref, xs_ref, z_ref, dfull_ref, sn_ref, c_ref, woy_ref, woc_ref, o_ref):
    y = (y_ref[...] + dfull_ref[...] * xs_ref[...]) * _silu(z_ref[...])
    gw = SSD_INNER // SSD_GROUPS
    out = _dot(c_ref[...], woc_ref[...])
    for grp in range(SSD_GROUPS):
        lanes = slice(grp * gw, (grp + 1) * gw)
        out = out + _dot(_rms(y[:, lanes], sn_ref[:, lanes]), woy_ref[lanes, :])
    o_ref[...] = x_ref[...] + out


def _call(body, out_shape, *args, name):
    return pl.pallas_call(body, out_shape=out_shape, compiler_params=_params(None), name=name)(*args)


def _even_sample(x, w, s_ssd, buf_ssd, buf_cf):
    nb = x.shape[0]
    f = lambda *s: jax.ShapeDtypeStruct(s, F32)
    z, xc, dt, xnew, c, unew = _call(
        _even_sample_proj_body,
        (f(nb, SSD_INNER), f(nb, SSD_CONV_DIM), f(nb, LANES), f(SSD_CONV - 1, nb, SSD_CONV_DIM), f(nb, CF_CH),
         f(CF_WIDTH - 1, nb, CF_CH)),
        x, w["g"], w["wz"], w["wxbc"], w["wdt"], w["wga"], w["wgb"], w["cw"], w["cb"], w["dtb"],
        jnp.swapaxes(buf_ssd, 0, 1), w["cfw"], w["cfb"], w["cfg"], w["cfbeta"], jnp.swapaxes(buf_cf, 0, 1),
        name="even_sample_proj")
    xs = xc[:, :SSD_INNER]
    rows = nb * SSD_HEADS
    rep_heads = SSD_HEADS // SSD_GROUPS
    grp = lambda a: jnp.repeat(a.reshape(nb, SSD_GROUPS, SSD_STATE), rep_heads, axis=1).reshape(rows, SSD_STATE)
    bm = grp(xc[:, SSD_INNER:SSD_INNER + SSD_GROUPS * SSD_STATE])
    cm = grp(xc[:, SSD_INNER + SSD_GROUPS * SSD_STATE:])
    rb = STEP_ROWS
    assert rows % rb == 0
    rowspec = pl.BlockSpec((rb, SSD_STATE), lambda i: (i, 0))
    colspec = pl.BlockSpec((SSD_HEAD_DIM, rb), lambda i: (0, i))
    sspec = pl.BlockSpec((rb, SSD_HEAD_DIM, SSD_STATE), lambda i: (i, 0, 0))
    per_row = lambda a: jnp.broadcast_to(a.reshape(rows, 1), (rows, SSD_STATE))
    s_new, yt = pl.pallas_call(
        _ssd_step_body, grid=(rows // rb,),
        in_specs=[sspec, sspec, rowspec, rowspec, rowspec, rowspec],
        out_specs=(sspec, colspec),
        out_shape=(f(rows, SSD_HEAD_DIM, SSD_STATE), f(SSD_HEAD_DIM, rows)),
        compiler_params=_params(("parallel",)), name="ssd_step")(
            s_ssd.reshape(rows, SSD_HEAD_DIM, SSD_STATE),
            jnp.broadcast_to(xs.reshape(rows, SSD_HEAD_DIM, 1), (rows, SSD_HEAD_DIM, SSD_STATE)),
            per_row(dt[:, :SSD_HEADS]), per_row(jnp.tile(w["alog"][0, :SSD_HEADS], nb)), bm, cm)
    xo = _call(_even_sample_out_body, f(nb, x.shape[1]),
               x, yt.T.reshape(nb, SSD_INNER), xs, z, w["dfull"], w["sn"], c, w["woy"], w["woc"],
               name="even_sample_out")
    return (xo, s_new.reshape(s_ssd.shape), jnp.swapaxes(xnew, 0, 1), jnp.swapaxes(unew, 0, 1))


def _odd_sample_proj_body(x_ref, g_ref, win_ref, c_ref, s1_ref, s2_ref, lb_ref,
                          q_ref, k_ref, v_ref, hq_ref, f_ref, hv_ref, hg_ref, *, layer):
    W = MOBA_WIDTH
    hb = _rms(x_ref[...], g_ref[...]).astype(BF16)
    proj = jnp.dot(hb, win_ref[...], preferred_element_type=F32)
    c, s1, s2 = c_ref[...], s1_ref[...], s2_ref[...]
    qb = _rope(proj[:, 0:W], c, s1, s2)
    kb = _rope(proj[:, W:2 * W], c, s1, s2)
    for j in range(W // LANES):
        q_ref[:, j * LANES:(j + 1) * LANES] = qb[j]
        k_ref[:, j * LANES:(j + 1) * LANES] = kb[j]
    v_ref[...] = proj[:, 2 * W:3 * W]
    lb = _hg_lower_bound(lb_ref, layer)
    hq_ref[...] = _silu(proj[:, 3 * W:3 * W + HG_WIDTH])
    f_ref[...] = lb + (1.0 - lb) * _sigmoid(proj[:, 3 * W + HG_WIDTH:3 * W + 2 * HG_WIDTH])
    hv_ref[...] = proj[:, 3 * W + 2 * HG_WIDTH:3 * W + 3 * HG_WIDTH]
    hg_ref[...] = proj[:, 3 * W + 3 * HG_WIDTH:3 * W + 4 * HG_WIDTH]


def _hg_step_body(s_ref, qt_ref, ft_ref, v_ref, so_ref, o_ref):
    qt, ft, vv = qt_ref[0], ft_ref[0], v_ref[...]
    for r in range(s_ref.shape[0]):
        fc = ft[:, r:r + 1]
        s_new = fc * s_ref[r] + (1.0 - fc) * vv[r:r + 1, :]
        so_ref[r] = s_new
        o_ref[r:r + 1, :] = jnp.sum(s_new * qt[:, r:r + 1], axis=0, keepdims=True)


def _attn_copies(idx_ref, pt_ref, ck_ref, cv_ref, kbuf, vbuf, sem, layer, seq, slot):
    ppb = MOBA_BLOCK // PAGE_SIZE
    cps = []
    for h in range(MOBA_HEADS):
        for r in range(MOBA_TOPK):
            blk = idx_ref[seq, h * MOBA_TOPK + r]
            for pg in range(ppb):
                phys = pt_ref[seq, blk * ppb + pg]
                j = r * ppb + pg
                cps.append(pltpu.make_async_copy(ck_ref.at[layer, phys, h], kbuf.at[slot, h, j], sem.at[0, slot]))
                cps.append(pltpu.make_async_copy(cv_ref.at[layer, phys, h], vbuf.at[slot, h, j], sem.at[1, slot]))
    return cps


def _sample_attn_body(idx_ref, pt_ref, ck_ref, cv_ref, q_ref, kn_ref, vn_ref, o_ref, kbuf, vbuf, sem, *,
                      layer, n_seq):
    b = pl.program_id(0)
    slot = b % 2
    scale = MOBA_HEAD_DIM ** -0.5
    nsel = MOBA_TOPK * (MOBA_BLOCK // PAGE_SIZE)

    @pl.when(b == 0)
    def _():
        for cp in _attn_copies(idx_ref, pt_ref, ck_ref, cv_ref, kbuf, vbuf, sem, layer, b, slot):
            cp.start()

    @pl.when(b + 1 < n_seq)
    def _():
        for cp in _attn_copies(idx_ref, pt_ref, ck_ref, cv_ref, kbuf, vbuf, sem, layer, b + 1, 1 - slot):
            cp.start()

    for cp in _attn_copies(idx_ref, pt_ref, ck_ref, cv_ref, kbuf, vbuf, sem, layer, b, slot):
        cp.wait()

    for h in range(MOBA_HEADS):
        qc = q_ref[0, h]
        s_self = jnp.sum(qc * kn_ref[0, h], axis=0, keepdims=True) * scale
        ss = [jnp.sum(kbuf[slot, h, j] * qc, axis=0, keepdims=True) * scale for j in range(nsel)]
        m = s_self
        for s in ss:
            m = jnp.maximum(m, jnp.max(s, axis=1, keepdims=True))
        p_self = jnp.exp(s_self - m)
        l = p_self
        acc = jnp.zeros((MOBA_HEAD_DIM, LANES), F32)
        for j in range(nsel):
            p = jnp.exp(ss[j] - m)
            l = l + jnp.sum(p, axis=1, keepdims=True)
            acc = acc + vbuf[slot, h, j] * p
        o_ref[0, h] = (vn_ref[0, h] * p_self + jnp.sum(acc, axis=1, keepdims=True)) / l


def _odd_sample_out_body(x_ref, att_ref, o_ref, hg_ref, gn_ref, woa_ref, woh_ref, xo_ref):
    out = _dot(att_ref[...], woa_ref[...])
    o = o_ref[...]
    gated = []
    for h in range(HG_HEADS):
        lanes = slice(h * HG_V, (h + 1) * HG_V)
        gated.append(_rms(o[:, lanes], gn_ref[:, lanes]) * _silu(hg_ref[:, lanes]))
    out = out + _dot(jnp.concatenate(gated, axis=1), woh_ref[...])
    xo_ref[...] = x_ref[...] + out


def _odd_sample_pre(x, w, layer, s_hg, page_table, past_len):
    nb = x.shape[0]
    f = lambda *s: jax.ShapeDtypeStruct(s, F32)
    n_pages = page_table.shape[1]
    ppb = MOBA_BLOCK // PAGE_SIZE
    nblk = n_pages // ppb
    assert n_pages % ppb == 0 and nblk >= MOBA_TOPK and nblk < LANES
    assert n_pages % GATE_CHUNK == 0 and GATE_CHUNK % ppb == 0
    c, s1, s2 = _rope_tables(np.full((1,), past_len))
    q, k, v, hq, fg, hv, hg = _call(
        functools.partial(_odd_sample_proj_body, layer=layer),
        tuple(f(nb, MOBA_WIDTH) for _ in range(7)),
        x, w["g"], w["win"], c, s1, s2, w["lb"], name="odd_sample_proj")

    rows = nb * HG_HEADS
    rb = HG_STEP_ROWS
    assert rows % rb == 0
    cols = lambda a: _pad_lanes(jnp.swapaxes(a.reshape(rows // rb, rb, HG_K), 1, 2))
    sspec = pl.BlockSpec((rb, HG_K, HG_V), lambda i: (i, 0, 0))
    cspec = pl.BlockSpec((1, HG_K, LANES), lambda i: (i, 0, 0))
    rspec = pl.BlockSpec((rb, HG_V), lambda i: (i, 0))
    s_new, o = pl.pallas_call(
        _hg_step_body, grid=(rows // rb,),
        in_specs=[sspec, cspec, cspec, rspec], out_specs=(sspec, rspec),
        out_shape=(f(rows, HG_K, HG_V), f(rows, HG_V)),
        compiler_params=_params(("parallel",)), name="hg_step")(
            s_hg.reshape(rows, HG_K, HG_V), cols(hq), cols(fg), hv.reshape(rows, HG_V))

    hd = MOBA_HEAD_DIM
    col = lambda a: jnp.broadcast_to(a.reshape(nb, MOBA_HEADS, hd, 1), (nb, MOBA_HEADS, hd, LANES))
    return dict(qcol=col(q), kcol=col(k), vcol=col(v), kn=k.reshape(nb, MOBA_HEADS, 1, hd),
                vn=v.reshape(nb, MOBA_HEADS, 1, hd), o=o.reshape(nb, HG_WIDTH), hg=hg,
                state=s_new.reshape(s_hg.shape))


def _odd_sample_post(x, w, pre, idx, cache_kt, cache_vt, cache_layer, page_table):
    nb = x.shape[0]
    f = lambda *s: jax.ShapeDtypeStruct(s, F32)
    hd = MOBA_HEAD_DIM
    nsel = MOBA_TOPK * (MOBA_BLOCK // PAGE_SIZE)
    colspec = pl.BlockSpec((1, MOBA_HEADS, hd, LANES), lambda i, *_: (i, 0, 0, 0))
    att = pl.pallas_call(
        functools.partial(_sample_attn_body, layer=cache_layer, n_seq=nb),
        grid_spec=pltpu.PrefetchScalarGridSpec(
            num_scalar_prefetch=2, grid=(nb,),
            in_specs=[pl.BlockSpec(memory_space=pl.ANY), pl.BlockSpec(memory_space=pl.ANY),
                      colspec, colspec, colspec],
            out_specs=colspec,
            scratch_shapes=[pltpu.VMEM((2, MOBA_HEADS, nsel, hd, PAGE_SIZE), F32),
                            pltpu.VMEM((2, MOBA_HEADS, nsel, hd, PAGE_SIZE), F32),
                            pltpu.SemaphoreType.DMA((2, 2))]),
        out_shape=f(nb, MOBA_HEADS, hd, LANES),
        compiler_params=_params(("arbitrary",)), name="sample_attn")(
            jnp.swapaxes(idx[:, :MOBA_TOPK, :MOBA_HEADS], 1, 2).reshape(nb, MOBA_HEADS * MOBA_TOPK),
            page_table, cache_kt, cache_vt,
            pre["qcol"], pre["kcol"], pre["vcol"])
    att = att[:, :, :, 0].reshape(nb, MOBA_WIDTH)
    return _call(_odd_sample_out_body, f(nb, x.shape[1]),
                 x, att, pre["o"], pre["hg"], w["gn"], w["woa"], w["woh"], name="odd_sample_out")


def _row(a):
    return a.reshape(1, -1).astype(F32)


def _pad_lanes(a, n=LANES):
    return jnp.pad(a, [(0, 0)] * (a.ndim - 1) + [(0, n - a.shape[-1])])


def kernel(x_prompt, x_sample, state_ssd, state_ssd_conv, state_cf_conv, cache_k, cache_v, page_table, state_hg,
           ffn1_norm, ffn1_w_gu, ffn1_w_down, mix_norm, ffn2_norm, ffn2_w_gu, ffn2_w_down, final_norm,
           even_w_in, ssd_conv_w, ssd_conv_b, ssd_dt_bias, ssd_a_log, ssd_d, ssd_norm,
           cf_dw_w, cf_dw_b, cf_ln_g, cf_ln_b, even_w_out,
           odd_w_in, hg_lower_bound, hg_norm, odd_w_out):
    depth = ffn1_norm.shape[0]
    bp, lp, d = x_prompt.shape
    nb = x_sample.shape[0]
    assert x_sample.shape[1] == 1 and lp % (2 * TILE) == 0
    past_len = page_table.shape[1] * PAGE_SIZE

    ffn_bf16 = [a.astype(BF16) for a in (ffn1_w_gu, ffn1_w_down, ffn2_w_gu, ffn2_w_down)]

    def ffn_w(norm, w_gu, w_down, l):
        return (_row(norm[l]), w_gu, w_down, l)

    def even_w(e, l):
        wi = even_w_in[e]
        o1 = SSD_INNER
        o2 = o1 + SSD_CONV_DIM
        o3 = o2 + SSD_HEADS
        sel = (np.arange(LANES)[:, None] == np.arange(SSD_INNER)[None, :] // SSD_HEAD_DIM).astype(np.float32)
        return dict(
            g=_row(mix_norm[l]), wz=wi[:, :o1].astype(BF16), wxbc=wi[:, o1:o2].astype(BF16),
            wdt=_pad_lanes(wi[:, o2:o3]).astype(BF16), wga=wi[:, o3:o3 + CF_CH].astype(BF16),
            wgb=wi[:, o3 + CF_CH:].astype(BF16), cw=ssd_conv_w[e], cb=_row(ssd_conv_b[e]),
            dtb=_pad_lanes(_row(ssd_dt_bias[e])), alog=_pad_lanes(_row(ssd_a_log[e])),
            dfull=_row(jnp.repeat(ssd_d[e], SSD_HEAD_DIM)), sn=_row(ssd_norm[e]), sel=jnp.asarray(sel, BF16),
            cfw=cf_dw_w[e], cfb=_row(cf_dw_b[e]), cfg=_row(cf_ln_g[e]), cfbeta=_row(cf_ln_b[e]),
            woy=even_w_out[e, :SSD_INNER].astype(BF16), woc=even_w_out[e, SSD_INNER:].astype(BF16))

    def odd_w(o, l):
        return dict(g=_row(mix_norm[l]), win=odd_w_in[o].astype(BF16), lb=hg_lower_bound.astype(F32),
                    gn=_row(hg_norm[o]), woa=odd_w_out[o, :MOBA_WIDTH].astype(BF16),
                    woh=odd_w_out[o, MOBA_WIDTH:].astype(BF16))

    cache_kt = jnp.swapaxes(cache_k, -1, -2)
    cache_vt = jnp.swapaxes(cache_v, -1, -2)
    layer_w = []
    for l in range(depth):
        mix = even_w(l // 2, l) if l % 2 == 0 else odd_w(l // 2, l)
        layer_w.append((ffn_w(ffn1_norm, ffn_bf16[0], ffn_bf16[1], l), mix,
                        ffn_w(ffn2_norm, ffn_bf16[2], ffn_bf16[3], l), _row(final_norm) if l == depth - 1 else None))

    steps = bp * lp // TILE
    n_chunks = page_table.shape[1] // GATE_CHUNK
    host_layer = depth - 1 if depth % 2 == 0 else depth - 2
    if host_layer < 1 or steps % n_chunks or 2 * depth * (steps // n_chunks) != nb:
        host_layer = None

    def sample_group(hosted_gate):
        xs = x_sample.reshape(nb, d)
        ssd_s, sc_s, cf_s, k_s, v_s, hg_s = [], [], [], [], [], []
        for l, (w1, w, w2, fin) in enumerate(layer_w):
            xs = _ffn(xs, w1, tm=nb)
            if l % 2 == 0:
                e = l // 2
                xs, st, xt, ut = _even_sample(xs, w, state_ssd[e], state_ssd_conv[e], state_cf_conv[e])
                ssd_s.append(st)
                sc_s.append(xt)
                cf_s.append(ut)
            else:
                o = l // 2
                pre = _odd_sample_pre(xs, w, l, state_hg[o], page_table, past_len)
                if l == host_layer:
                    idx = hosted_gate(pre["qcol"], o)
                else:
                    idx = _sample_gate(page_table, cache_kt, pre["qcol"], o)
                xs = _odd_sample_post(xs, w, pre, idx, cache_kt, cache_vt, o, page_table)
                k_s.append(pre["kn"])
                v_s.append(pre["vn"])
                hg_s.append(pre["state"])
            xs = _ffn(xs, w2, tm=nb, final_g=fin)
        return xs.reshape(nb, 1, d), ssd_s, sc_s, cf_s, k_s, v_s, hg_s

    def prompt_group(hosted):
        xp = x_prompt.reshape(bp * lp, d)
        ssd_p, sc_p, cf_p, k_p, v_p, hg_p, ids = [], [], [], [], [], [], []

        def ffn(x, *args, **kw):
            if hosted is None:
                return _ffn(x, *args, **kw)
            qcol, cache_layer = hosted
            x, part = _ffn(x, *args, gate=(page_table, cache_kt, qcol, cache_layer, len(ids) * (steps // n_chunks)),
                           **kw)
            ids.append(part)
            return x

        for l, (w1, w, w2, fin) in enumerate(layer_w):
            xp = ffn(xp, w1, tm=TILE)
            if l % 2 == 0:
                xp3, st, xt, ut = _even_prompt(xp.reshape(bp, lp, d), w)
                xp = xp3.reshape(bp * lp, d)
                ssd_p.append(st)
                sc_p.append(xt)
                cf_p.append(ut)
                xp = ffn(xp, w2, tm=TILE, final_g=fin)
            else:
                q, kb, vt, kmean, ko, vo, ohg, hst = _odd_prompt_proj(xp.reshape(bp, lp, d), w, l)
                att = _moba_tiles(q, kb, vt, kmean[:, :, 0, :])
                k_p.append(jnp.swapaxes(ko, -1, -2))
                v_p.append(jnp.swapaxes(vo, -1, -2))
                hg_p.append(hst)
                xp = ffn(xp, w2, tm=TILE, final_g=fin, pre=(att.reshape(bp * lp, MOBA_WIDTH),
                                                            ohg.reshape(bp * lp, HG_WIDTH), w["woa"], w["woh"]))
        return (xp.reshape(bp, lp, d), ssd_p, sc_p, cf_p, k_p, v_p, hg_p), ids

    prompt_result = []

    def hosted_gate(qcol, cache_layer):
        out, ids = prompt_group((qcol, cache_layer))
        prompt_result.append(out)
        return jnp.concatenate(ids, axis=0)

    sample_out = sample_group(hosted_gate)
    prompt_out = prompt_result[0] if prompt_result else prompt_group(None)[0]
    y_p, ssd_p, sc_p, cf_p, k_p, v_p, hg_p = prompt_out
    y_s, ssd_s, sc_s, cf_s, k_s, v_s, hg_s = sample_out
    st = jnp.stack
    return (y_p, y_s, st(ssd_p), st(ssd_s), st(sc_p), st(sc_s), st(cf_p), st(cf_s),
            st(k_p), st(k_s), st(v_p), st(v_s), st(hg_p), st(hg_s))
```

```python
import functools
import math

import numpy as np
import jax
import jax.numpy as jnp
from jax import lax
from jax.experimental import pallas as pl
from jax.experimental.pallas import tpu as pltpu

F32 = jnp.float32
BF16 = jnp.bfloat16

SSD_HEADS = 16
SSD_HEAD_DIM = 64
SSD_INNER = SSD_HEADS * SSD_HEAD_DIM
SSD_GROUPS = 2
SSD_STATE = 64
SSD_CONV = 4
SSD_CONV_DIM = SSD_INNER + 2 * SSD_GROUPS * SSD_STATE
CF_CH = 512
CF_WIDTH = 31
MOBA_HEADS = 8
MOBA_HEAD_DIM = 64
MOBA_WIDTH = MOBA_HEADS * MOBA_HEAD_DIM
MOBA_BLOCK = 256
MOBA_TOPK = 3
ROPE_DIM = MOBA_HEAD_DIM // 4
ROPE_THETA = 500000.0
HG_HEADS = 4
HG_K = 128
HG_V = 128
HG_WIDTH = HG_HEADS * HG_K
PAGE_SIZE = 128
NORM_EPS = 1e-6
NEG = -1e30

LANES = 128
SSD_CHUNK = 128
TILE = 256
VT_ROWS = MOBA_HEAD_DIM + 16
VMEM_LIMIT = 56 * 1024 * 1024


def _sigmoid(x):
    return 1.0 / (1.0 + jnp.exp(-x))


def _silu(x):
    return x * _sigmoid(x)


def _softplus(x):
    return jnp.maximum(x, 0.0) + jnp.log1p(jnp.exp(-jnp.abs(x)))


def _rms(x, g):
    return x * lax.rsqrt(jnp.mean(x * x, axis=-1, keepdims=True) + NORM_EPS) * g


def _dot(a, b):
    return jnp.dot(a.astype(BF16), b.astype(BF16), preferred_element_type=F32)


def _dot_nt(a, b):
    return lax.dot_general(a.astype(BF16), b.astype(BF16), (((1,), (1,)), ((), ())),
                           preferred_element_type=F32)


def _split3(a):
    hi = a.astype(BF16)
    r = a - hi.astype(F32)
    mid = r.astype(BF16)
    lo = (r - mid.astype(F32)).astype(BF16)
    return hi, mid, lo


def _dot3_l(a, m):
    hi, mid, lo = _split3(a)
    return (jnp.dot(hi, m, preferred_element_type=F32) + jnp.dot(mid, m, preferred_element_type=F32)
            + jnp.dot(lo, m, preferred_element_type=F32))


def _dot3_r(m, parts):
    hi, mid, lo = parts
    return (jnp.dot(m, hi, preferred_element_type=F32) + jnp.dot(m, mid, preferred_element_type=F32)
            + jnp.dot(m, lo, preferred_element_type=F32))


def _const_spec(shape):
    n = len(shape)
    return pl.BlockSpec(shape, lambda *_: (0,) * n, pipeline_mode=pl.Buffered(1))


def _params(sem):
    return pltpu.CompilerParams(dimension_semantics=sem, vmem_limit_bytes=VMEM_LIMIT)


GATE_CHUNK = 16


def _gate_copies(pt_ref, ck_ref, kbuf, sem, layer, seq, chunk, slot):
    return [pltpu.make_async_copy(ck_ref.at[layer, pt_ref[seq, chunk * GATE_CHUNK + j]], kbuf.at[slot, j],
                                  sem.at[slot]) for j in range(GATE_CHUNK)]


def _gate_fetch(pt_ref, ck_ref, kbuf, sem, *, layer, seq0, n_chunks, n_steps):
    step = pl.program_id(0)
    slot = step % 2
    seq = seq0 + step // n_chunks
    c = step % n_chunks

    @pl.when(step == 0)
    def _():
        for cp in _gate_copies(pt_ref, ck_ref, kbuf, sem, layer, seq, c, slot):
            cp.start()

    nxt = step + 1

    @pl.when(nxt < n_steps)
    def _():
        for cp in _gate_copies(pt_ref, ck_ref, kbuf, sem, layer, seq0 + nxt // n_chunks, nxt % n_chunks, 1 - slot):
            cp.start()

    for cp in _gate_copies(pt_ref, ck_ref, kbuf, sem, layer, seq, c, slot):
        cp.wait()
    return slot, c


GATE_CHUNK_BLOCKS = GATE_CHUNK // (MOBA_BLOCK // PAGE_SIZE)


def _gate_accumulate(kbuf, q_ref, g_ref, slot, c):
    ppb = MOBA_BLOCK // PAGE_SIZE
    nbc = GATE_CHUNK_BLOCKS
    lane = lax.broadcasted_iota(jnp.int32, (nbc, LANES), 1)
    sub = lax.broadcasted_iota(jnp.int32, (nbc, LANES), 0)
    rows = jnp.zeros((nbc, LANES), F32)
    for h in range(MOBA_HEADS):
        q = q_ref[0, h]
        tile = jnp.zeros((nbc, LANES), F32)
        for jb in range(nbc):
            ksum = kbuf[slot, jb * ppb, h]
            for r in range(1, ppb):
                ksum = ksum + kbuf[slot, jb * ppb + r, h]
            tile = jnp.where(sub == jb, jnp.sum(ksum * q, axis=0, keepdims=True), tile)
        rows = jnp.where(lane == h, jnp.sum(tile, axis=1, keepdims=True), rows)
    g_ref[pl.ds(pl.multiple_of(c * nbc, nbc), nbc), :] = rows


def _gate_finish(g_ref, idx_ref, c, n_chunks):
    @pl.when(c == n_chunks - 1)
    def _():
        nblk = g_ref.shape[0]
        row = lax.broadcasted_iota(jnp.int32, (nblk, LANES), 0)
        rank = lax.broadcasted_iota(jnp.int32, idx_ref.shape[1:], 0)
        gcur = g_ref[...] * (1.0 / MOBA_BLOCK)
        out = jnp.zeros(idx_ref.shape[1:], jnp.int32)
        for r in range(MOBA_TOPK):
            mx = jnp.max(gcur, axis=0, keepdims=True)
            first = jnp.min(jnp.where(gcur == mx, row, nblk), axis=0, keepdims=True)
            out = jnp.where(rank == r, first, out)
            gcur = jnp.where(row == first, -jnp.inf, gcur)
        idx_ref[0] = out


def _gate_scratch(n_chunks):
    return [pltpu.VMEM((2, GATE_CHUNK, MOBA_HEADS, MOBA_HEAD_DIM, PAGE_SIZE), F32), pltpu.SemaphoreType.DMA((2,)),
            pltpu.VMEM((n_chunks * GATE_CHUNK_BLOCKS, LANES), F32)]


def _sample_gate_body(pt_ref, ck_ref, q_ref, idx_ref, kbuf, sem, g_ref, **job):
    slot, c = _gate_fetch(pt_ref, ck_ref, kbuf, sem, **job)
    _gate_accumulate(kbuf, q_ref, g_ref, slot, c)
    _gate_finish(g_ref, idx_ref, c, job["n_chunks"])


def _sample_gate(page_table, cache_kt, qcol, layer):
    nb, n_pages = page_table.shape
    n_chunks = n_pages // GATE_CHUNK
    job = dict(layer=layer, seq0=0, n_chunks=n_chunks, n_steps=nb * n_chunks)
    return pl.pallas_call(
        functools.partial(_sample_gate_body, **job),
        grid_spec=pltpu.PrefetchScalarGridSpec(
            num_scalar_prefetch=1, grid=(nb * n_chunks,),
            in_specs=[pl.BlockSpec(memory_space=pl.ANY),
                      pl.BlockSpec((1,) + qcol.shape[1:], lambda s, pt: (s // n_chunks, 0, 0, 0))],
            out_specs=pl.BlockSpec((1, MOBA_HEADS, LANES), lambda s, pt: (s // n_chunks, 0, 0)),
            scratch_shapes=_gate_scratch(n_chunks)),
        out_shape=jax.ShapeDtypeStruct((nb, MOBA_HEADS, LANES), jnp.int32),
        compiler_params=_params(("arbitrary",)), name="sample_gate")(page_table, cache_kt, qcol)


def _ffn_body(*refs, pre, final, job):
    it = iter(refs)
    if job is not None:
        pt_ref = next(it)
    x_ref = next(it)
    if pre:
        a_ref, b_ref, wa_ref, wb_ref = next(it), next(it), next(it), next(it)
    g_ref, wgu_ref, wd_ref = next(it), next(it), next(it)
    if final:
        fg_ref = next(it)
    if job is not None:
        ck_ref, q_ref = next(it), next(it)
    o_ref = next(it)
    if job is not None:
        idx_ref, kbuf, sem, gacc_ref = next(it), next(it), next(it), next(it)
        slot, c = _gate_fetch(pt_ref, ck_ref, kbuf, sem, **job)
        _gate_accumulate(kbuf, q_ref, gacc_ref, slot, c)
    d_ff = wd_ref.shape[1]
    x = x_ref[...]
    if pre:
        x = x + _dot(a_ref[...], wa_ref[...]) + _dot(b_ref[...], wb_ref[...])
    hb = _rms(x, g_ref[...]).astype(BF16)
    g = jnp.dot(hb, wgu_ref[0, :, :d_ff], preferred_element_type=F32)
    u = jnp.dot(hb, wgu_ref[0, :, d_ff:], preferred_element_type=F32)
    act = (_silu(g) * u).astype(BF16)
    y = x + 0.5 * jnp.dot(act, wd_ref[0], preferred_element_type=F32)
    if final:
        y = _rms(y, fg_ref[...])
    o_ref[...] = y
    if job is not None:
        _gate_finish(gacc_ref, idx_ref, c, job["n_chunks"])


def _ffn(x, w, *, tm, pre=None, final_g=None, gate=None):
    m, d = x.shape
    g, wgu, wd, layer = w
    steps = m // tm
    row = lambda c: pl.BlockSpec((tm, c), lambda i, *_: (i, 0))
    layer_spec = lambda a: pl.BlockSpec((1,) + a.shape[1:], lambda i, *_: (layer, 0, 0),
                                        pipeline_mode=pl.Buffered(1))
    args, specs = [x], [row(d)]
    if pre is not None:
        a, b, wa, wb = pre
        args += [a, b, wa, wb]
        specs += [row(a.shape[1]), row(b.shape[1]), _const_spec(wa.shape), _const_spec(wb.shape)]
    args += [g, wgu, wd]
    specs += [_const_spec(g.shape), layer_spec(wgu), layer_spec(wd)]
    if final_g is not None:
        args.append(final_g)
        specs.append(_const_spec(final_g.shape))
    out_shape = jax.ShapeDtypeStruct((m, d), F32)
    if gate is None:
        return pl.pallas_call(
            functools.partial(_ffn_body, pre=pre is not None, final=final_g is not None, job=None),
            grid=(steps,), in_specs=specs, out_specs=row(d), out_shape=out_shape,
            compiler_params=_params(("parallel",)), name="ffn")(*args)
    page_table, cache_kt, qcol, cache_layer, seq0 = gate
    n_chunks = page_table.shape[1] // GATE_CHUNK
    n_seq = steps // n_chunks
    assert n_seq * n_chunks == steps
    job = dict(layer=cache_layer, seq0=seq0, n_chunks=n_chunks, n_steps=steps)
    specs += [pl.BlockSpec(memory_space=pl.ANY),
              pl.BlockSpec((1,) + qcol.shape[1:], lambda i, pt: (seq0 + i // n_chunks, 0, 0, 0))]
    return pl.pallas_call(
        functools.partial(_ffn_body, pre=pre is not None, final=final_g is not None, job=job),
        grid_spec=pltpu.PrefetchScalarGridSpec(
            num_scalar_prefetch=1, grid=(steps,), in_specs=specs,
            out_specs=(row(d), pl.BlockSpec((1, MOBA_HEADS, LANES), lambda i, pt: (i // n_chunks, 0, 0))),
            scratch_shapes=_gate_scratch(n_chunks)),
        out_shape=(out_shape, jax.ShapeDtypeStruct((n_seq, MOBA_HEADS, LANES), jnp.int32)),
        compiler_params=_params(("arbitrary",)), name="ffn_gate")(page_table, *args, cache_kt, qcol)


def _even_prompt_body(x_ref, g_ref, wz_ref, wxbc_ref, wdt_ref, wga_ref, wgb_ref,
                      cw_ref, cb_ref, dtb_ref, alog_ref, dfull_ref, sn_ref, sel_ref,
                      cfw_ref, cfb_ref, cfg_ref, cfbeta_ref, woy_ref, woc_ref,
                      xo_ref, st_ref, xtail_ref, utail_ref,
                      xbuf, ubuf, s_ref, ybuf, *, nt):
    t = pl.program_id(1)
    T = TILE
    C = SSD_CHUNK

    @pl.when(t == 0)
    def _():
        xbuf[0:8, :] = jnp.zeros((8, SSD_CONV_DIM), F32)
        ubuf[0:32, :] = jnp.zeros((32, CF_CH), F32)
        s_ref[...] = jnp.zeros(s_ref.shape, F32)

    x = x_ref[0]
    hb = _rms(x, g_ref[...]).astype(BF16)
    z = jnp.dot(hb, wz_ref[...], preferred_element_type=F32)
    xbc = jnp.dot(hb, wxbc_ref[...], preferred_element_type=F32)
    dtr = jnp.dot(hb, wdt_ref[...], preferred_element_type=F32)
    ga = jnp.dot(hb, wga_ref[...], preferred_element_type=F32)
    gb = jnp.dot(hb, wgb_ref[...], preferred_element_type=F32)

    xbuf[8:8 + T, :] = xbc
    k0 = SSD_CONV - 1
    acc = cw_ref[k0:k0 + 1, :] * xbuf[5 + k0:5 + k0 + T, :]
    for k in range(k0):
        acc = acc + cw_ref[k:k + 1, :] * xbuf[5 + k:5 + k + T, :]
    tail8 = xbuf[T:T + 8, :]
    xtail_ref[0] = tail8
    xbuf[0:8, :] = tail8
    xc = _silu(acc + cb_ref[...])
    xs = xc[:, 0:SSD_INNER]
    bm = xc[:, SSD_INNER:SSD_INNER + LANES]
    cm = xc[:, SSD_INNER + LANES:SSD_INNER + 2 * LANES]
    dt = _softplus(dtr + dtb_ref[...])
    dta = dt * (-jnp.exp(alog_ref[...]))
    sel = sel_ref[...]

    ri = lax.broadcasted_iota(jnp.int32, (C, C), 0)
    ci = lax.broadcasted_iota(jnp.int32, (C, C), 1)
    tri = ri >= ci
    trib = tri.astype(BF16)
    low = ci < SSD_HEAD_DIM

    for c in range(T // C):
        r0 = c * C
        xs_c, bm_c, cm_c = xs[r0:r0 + C], bm[r0:r0 + C], cm[r0:r0 + C]
        dt_c = dt[r0:r0 + C]
        cum = _dot3_r(trib, _split3(dta[r0:r0 + C]))
        cum_t = cum.T
        dt_t = dt_c.T
        cum_last = cum[C - 1:C, :]
        ecum_full = _dot3_l(jnp.exp(cum), sel)
        tail_full = _dot3_l(jnp.exp(cum_last - cum) * dt_c, sel)
        dlast_full = _dot3_l(jnp.broadcast_to(jnp.exp(cum_last), (8, LANES)), sel)[0:1]
        bm_t = bm_c.T
        cmb = cm_c.astype(BF16)
        for grp in range(SSD_GROUPS):
            bm_tg = jnp.where((ri // SSD_STATE) == grp, bm_t, 0.0).astype(BF16)
            gmat = jnp.dot(cmb, bm_tg, preferred_element_type=F32)
            pairs = SSD_HEADS // SSD_GROUPS // 2
            for pp in range(pairs):
                p = grp * pairs + pp
                lanes = slice(p * LANES, (p + 1) * LANES)
                xp = xs_c[:, lanes]
                xpb = xp.astype(BF16)
                ys = []
                for e in range(2):
                    h = 2 * p + e
                    seg = cum[:, h:h + 1] - cum_t[h:h + 1, :]
                    wm = gmat * jnp.exp(jnp.where(tri, seg, NEG)) * dt_t[h:h + 1, :]
                    ys.append(jnp.dot(wm.astype(BF16), xpb, preferred_element_type=F32))
                yp = jnp.where(low, ys[0], ys[1])
                sp = s_ref[p]
                yp = yp + jnp.dot(cmb, sp.astype(BF16), preferred_element_type=F32) * ecum_full[:, lanes]
                s_ref[p] = dlast_full[:, lanes] * sp + jnp.dot(
                    bm_tg, (xp * tail_full[:, lanes]).astype(BF16), preferred_element_type=F32)
                ybuf[r0:r0 + C, lanes] = yp

    @pl.when(t == nt - 1)
    def _():
        for p in range(SSD_HEADS // 2):
            st_ref[0, p] = s_ref[p].T

    y = (ybuf[...] + dfull_ref[...] * xs) * _silu(z)
    gw = SSD_INNER // SSD_GROUPS
    yn = []
    for grp in range(SSD_GROUPS):
        yg = y[:, grp * gw:(grp + 1) * gw]
        yn.append(_rms(yg, sn_ref[:, grp * gw:(grp + 1) * gw]).astype(BF16))

    u = ga * _sigmoid(gb)
    ubuf[32:32 + T, :] = u
    base = 32 - (CF_WIDTH - 1)
    halves = []
    cw = CF_CH // 2
    for c0 in range(0, CF_CH, cw):
        cacc = None
        for r in range(8):
            offs = [o for o in range(base, base + CF_WIDTH) if o % 8 == r]
            if not offs:
                continue
            ur = ubuf[r:max(offs) + T, c0:c0 + cw]
            part = None
            for o in offs:
                term = cfw_ref[o - base:o - base + 1, c0:c0 + cw] * ur[o - r:o - r + T]
                part = term if part is None else part + term
            cacc = part if cacc is None else cacc + part
        halves.append(cacc)
    cacc = jnp.concatenate(halves, axis=1)
    tail32 = ubuf[T:T + 32, :]
    utail_ref[0] = tail32
    ubuf[0:32, :] = tail32
    c32 = cacc + cfb_ref[...]
    mu = jnp.mean(c32, axis=-1, keepdims=True)
    var = jnp.mean(jnp.square(c32 - mu), axis=-1, keepdims=True)
    c32 = _silu((c32 - mu) * lax.rsqrt(var + NORM_EPS) * cfg_ref[...] + cfbeta_ref[...])

    out = jnp.dot(c32.astype(BF16), woc_ref[...], preferred_element_type=F32)
    for grp in range(SSD_GROUPS):
        out = out + jnp.dot(yn[grp], woy_ref[grp * gw:(grp + 1) * gw, :], preferred_element_type=F32)
    xo_ref[0] = x + out


def _even_prompt(x, w):
    b, l, d = x.shape
    nt = l // TILE
    consts = [w[k] for k in ("g", "wz", "wxbc", "wdt", "wga", "wgb", "cw", "cb", "dtb", "alog", "dfull", "sn",
                             "sel", "cfw", "cfb", "cfg", "cfbeta", "woy", "woc")]
    out_shape = (jax.ShapeDtypeStruct((b, l, d), F32),
                 jax.ShapeDtypeStruct((b, SSD_HEADS // 2, LANES, LANES), F32),
                 jax.ShapeDtypeStruct((b, 8, SSD_CONV_DIM), F32),
                 jax.ShapeDtypeStruct((b, 32, CF_CH), F32))
    out_specs = (pl.BlockSpec((1, TILE, d), lambda i, t: (i, t, 0)),
                 pl.BlockSpec((1, SSD_HEADS // 2, LANES, LANES), lambda i, t: (i, 0, 0, 0)),
                 pl.BlockSpec((1, 8, SSD_CONV_DIM), lambda i, t: (i, 0, 0)),
                 pl.BlockSpec((1, 32, CF_CH), lambda i, t: (i, 0, 0)))
    xo, st, xtail, utail = pl.pallas_call(
        functools.partial(_even_prompt_body, nt=nt),
        grid=(b, nt),
        in_specs=[pl.BlockSpec((1, TILE, d), lambda i, t: (i, t, 0))] + [_const_spec(c.shape) for c in consts],
        out_specs=out_specs, out_shape=out_shape,
        scratch_shapes=[pltpu.VMEM((TILE + 8, SSD_CONV_DIM), F32), pltpu.VMEM((TILE + 32, CF_CH), F32),
                        pltpu.VMEM((SSD_HEADS // 2, LANES, LANES), F32), pltpu.VMEM((TILE, SSD_INNER), F32)],
        compiler_params=_params(("arbitrary", "arbitrary")), name="even_prompt")(x, *consts)
    half = SSD_HEADS // 2 // SSD_GROUPS
    st = st.reshape(b, SSD_HEADS // 2, 2, SSD_HEAD_DIM, SSD_GROUPS, SSD_STATE)
    state = jnp.concatenate([st[:, grp * half:(grp + 1) * half, :, :, grp, :] for grp in range(SSD_GROUPS)], axis=1)
    state = state.reshape(b, SSD_HEADS, SSD_HEAD_DIM, SSD_STATE)
    return xo, state, xtail[:, 8 - (SSD_CONV - 1):], utail[:, 32 - (CF_WIDTH - 1):]


def _hg_levels(T):
    t = np.arange(T)
    le = (t[None, :] <= t[:, None]).astype(np.float32)
    ds, ms = [], []
    m = 1
    while m < T:
        rb = (t // (2 * m)) * 2 * m + m - 1
        if m < 8:
            ds.append(le - (t[None, :] <= rb[:, None]).astype(np.float32))
        same = (t[:, None] // (2 * m)) == (t[None, :] // (2 * m))
        ms.append((same & ((t[:, None] % (2 * m)) >= m) & ((t[None, :] % (2 * m)) < m)).astype(np.float32))
        m *= 2
    return (jnp.asarray(le, BF16), jnp.asarray(np.stack(ds), BF16), jnp.asarray(np.stack(ms), BF16))


def _rope_tables(pos):
    half = ROPE_DIM // 2
    inv = ROPE_THETA ** (-np.arange(half, dtype=np.float64) / half)
    ang = pos.astype(np.float64)[:, None] * inv[None, :]
    cos, sin = np.cos(ang), np.sin(ang)
    n = pos.shape[0]
    one = np.ones((n, MOBA_HEAD_DIM - ROPE_DIM))
    zero = np.zeros((n, MOBA_HEAD_DIM - ROPE_DIM))
    zh = np.zeros((n, half))
    c = np.concatenate([cos, cos, one], axis=1)
    s1 = np.concatenate([-sin, zh, zero], axis=1)
    s2 = np.concatenate([zh, sin, zero], axis=1)
    return tuple(jnp.asarray(np.concatenate([a, a], axis=1), F32) for a in (c, s1, s2))


def _rope(x, c, s1, s2):
    outs = []
    for j in range(x.shape[1] // LANES):
        blk = x[:, j * LANES:(j + 1) * LANES]
        outs.append(blk * c + pltpu.roll(blk, LANES - ROPE_DIM // 2, 1) * s1 + pltpu.roll(blk, ROPE_DIM // 2, 1) * s2)
    return outs


def _hg_lower_bound(lb_ref, layer):
    a = lb_ref[...]
    mx = jnp.max(a, axis=0, keepdims=True)
    e = jnp.exp(a - mx)
    return jnp.sum(e[1:layer + 1], axis=0, keepdims=True) / jnp.sum(e, axis=0, keepdims=True)


def _odd_prompt_body(x_ref, g_ref, win_ref, c_ref, s1_ref, s2_ref, lb_ref, gn_ref, tril_ref, dm_ref, mm_ref,
                     q_ref, kb_ref, vt_ref, km_ref, ko_ref, vo_ref, o_ref, hs_ref,
                     st_ref, *, nt, layer):
    t = pl.program_id(1)
    T = TILE
    W = MOBA_WIDTH

    @pl.when(t == 0)
    def _():
        st_ref[...] = jnp.zeros(st_ref.shape, F32)

    x = x_ref[0]
    hb = _rms(x, g_ref[...]).astype(BF16)
    proj = jnp.dot(hb, win_ref[...], preferred_element_type=F32)

    c, s1, s2 = c_ref[...], s1_ref[...], s2_ref[...]
    qb = _rope(proj[:, 0:W], c, s1, s2)
    kb = _rope(proj[:, W:2 * W], c, s1, s2)
    v = proj[:, 2 * W:3 * W]
    for j in range(W // LANES):
        lanes = slice(j * LANES, (j + 1) * LANES)
        q_ref[0, :, lanes] = qb[j]
        kb_ref[0, :, lanes] = kb[j].astype(BF16)
        km_ref[0, 0, :, lanes] = jnp.mean(kb[j], axis=0, keepdims=True)
        kt = kb[j].T
        vt = v[:, lanes].T
        for e in range(2):
            ko_ref[0, 2 * j + e] = kt[e * 64:(e + 1) * 64, :]
            vo_ref[0, 2 * j + e] = vt[e * 64:(e + 1) * 64, :]
            r0 = (2 * j + e) * VT_ROWS
            vt_ref[0, r0:r0 + 64, :] = vt[e * 64:(e + 1) * 64, :].astype(BF16)
            vt_ref[0, r0 + 64:r0 + VT_ROWS, :] = (
                lax.broadcasted_iota(jnp.int32, (VT_ROWS - 64, T), 0) == 0).astype(BF16)

    lb = _hg_lower_bound(lb_ref, layer)
    hq = _silu(proj[:, 3 * W:3 * W + HG_WIDTH])
    f = lb + (1.0 - lb) * _sigmoid(proj[:, 3 * W + HG_WIDTH:3 * W + 2 * HG_WIDTH])
    hv = proj[:, 3 * W + 2 * HG_WIDTH:3 * W + 3 * HG_WIDTH]
    hgate = proj[:, 3 * W + 3 * HG_WIDTH:3 * W + 4 * HG_WIDTH]
    kk = 1.0 - f
    lf3 = _split3(jnp.log(f))
    cum = _dot3_r(tril_ref[...], lf3)
    nlev = mm_ref.shape[0]
    ri = lax.broadcasted_iota(jnp.int32, (T, T), 0)
    ci = lax.broadcasted_iota(jnp.int32, (T, T), 1)
    eye = ri == ci
    amat = []
    for h in range(HG_HEADS):
        lanes = slice(h * HG_K, (h + 1) * HG_K)
        diag = jnp.sum(hq[:, lanes] * kk[:, lanes], axis=1, keepdims=True)
        amat.append(jnp.where(eye, diag, 0.0))
    for lev in range(nlev):
        m = 1 << lev
        if lev < dm_ref.shape[0]:
            d = _dot3_r(dm_ref[lev], lf3)
        else:
            refs = [jnp.broadcast_to(cum[r0 + m - 1:r0 + m, :], (2 * m, cum.shape[1])) for r0 in range(0, T, 2 * m)]
            d = cum - (refs[0] if len(refs) == 1 else jnp.concatenate(refs, axis=0))
        e_all = jnp.exp(-jnp.abs(d))
        mk = mm_ref[lev].astype(F32)
        for h in range(HG_HEADS):
            lanes = slice(h * HG_K, (h + 1) * HG_K)
            a_m = _dot_nt(hq[:, lanes] * e_all[:, lanes], kk[:, lanes] * e_all[:, lanes])
            amat[h] = amat[h] + a_m * mk
    cum_last = cum[T - 1:T, :]
    ecum = jnp.exp(cum)
    ktail = kk * jnp.exp(cum_last - cum)
    elast = jnp.exp(cum_last)
    for h in range(HG_HEADS):
        lanes = slice(h * HG_K, (h + 1) * HG_K)
        vh = hv[:, lanes]
        vhb = vh.astype(BF16)
        st = st_ref[h]
        o = jnp.dot(amat[h].astype(BF16), vhb, preferred_element_type=F32)
        o = o + _dot_nt(hq[:, lanes] * ecum[:, lanes], st)
        st_new = elast[:, lanes] * st + jnp.dot(vh.T.astype(BF16), ktail[:, lanes].astype(BF16),
                                                preferred_element_type=F32)
        st_ref[h] = st_new
        o_ref[0, :, lanes] = _rms(o, gn_ref[:, lanes]) * _silu(hgate[:, lanes])

    @pl.when(t == nt - 1)
    def _():
        for h in range(HG_HEADS):
            hs_ref[0, h] = st_ref[h].T


def _odd_prompt_proj(x, w, layer):
    b, l, d = x.shape
    nt = l // TILE
    c, s1, s2 = _rope_tables(np.arange(l))
    tril, dm, mm = _hg_levels(TILE)
    consts_a = [w["g"], w["win"]]
    consts_b = [w["lb"], w["gn"], tril, dm, mm]
    tab = pl.BlockSpec((TILE, LANES), lambda i, t: (t, 0))
    W = MOBA_WIDTH
    out_shape = (jax.ShapeDtypeStruct((b, l, W), F32),
                 jax.ShapeDtypeStruct((b, l, W), BF16),
                 jax.ShapeDtypeStruct((b, MOBA_HEADS * VT_ROWS, l), BF16),
                 jax.ShapeDtypeStruct((b, nt, 1, W), F32),
                 jax.ShapeDtypeStruct((b, MOBA_HEADS, MOBA_HEAD_DIM, l), F32),
                 jax.ShapeDtypeStruct((b, MOBA_HEADS, MOBA_HEAD_DIM, l), F32),
                 jax.ShapeDtypeStruct((b, l, HG_WIDTH), F32),
                 jax.ShapeDtypeStruct((b, HG_HEADS, HG_K, HG_V), F32))
    out_specs = (pl.BlockSpec((1, TILE, W), lambda i, t: (i, t, 0)),
                 pl.BlockSpec((1, TILE, W), lambda i, t: (i, t, 0)),
                 pl.BlockSpec((1, MOBA_HEADS * VT_ROWS, TILE), lambda i, t: (i, 0, t)),
                 pl.BlockSpec((1, 1, 1, W), lambda i, t: (i, t, 0, 0)),
                 pl.BlockSpec((1, MOBA_HEADS, MOBA_HEAD_DIM, TILE), lambda i, t: (i, 0, 0, t)),
                 pl.BlockSpec((1, MOBA_HEADS, MOBA_HEAD_DIM, TILE), lambda i, t: (i, 0, 0, t)),
                 pl.BlockSpec((1, TILE, HG_WIDTH), lambda i, t: (i, t, 0)),
                 pl.BlockSpec((1, HG_HEADS, HG_K, HG_V), lambda i, t: (i, 0, 0, 0)))
    return pl.pallas_call(
        functools.partial(_odd_prompt_body, nt=nt, layer=layer),
        grid=(b, nt),
        in_specs=([pl.BlockSpec((1, TILE, d), lambda i, t: (i, t, 0))] + [_const_spec(a.shape) for a in consts_a]
                  + [tab, tab, tab] + [_const_spec(a.shape) for a in consts_b]),
        out_specs=out_specs, out_shape=out_shape,
        scratch_shapes=[pltpu.VMEM((HG_HEADS, HG_V, HG_K), F32)],
        compiler_params=_params(("arbitrary", "arbitrary")), name="odd_prompt_proj")(
            x, *consts_a, c, s1, s2, *consts_b)


def _moba_tile_body(q_ref, k_ref, vt_ref, km_ref, o_ref, sel_ref, s_ref, cm_ref, p_ref):
    j = pl.program_id(2)
    T = TILE
    Q = 2 * T
    hd = MOBA_HEAD_DIM
    q = q_ref[0]
    km = km_ref[0, 0]
    nbp = km.shape[0]
    lane = lax.broadcasted_iota(jnp.int32, (Q, LANES), 1)
    qes = [jnp.where((lane // hd) == e, q, 0.0) for e in range(2)]
    qs = [(qe * (hd ** -0.5 * math.log2(math.e))).astype(BF16) for qe in qes]
    last_blk = k_ref.shape[1] // T - 1

    def scores(b, slot, keep):
        off = pl.multiple_of(jnp.minimum(b, last_blk) * T, T)
        kk = k_ref[0, pl.ds(off, T), :]
        for e in range(2):
            s = jnp.where(keep(e), _dot_nt(kk, qs[e]), NEG)
            s_ref[slot, e] = s
            cm_ref[slot, e] = jnp.max(s, axis=0, keepdims=True)

    def selected(b):
        return lambda e: sel_ref[e, pl.ds(b, 1), :] > 0.5

    def pv(b, slot, e):
        off = pl.multiple_of(jnp.clip(b, 0, last_blk) * T, T)
        return jnp.dot(vt_ref[0, e * VT_ROWS:(e + 1) * VT_ROWS, pl.ds(off, T)], p_ref[slot, e],
                       preferred_element_type=F32)

    def softmax_step(slot, e, m, acc, prev):
        m_new = jnp.maximum(m, cm_ref[slot, e])
        p_ref[slot, e] = jnp.exp2((s_ref[slot, e] - m_new).astype(BF16))
        return m_new, jnp.exp2(m - m_new) * (acc + prev)

    p_ref[...] = jnp.zeros(p_ref.shape, BF16)

    blk = lax.broadcasted_iota(jnp.int32, (nbp, Q), 0)
    col = lax.broadcasted_iota(jnp.int32, (nbp, Q), 1)
    elig = blk < 2 * j + (col >= T).astype(jnp.int32)
    kh = km.astype(BF16)
    kl = (km - kh.astype(F32)).astype(BF16)
    for e in range(2):
        qh = qes[e].astype(BF16)
        ql = (qes[e] - qh.astype(F32)).astype(BF16)
        gate = _dot_nt(kh, qh) + _dot_nt(kh, ql) + _dot_nt(kl, qh)
        gcur = jnp.where(elig, gate, -jnp.inf)
        selm = jnp.zeros((nbp, Q), F32)
        for _ in range(MOBA_TOPK):
            mx = jnp.max(gcur, axis=0, keepdims=True)
            first = jnp.min(jnp.where(gcur == mx, blk, nbp), axis=0, keepdims=True)
            hit = blk == first
            selm = jnp.where(hit & elig, 1.0, selm)
            gcur = jnp.where(hit, -jnp.inf, gcur)
        sel_ref[e] = selm

    ri = lax.broadcasted_iota(jnp.int32, (T, Q), 0)
    ci = lax.broadcasted_iota(jnp.int32, (T, Q), 1)
    carry = []
    for e in range(2):
        carry += [jnp.full((1, Q), NEG, F32), jnp.zeros((VT_ROWS, Q), F32)]

    own_a = (ci < T) & (ri <= ci)
    scores(2 * j, 0, lambda e: own_a | (sel_ref[e, pl.ds(2 * j, 1), :] > 0.5))
    scores(2 * j + 1, 1, lambda e: (ci >= T) & (ri <= ci - T))
    for e in range(2):
        carry[2 * e], carry[2 * e + 1] = softmax_step(0, e, carry[2 * e], carry[2 * e + 1], 0.0)
    scores(0, 0, selected(0))
    for e in range(2):
        carry[2 * e], carry[2 * e + 1] = softmax_step(1, e, carry[2 * e], carry[2 * e + 1], pv(2 * j, 0, e))

    def body(u, carry):
        carry = list(carry)
        scores(2 * u + 1, 1, selected(2 * u + 1))
        for e in range(2):
            prev = pv(jnp.where(u == 0, 2 * j + 1, 2 * u - 1), 1, e)
            carry[2 * e], carry[2 * e + 1] = softmax_step(0, e, carry[2 * e], carry[2 * e + 1], prev)
        scores(2 * u + 2, 0, selected(2 * u + 2))
        for e in range(2):
            carry[2 * e], carry[2 * e + 1] = softmax_step(1, e, carry[2 * e], carry[2 * e + 1], pv(2 * u, 0, e))
        return tuple(carry)

    fin = lax.fori_loop(0, j, body, tuple(carry))
    outs = []
    for e in range(2):
        tot = fin[2 * e + 1] + pv(jnp.where(j == 0, 1, 2 * j - 1), 1, e)
        outs.append(tot[0:hd] / tot[hd:hd + 1])
    o_ref[0] = jnp.concatenate(outs, axis=0).T


def _moba_tiles(q, kb, vt, kmean):
    b, l, w = q.shape
    tq = 2 * TILE
    assert l % tq == 0
    npair = w // LANES
    nb = kmean.shape[1]
    km = kmean.reshape(b, nb, npair, LANES).transpose(0, 2, 1, 3)
    nbp = -(-nb // 8) * 8
    km = jnp.pad(km, ((0, 0), (0, 0), (0, nbp - nb), (0, 0)))
    return pl.pallas_call(
        _moba_tile_body,
        grid=(b, npair, l // tq),
        in_specs=[pl.BlockSpec((1, tq, LANES), lambda i, p, t: (i, t, p)),
                  pl.BlockSpec((1, l, LANES), lambda i, p, t: (i, 0, p)),
                  pl.BlockSpec((1, 2 * VT_ROWS, l), lambda i, p, t: (i, p, 0)),
                  pl.BlockSpec((1, 1, nbp, LANES), lambda i, p, t: (i, p, 0, 0))],
        out_specs=pl.BlockSpec((1, tq, LANES), lambda i, p, t: (i, t, p)),
        out_shape=jax.ShapeDtypeStruct((b, l, w), F32),
        scratch_shapes=[pltpu.VMEM((2, nbp, tq), F32), pltpu.VMEM((2, 2, TILE, tq), F32),
                        pltpu.VMEM((2, 2, 1, tq), F32), pltpu.VMEM((2, 2, TILE, tq), BF16)],
        compiler_params=_params(("arbitrary", "arbitrary", "arbitrary")), name="moba_prompt")(q, kb, vt, km)


def _even_sample_proj_body(x_ref, g_ref, wz_ref, wxbc_ref, wdt_ref, wga_ref, wgb_ref, cw_ref, cb_ref, dtb_ref,
                           xbuf_ref, cfw_ref, cfb_ref, cfg_ref, cfbeta_ref, ubuf_ref,
                           z_ref, xc_ref, dt_ref, xnew_ref, c_ref, unew_ref):
    x = x_ref[...]
    hb = _rms(x, g_ref[...]).astype(BF16)
    z_ref[...] = jnp.dot(hb, wz_ref[...], preferred_element_type=F32)
    xbc = jnp.dot(hb, wxbc_ref[...], preferred_element_type=F32)
    dtr = jnp.dot(hb, wdt_ref[...], preferred_element_type=F32)
    ga = jnp.dot(hb, wga_ref[...], preferred_element_type=F32)
    gb = jnp.dot(hb, wgb_ref[...], preferred_element_type=F32)
    k1 = SSD_CONV - 1
    acc = cw_ref[k1:k1 + 1, :] * xbc
    for k in range(k1):
        acc = acc + cw_ref[k:k + 1, :] * xbuf_ref[k]
        if k > 0:
            xnew_ref[k - 1] = xbuf_ref[k]
    xnew_ref[k1 - 1] = xbc
    xc_ref[...] = _silu(acc + cb_ref[...])
    dt_ref[...] = _softplus(dtr + dtb_ref[...])
    u = ga * _sigmoid(gb)
    k2 = CF_WIDTH - 1
    cacc = cfw_ref[k2:k2 + 1, :] * u
    for k in range(k2):
        cacc = cacc + cfw_ref[k:k + 1, :] * ubuf_ref[k]
        if k > 0:
            unew_ref[k - 1] = ubuf_ref[k]
    unew_ref[k2 - 1] = u
    c32 = cacc + cfb_ref[...]
    mu = jnp.mean(c32, axis=-1, keepdims=True)
    var = jnp.mean(jnp.square(c32 - mu), axis=-1, keepdims=True)
    c_ref[...] = _silu((c32 - mu) * lax.rsqrt(var + NORM_EPS) * cfg_ref[...] + cfbeta_ref[...])


STEP_ROWS = 128
HG_STEP_ROWS = 32


def _ssd_step_body(s_ref, xb_ref, dt_ref, alog_ref, b_ref, c_ref, so_ref, yt_ref):
    rows = s_ref.shape[0]
    dt = dt_ref[...]
    decay = jnp.exp(dt * (-jnp.exp(alog_ref[...])))
    bdt = b_ref[...] * dt
    cc = c_ref[...]
    for r in range(rows):
        s_new = decay[r:r + 1, :] * s_ref[r] + xb_ref[r] * bdt[r:r + 1, :]
        so_ref[r] = s_new
        yt_ref[:, r:r + 1] = jnp.sum(s_new * cc[r:r + 1, :], axis=1, keepdims=True)


def _even_sample_out_body(x_ref, y_ref, xs_ref, z_ref, dfull_ref, sn_ref, c_ref, woy_ref, woc_ref, o_ref):
    y = (y_ref[...] + dfull_ref[...] * xs_ref[...]) * _silu(z_ref[...])
    gw = SSD_INNER // SSD_GROUPS
    out = _dot(c_ref[...], woc_ref[...])
    for grp in range(SSD_GROUPS):
        lanes = slice(grp * gw, (grp + 1) * gw)
        out = out + _dot(_rms(y[:, lanes], sn_ref[:, lanes]), woy_ref[lanes, :])
    o_ref[...] = x_ref[...] + out


def _call(body, out_shape, *args, name):
    return pl.pallas_call(body, out_shape=out_shape, compiler_params=_params(None), name=name)(*args)


def _even_sample(x, w, s_ssd, buf_ssd, buf_cf):
    nb = x.shape[0]
    f = lambda *s: jax.ShapeDtypeStruct(s, F32)
    z, xc, dt, xnew, c, unew = _call(
        _even_sample_proj_body,
        (f(nb, SSD_INNER), f(nb, SSD_CONV_DIM), f(nb, LANES), f(SSD_CONV - 1, nb, SSD_CONV_DIM), f(nb, CF_CH),
         f(CF_WIDTH - 1, nb, CF_CH)),
        x, w["g"], w["wz"], w["wxbc"], w["wdt"], w["wga"], w["wgb"], w["cw"], w["cb"], w["dtb"],
        jnp.swapaxes(buf_ssd, 0, 1), w["cfw"], w["cfb"], w["cfg"], w["cfbeta"], jnp.swapaxes(buf_cf, 0, 1),
        name="even_sample_proj")
    xs = xc[:, :SSD_INNER]
    rows = nb * SSD_HEADS
    rep_heads = SSD_HEADS // SSD_GROUPS
    grp = lambda a: jnp.repeat(a.reshape(nb, SSD_GROUPS, SSD_STATE), rep_heads, axis=1).reshape(rows, SSD_STATE)
    bm = grp(xc[:, SSD_INNER:SSD_INNER + SSD_GROUPS * SSD_STATE])
    cm = grp(xc[:, SSD_INNER + SSD_GROUPS * SSD_STATE:])
    rb = STEP_ROWS
    assert rows % rb == 0
    rowspec = pl.BlockSpec((rb, SSD_STATE), lambda i: (i, 0))
    colspec = pl.BlockSpec((SSD_HEAD_DIM, rb), lambda i: (0, i))
    sspec = pl.BlockSpec((rb, SSD_HEAD_DIM, SSD_STATE), lambda i: (i, 0, 0))
    per_row = lambda a: jnp.broadcast_to(a.reshape(rows, 1), (rows, SSD_STATE))
    s_new, yt = pl.pallas_call(
        _ssd_step_body, grid=(rows // rb,),
        in_specs=[sspec, sspec, rowspec, rowspec, rowspec, rowspec],
        out_specs=(sspec, colspec),
        out_shape=(f(rows, SSD_HEAD_DIM, SSD_STATE), f(SSD_HEAD_DIM, rows)),
        compiler_params=_params(("parallel",)), name="ssd_step")(
            s_ssd.reshape(rows, SSD_HEAD_DIM, SSD_STATE),
            jnp.broadcast_to(xs.reshape(rows, SSD_HEAD_DIM, 1), (rows, SSD_HEAD_DIM, SSD_STATE)),
            per_row(dt[:, :SSD_HEADS]), per_row(jnp.tile(w["alog"][0, :SSD_HEADS], nb)), bm, cm)
    xo = _call(_even_sample_out_body, f(nb, x.shape[1]),
               x, yt.T.reshape(nb, SSD_INNER), xs, z, w["dfull"], w["sn"], c, w["woy"], w["woc"],
               name="even_sample_out")
    return (xo, s_new.reshape(s_ssd.shape), jnp.swapaxes(xnew, 0, 1), jnp.swapaxes(unew, 0, 1))


def _odd_sample_proj_body(x_ref, g_ref, win_ref, c_ref, s1_ref, s2_ref, lb_ref,
                          q_ref, k_ref, v_ref, hq_ref, f_ref, hv_ref, hg_ref, *, layer):
    W = MOBA_WIDTH
    hb = _rms(x_ref[...], g_ref[...]).astype(BF16)
    proj = jnp.dot(hb, win_ref[...], preferred_element_type=F32)
    c, s1, s2 = c_ref[...], s1_ref[...], s2_ref[...]
    qb = _rope(proj[:, 0:W], c, s1, s2)
    kb = _rope(proj[:, W:2 * W], c, s1, s2)
    for j in range(W // LANES):
        q_ref[:, j * LANES:(j + 1) * LANES] = qb[j]
        k_ref[:, j * LANES:(j + 1) * LANES] = kb[j]
    v_ref[...] = proj[:, 2 * W:3 * W]
    lb = _hg_lower_bound(lb_ref, layer)
    hq_ref[...] = _silu(proj[:, 3 * W:3 * W + HG_WIDTH])
    f_ref[...] = lb + (1.0 - lb) * _sigmoid(proj[:, 3 * W + HG_WIDTH:3 * W + 2 * HG_WIDTH])
    hv_ref[...] = proj[:, 3 * W + 2 * HG_WIDTH:3 * W + 3 * HG_WIDTH]
    hg_ref[...] = proj[:, 3 * W + 3 * HG_WIDTH:3 * W + 4 * HG_WIDTH]


def _hg_step_body(s_ref, qt_ref, ft_ref, v_ref, so_ref, o_ref):
    qt, ft, vv = qt_ref[0], ft_ref[0], v_ref[...]
    for r in range(s_ref.shape[0]):
        fc = ft[:, r:r + 1]
        s_new = fc * s_ref[r] + (1.0 - fc) * vv[r:r + 1, :]
        so_ref[r] = s_new
        o_ref[r:r + 1, :] = jnp.sum(s_new * qt[:, r:r + 1], axis=0, keepdims=True)


def _attn_copies(idx_ref, pt_ref, ck_ref, cv_ref, kbuf, vbuf, sem, layer, seq, slot):
    ppb = MOBA_BLOCK // PAGE_SIZE
    cps = []
    for h in range(MOBA_HEADS):
        for r in range(MOBA_TOPK):
            blk = idx_ref[seq, h * MOBA_TOPK + r]
            for pg in range(ppb):
                phys = pt_ref[seq, blk * ppb + pg]
                j = r * ppb + pg
                cps.append(pltpu.make_async_copy(ck_ref.at[layer, phys, h], kbuf.at[slot, h, j], sem.at[0, slot]))
                cps.append(pltpu.make_async_copy(cv_ref.at[layer, phys, h], vbuf.at[slot, h, j], sem.at[1, slot]))
    return cps


def _sample_attn_body(idx_ref, pt_ref, ck_ref, cv_ref, q_ref, kn_ref, vn_ref, o_ref, kbuf, vbuf, sem, *,
                      layer, n_seq):
    b = pl.program_id(0)
    slot = b % 2
    scale = MOBA_HEAD_DIM ** -0.5
    nsel = MOBA_TOPK * (MOBA_BLOCK // PAGE_SIZE)

    @pl.when(b == 0)
    def _():
        for cp in _attn_copies(idx_ref, pt_ref, ck_ref, cv_ref, kbuf, vbuf, sem, layer, b, slot):
            cp.start()

    @pl.when(b + 1 < n_seq)
    def _():
        for cp in _attn_copies(idx_ref, pt_ref, ck_ref, cv_ref, kbuf, vbuf, sem, layer, b + 1, 1 - slot):
            cp.start()

    for cp in _attn_copies(idx_ref, pt_ref, ck_ref, cv_ref, kbuf, vbuf, sem, layer, b, slot):
        cp.wait()

    for h in range(MOBA_HEADS):
        qc = q_ref[0, h]
        s_self = jnp.sum(qc * kn_ref[0, h], axis=0, keepdims=True) * scale
        ss = [jnp.sum(kbuf[slot, h, j] * qc, axis=0, keepdims=True) * scale for j in range(nsel)]
        m = s_self
        for s in ss:
            m = jnp.maximum(m, jnp.max(s, axis=1, keepdims=True))
        p_self = jnp.exp(s_self - m)
        l = p_self
        acc = jnp.zeros((MOBA_HEAD_DIM, LANES), F32)
        for j in range(nsel):
            p = jnp.exp(ss[j] - m)
            l = l + jnp.sum(p, axis=1, keepdims=True)
            acc = acc + vbuf[slot, h, j] * p
        o_ref[0, h] = (vn_ref[0, h] * p_self + jnp.sum(acc, axis=1, keepdims=True)) / l


def _odd_sample_out_body(x_ref, att_ref, o_ref, hg_ref, gn_ref, woa_ref, woh_ref, xo_ref):
    out = _dot(att_ref[...], woa_ref[...])
    o = o_ref[...]
    gated = []
    for h in range(HG_HEADS):
        lanes = slice(h * HG_V, (h + 1) * HG_V)
        gated.append(_rms(o[:, lanes], gn_ref[:, lanes]) * _silu(hg_ref[:, lanes]))
    out = out + _dot(jnp.concatenate(gated, axis=1), woh_ref[...])
    xo_ref[...] = x_ref[...] + out


def _odd_sample_pre(x, w, layer, s_hg, page_table, past_len):
    nb = x.shape[0]
    f = lambda *s: jax.ShapeDtypeStruct(s, F32)
    n_pages = page_table.shape[1]
    ppb = MOBA_BLOCK // PAGE_SIZE
    nblk = n_pages // ppb
    assert n_pages % ppb == 0 and nblk >= MOBA_TOPK and nblk < LANES
    assert n_pages % GATE_CHUNK == 0 and GATE_CHUNK % ppb == 0
    c, s1, s2 = _rope_tables(np.full((1,), past_len))
    q, k, v, hq, fg, hv, hg = _call(
        functools.partial(_odd_sample_proj_body, layer=layer),
        tuple(f(nb, MOBA_WIDTH) for _ in range(7)),
        x, w["g"], w["win"], c, s1, s2, w["lb"], name="odd_sample_proj")

    rows = nb * HG_HEADS
    rb = HG_STEP_ROWS
    assert rows % rb == 0
    cols = lambda a: _pad_lanes(jnp.swapaxes(a.reshape(rows // rb, rb, HG_K), 1, 2))
    sspec = pl.BlockSpec((rb, HG_K, HG_V), lambda i: (i, 0, 0))
    cspec = pl.BlockSpec((1, HG_K, LANES), lambda i: (i, 0, 0))
    rspec = pl.BlockSpec((rb, HG_V), lambda i: (i, 0))
    s_new, o = pl.pallas_call(
        _hg_step_body, grid=(rows // rb,),
        in_specs=[sspec, cspec, cspec, rspec], out_specs=(sspec, rspec),
        out_shape=(f(rows, HG_K, HG_V), f(rows, HG_V)),
        compiler_params=_params(("parallel",)), name="hg_step")(
            s_hg.reshape(rows, HG_K, HG_V), cols(hq), cols(fg), hv.reshape(rows, HG_V))

    hd = MOBA_HEAD_DIM
    col = lambda a: jnp.broadcast_to(a.reshape(nb, MOBA_HEADS, hd, 1), (nb, MOBA_HEADS, hd, LANES))
    return dict(qcol=col(q), kcol=col(k), vcol=col(v), kn=k.reshape(nb, MOBA_HEADS, 1, hd),
                vn=v.reshape(nb, MOBA_HEADS, 1, hd), o=o.reshape(nb, HG_WIDTH), hg=hg,
                state=s_new.reshape(s_hg.shape))


def _odd_sample_post(x, w, pre, idx, cache_kt, cache_vt, cache_layer, page_table):
    nb = x.shape[0]
    f = lambda *s: jax.ShapeDtypeStruct(s, F32)
    hd = MOBA_HEAD_DIM
    nsel = MOBA_TOPK * (MOBA_BLOCK // PAGE_SIZE)
    colspec = pl.BlockSpec((1, MOBA_HEADS, hd, LANES), lambda i, *_: (i, 0, 0, 0))
    att = pl.pallas_call(
        functools.partial(_sample_attn_body, layer=cache_layer, n_seq=nb),
        grid_spec=pltpu.PrefetchScalarGridSpec(
            num_scalar_prefetch=2, grid=(nb,),
            in_specs=[pl.BlockSpec(memory_space=pl.ANY), pl.BlockSpec(memory_space=pl.ANY),
                      colspec, colspec, colspec],
            out_specs=colspec,
            scratch_shapes=[pltpu.VMEM((2, MOBA_HEADS, nsel, hd, PAGE_SIZE), F32),
                            pltpu.VMEM((2, MOBA_HEADS, nsel, hd, PAGE_SIZE), F32),
                            pltpu.SemaphoreType.DMA((2, 2))]),
        out_shape=f(nb, MOBA_HEADS, hd, LANES),
        compiler_params=_params(("arbitrary",)), name="sample_attn")(
            jnp.swapaxes(idx[:, :MOBA_TOPK, :MOBA_HEADS], 1, 2).reshape(nb, MOBA_HEADS * MOBA_TOPK),
            page_table, cache_kt, cache_vt,
            pre["qcol"], pre["kcol"], pre["vcol"])
    att = att[:, :, :, 0].reshape(nb, MOBA_WIDTH)
    return _call(_odd_sample_out_body, f(nb, x.shape[1]),
                 x, att, pre["o"], pre["hg"], w["gn"], w["woa"], w["woh"], name="odd_sample_out")


def _row(a):
    return a.reshape(1, -1).astype(F32)


def _pad_lanes(a, n=LANES):
    return jnp.pad(a, [(0, 0)] * (a.ndim - 1) + [(0, n - a.shape[-1])])


def kernel(x_prompt, x_sample, state_ssd, state_ssd_conv, state_cf_conv, cache_k, cache_v, page_table, state_hg,
           ffn1_norm, ffn1_w_gu, ffn1_w_down, mix_norm, ffn2_norm, ffn2_w_gu, ffn2_w_down, final_norm,
           even_w_in, ssd_conv_w, ssd_conv_b, ssd_dt_bias, ssd_a_log, ssd_d, ssd_norm,
           cf_dw_w, cf_dw_b, cf_ln_g, cf_ln_b, even_w_out,
           odd_w_in, hg_lower_bound, hg_norm, odd_w_out):
    depth = ffn1_norm.shape[0]
    bp, lp, d = x_prompt.shape
    nb = x_sample.shape[0]
    assert x_sample.shape[1] == 1 and lp % (2 * TILE) == 0
    past_len = page_table.shape[1] * PAGE_SIZE

    ffn_bf16 = [a.astype(BF16) for a in (ffn1_w_gu, ffn1_w_down, ffn2_w_gu, ffn2_w_down)]

    def ffn_w(norm, w_gu, w_down, l):
        return (_row(norm[l]), w_gu, w_down, l)

    def even_w(e, l):
        wi = even_w_in[e]
        o1 = SSD_INNER
        o2 = o1 + SSD_CONV_DIM
        o3 = o2 + SSD_HEADS
        sel = (np.arange(LANES)[:, None] == np.arange(SSD_INNER)[None, :] // SSD_HEAD_DIM).astype(np.float32)
        return dict(
            g=_row(mix_norm[l]), wz=wi[:, :o1].astype(BF16), wxbc=wi[:, o1:o2].astype(BF16),
            wdt=_pad_lanes(wi[:, o2:o3]).astype(BF16), wga=wi[:, o3:o3 + CF_CH].astype(BF16),
            wgb=wi[:, o3 + CF_CH:].astype(BF16), cw=ssd_conv_w[e], cb=_row(ssd_conv_b[e]),
            dtb=_pad_lanes(_row(ssd_dt_bias[e])), alog=_pad_lanes(_row(ssd_a_log[e])),
            dfull=_row(jnp.repeat(ssd_d[e], SSD_HEAD_DIM)), sn=_row(ssd_norm[e]), sel=jnp.asarray(sel, BF16),
            cfw=cf_dw_w[e], cfb=_row(cf_dw_b[e]), cfg=_row(cf_ln_g[e]), cfbeta=_row(cf_ln_b[e]),
            woy=even_w_out[e, :SSD_INNER].astype(BF16), woc=even_w_out[e, SSD_INNER:].astype(BF16))

    def odd_w(o, l):
        return dict(g=_row(mix_norm[l]), win=odd_w_in[o].astype(BF16), lb=hg_lower_bound.astype(F32),
                    gn=_row(hg_norm[o]), woa=odd_w_out[o, :MOBA_WIDTH].astype(BF16),
                    woh=odd_w_out[o, MOBA_WIDTH:].astype(BF16))

    cache_kt = jnp.swapaxes(cache_k, -1, -2)
    cache_vt = jnp.swapaxes(cache_v, -1, -2)
    layer_w = []
    for l in range(depth):
        mix = even_w(l // 2, l) if l % 2 == 0 else odd_w(l // 2, l)
        layer_w.append((ffn_w(ffn1_norm, ffn_bf16[0], ffn_bf16[1], l), mix,
                        ffn_w(ffn2_norm, ffn_bf16[2], ffn_bf16[3], l), _row(final_norm) if l == depth - 1 else None))

    steps = bp * lp // TILE
    n_chunks = page_table.shape[1] // GATE_CHUNK
    host_layer = depth - 1 if depth % 2 == 0 else depth - 2
    if host_layer < 1 or steps % n_chunks or 2 * depth * (steps // n_chunks) != nb:
        host_layer = None

    def sample_group(hosted_gate):
        xs = x_sample.reshape(nb, d)
        ssd_s, sc_s, cf_s, k_s, v_s, hg_s = [], [], [], [], [], []
        for l, (w1, w, w2, fin) in enumerate(layer_w):
            xs = _ffn(xs, w1, tm=nb)
            if l % 2 == 0:
                e = l // 2
                xs, st, xt, ut = _even_sample(xs, w, state_ssd[e], state_ssd_conv[e], state_cf_conv[e])
                ssd_s.append(st)
                sc_s.append(xt)
                cf_s.append(ut)
            else:
                o = l // 2
                pre = _odd_sample_pre(xs, w, l, state_hg[o], page_table, past_len)
                if l == host_layer:
                    idx = hosted_gate(pre["qcol"], o)
                else:
                    idx = _sample_gate(page_table, cache_kt, pre["qcol"], o)
                xs = _odd_sample_post(xs, w, pre, idx, cache_kt, cache_vt, o, page_table)
                k_s.append(pre["kn"])
                v_s.append(pre["vn"])
                hg_s.append(pre["state"])
            xs = _ffn(xs, w2, tm=nb, final_g=fin)
        return xs.reshape(nb, 1, d), ssd_s, sc_s, cf_s, k_s, v_s, hg_s

    def prompt_group(hosted):
        xp = x_prompt.reshape(bp * lp, d)
        ssd_p, sc_p, cf_p, k_p, v_p, hg_p, ids = [], [], [], [], [], [], []

        def ffn(x, *args, **kw):
            if hosted is None:
                return _ffn(x, *args, **kw)
            qcol, cache_layer = hosted
            x, part = _ffn(x, *args, gate=(page_table, cache_kt, qcol, cache_layer, len(ids) * (steps // n_chunks)),
                           **kw)
            ids.append(part)
            return x

        for l, (w1, w, w2, fin) in enumerate(layer_w):
            xp = ffn(xp, w1, tm=TILE)
            if l % 2 == 0:
                xp3, st, xt, ut = _even_prompt(xp.reshape(bp, lp, d), w)
                xp = xp3.reshape(bp * lp, d)
                ssd_p.append(st)
                sc_p.append(xt)
                cf_p.append(ut)
                xp = ffn(xp, w2, tm=TILE, final_g=fin)
            else:
                q, kb, vt, kmean, ko, vo, ohg, hst = _odd_prompt_proj(xp.reshape(bp, lp, d), w, l)
                att = _moba_tiles(q, kb, vt, kmean[:, :, 0, :])
                k_p.append(jnp.swapaxes(ko, -1, -2))
                v_p.append(jnp.swapaxes(vo, -1, -2))
                hg_p.append(hst)
                xp = ffn(xp, w2, tm=TILE, final_g=fin, pre=(att.reshape(bp * lp, MOBA_WIDTH),
                                                            ohg.reshape(bp * lp, HG_WIDTH), w["woa"], w["woh"]))
        return (xp.reshape(bp, lp, d), ssd_p, sc_p, cf_p, k_p, v_p, hg_p), ids

    prompt_result = []

    def hosted_gate(qcol, cache_layer):
        out, ids = prompt_group((qcol, cache_layer))
        prompt_result.append(out)
        return jnp.concatenate(ids, axis=0)

    sample_out = sample_group(hosted_gate)
    prompt_out = prompt_result[0] if prompt_result else prompt_group(None)[0]
    y_p, ssd_p, sc_p, cf_p, k_p, v_p, hg_p = prompt_out
    y_s, ssd_s, sc_s, cf_s, k_s, v_s, hg_s = sample_out
    st = jnp.stack
    return (y_p, y_s, st(ssd_p), st(ssd_s), st(sc_p), st(sc_s), st(cf_p), st(cf_s),
            st(k_p), st(k_s), st(v_p), st(v_s), st(hg_p), st(hg_s))
```
